```python
import jax, jax.numpy as jnp
from jax import lax
import numpy as np

D_MODEL = 2048
BATCH = 1
SEQ = 16384
DEPTH = 1

D_FF = 5632
RMS_EPS = 1e-6
ROPE_THETA = 500000.0
A_HEAD_DIM = 128
A_HEADS = D_MODEL // (2 * A_HEAD_DIM)
A_WIDTH = A_HEADS * A_HEAD_DIM
ROPE_DIM = A_HEAD_DIM // 4
DILATED_PATTERNS = ((128, 1), (512, 4), (2048, 16))
A_BLOCK = 128
MAX_DILATION = 16
B_HEADS = 4
B_VAL_DIM = D_MODEL // (2 * B_HEADS)
B_KEY_DIM = B_VAL_DIM // 2
B_KEY_WIDTH = B_HEADS * B_KEY_DIM
B_VAL_WIDTH = B_HEADS * B_VAL_DIM
GATE_RANK = 16
GATE_NORMALIZER = 16.0
GLA_CHUNK = 64
GLA_SUB = 16
IN_SPLITS = (A_WIDTH, A_WIDTH, A_WIDTH,
             B_KEY_WIDTH, B_KEY_WIDTH, B_VAL_WIDTH, B_VAL_WIDTH, GATE_RANK,
             D_MODEL, D_MODEL)
IN_WIDTH = sum(IN_SPLITS)

kernel_name = "hybrid_dilated_gla_macaron_block"


def rms_norm(x, g):
    xf = x.astype(jnp.float32)
    y = xf * lax.rsqrt(jnp.mean(xf * xf, axis=-1, keepdims=True) + RMS_EPS)
    return (y * g.astype(jnp.float32)).astype(x.dtype)


def swiglu(h, w_gate, w_up, w_down):
    return (jax.nn.silu(h @ w_gate) * (h @ w_up)) @ w_down


def partial_rope(t, positions):
    half = ROPE_DIM // 2
    inv = jnp.power(ROPE_THETA, -(jnp.arange(half, dtype=jnp.float32) * 2.0 / ROPE_DIM))
    ang = positions.astype(jnp.float32)[..., None] * inv
    cos = jnp.cos(ang)[:, :, None, :]
    sin = jnp.sin(ang)[:, :, None, :]
    t1 = t[..., :half].astype(jnp.float32)
    t2 = t[..., half:ROPE_DIM].astype(jnp.float32)
    rot = jnp.concatenate([t1 * cos - t2 * sin, t2 * cos + t1 * sin], axis=-1).astype(t.dtype)
    return jnp.concatenate([rot, t[..., ROPE_DIM:]], axis=-1)


def dilated_pattern(q, k, v, window, dilation):
    B, S_pad, H, dh = q.shape
    span = window // dilation
    M = S_pad // dilation
    nb = M // A_BLOCK

    def to_blocks(t):
        return t.reshape(B, M, dilation, H, dh).transpose(0, 2, 3, 1, 4).reshape(
            B, dilation, H, nb, A_BLOCK, dh)

    def with_prev(t):
        prev = jnp.pad(t[:, :, :, :-1], ((0, 0), (0, 0), (0, 0), (1, 0), (0, 0), (0, 0)))
        return jnp.concatenate([prev, t], axis=4)

    qb = to_blocks(q)
    kb = with_prev(to_blocks(k))
    vb = with_prev(to_blocks(v))
    s = jnp.einsum('brhnqe,brhnke->brhnqk', qb, kb).astype(jnp.float32) * (dh ** -0.5)
    qi = jnp.arange(A_BLOCK)[:, None]
    kj = jnp.arange(2 * A_BLOCK)[None, :]
    dist = qi + A_BLOCK - kj
    band = (dist >= 0) & (dist <= span)
    blk = jnp.arange(nb)[:, None, None]
    valid = band[None] & ((blk > 0) | (kj[None] >= A_BLOCK))
    s = jnp.where(valid, s, -jnp.inf)
    lse = jax.nn.logsumexp(s, axis=-1)
    p = jnp.exp(s - lse[..., None]).astype(v.dtype)
    o = jnp.einsum('brhnqk,brhnke->brhnqe', p, vb)
    o = o.reshape(B, dilation, H, M, dh).transpose(0, 3, 1, 2, 4).reshape(B, S_pad, H, dh)
    lse = lse.reshape(B, dilation, H, M).transpose(0, 3, 1, 2).reshape(B, S_pad, H)
    return o, lse


def dilated_attention(q, k, v):
    B, S, H, dh = q.shape
    unit = A_BLOCK * MAX_DILATION
    S_pad = -(-S // unit) * unit
    pad = ((0, 0), (0, S_pad - S), (0, 0), (0, 0))
    qp, kp, vp = jnp.pad(q, pad), jnp.pad(k, pad), jnp.pad(v, pad)
    outs, lses = [], []
    for window, dilation in DILATED_PATTERNS:
        o_i, lse_i = dilated_pattern(qp, kp, vp, window, dilation)
        outs.append(o_i)
        lses.append(lse_i)
    w = jax.nn.softmax(jnp.stack(lses, axis=0), axis=0)
    o = jnp.sum(w[..., None] * jnp.stack(outs, axis=0).astype(jnp.float32), axis=0)
    return o[:, :S].astype(q.dtype)


def gla_chunked(q, k, v, log_a):
    out_dtype = v.dtype
    B, S, H, dk = q.shape
    dv = v.shape[-1]
    C, Cs = GLA_CHUNK, GLA_SUB
    NS = C // Cs
    n = S // C

    def chunks(t):
        return t.astype(jnp.float32).reshape(B, n, C, H, t.shape[-1]).transpose(0, 3, 1, 2, 4)

    q, k, v, g = chunks(q), chunks(k), chunks(v), chunks(log_a)
    b = jnp.cumsum(g, axis=3)
    b_last = b[:, :, :, -1]
    k_to_end = k * jnp.exp(b_last[:, :, :, None] - b)
    upd = jnp.einsum('bhncd,bhnce->bhnde', k_to_end, v)

    def step(state, inp):
        decay, u = inp
        return decay[..., None] * state + u, state

    s0 = jnp.zeros((B, H, dk, dv), jnp.float32)
    _, s_prev = lax.scan(step, s0, (jnp.moveaxis(jnp.exp(b_last), 2, 0), jnp.moveaxis(upd, 2, 0)))
    s_prev = jnp.moveaxis(s_prev, 0, 2)
    o_inter = jnp.einsum('bhncd,bhnde->bhnce', q * jnp.exp(b), s_prev)
    qs = q.reshape(B, H, n, NS, Cs, dk)
    ksub = k.reshape(B, H, n, NS, Cs, dk)
    vs = v.reshape(B, H, n, NS, Cs, dv)
    bs = b.reshape(B, H, n, NS, Cs, dk)
    ref = jnp.concatenate([jnp.zeros_like(bs[:, :, :, :1, 0]), bs[:, :, :, :-1, -1]], axis=3)
    q_ref = qs * jnp.exp(bs - ref[:, :, :, :, None])
    k_ref = k[:, :, :, None] * jnp.exp(jnp.minimum(ref[:, :, :, :, None] - b[:, :, :, None], 0.0))
    a_cross = jnp.einsum('bhnsid,bhnsjd->bhnsij', q_ref, k_ref)
    cross_mask = jnp.arange(C)[None, None, :] < (jnp.arange(NS) * Cs)[:, None, None]
    a_cross = jnp.where(cross_mask, a_cross, 0.0)
    o_cross = jnp.einsum('bhnsij,bhnje->bhnsie', a_cross, v)
    tri = jnp.tril(jnp.ones((Cs, Cs), dtype=bool))
    diff = bs[:, :, :, :, :, None, :] - bs[:, :, :, :, None, :, :]
    decay = jnp.exp(jnp.where(tri[:, :, None], diff, -jnp.inf))
    a_diag = jnp.einsum('bhnsid,bhnsjd,bhnsijd->bhnsij', qs, ksub, decay)
    o_diag = jnp.einsum('bhnsij,bhnsje->bhnsie', a_diag, vs)
    o = o_inter + (o_cross + o_diag).reshape(B, H, n, C, dv)
    return o.transpose(0, 2, 3, 1, 4).reshape(B, S, H, dv).astype(out_dtype)


def setup_inputs(seed: int = 0) -> dict:
    key = jax.random.key(seed)
    ks = jax.random.split(key, 21)
    f32 = jnp.float32
    L = DEPTH

    def dense(k, shape, fan_in):
        return jax.random.normal(k, shape, f32) * (fan_in ** -0.5)

    def gain(k, dim):
        return 1.0 + 0.02 * jax.random.normal(k, (L, dim), f32)

    return {
        "x": jax.random.normal(ks[0], (BATCH, SEQ, D_MODEL), f32),
        "positions": jnp.broadcast_to(jnp.arange(SEQ, dtype=jnp.int32), (BATCH, SEQ)),
        "ffn1_norm": gain(ks[1], D_MODEL),
        "ffn1_w_gate": dense(ks[2], (L, D_MODEL, D_FF), D_MODEL),
        "ffn1_w_up": dense(ks[3], (L, D_MODEL, D_FF), D_MODEL),
        "ffn1_w_down": dense(ks[4], (L, D_FF, D_MODEL), D_FF),
        "mix_norm": gain(ks[5], D_MODEL),
        "w_in": dense(ks[6], (L, D_MODEL, IN_WIDTH), D_MODEL),
        "a_q_norm": gain(ks[7], A_HEAD_DIM),
        "a_k_norm": gain(ks[8], A_HEAD_DIM),
        "b_gate_w2": dense(ks[9], (L, GATE_RANK, B_KEY_WIDTH), GATE_RANK),
        "b_gate_bias": 0.1 * jax.random.normal(ks[10], (L, B_KEY_WIDTH), f32),
        "b_out_norm": gain(ks[11], B_VAL_DIM),
        "w_a_up": dense(ks[12], (L, A_WIDTH, D_MODEL), A_WIDTH),
        "w_b_up": dense(ks[13], (L, B_VAL_WIDTH, D_MODEL), B_VAL_WIDTH),
        "w_out": dense(ks[14], (L, D_MODEL, D_MODEL), D_MODEL),
        "ffn2_norm": gain(ks[15], D_MODEL),
        "ffn2_w_gate": dense(ks[16], (L, D_MODEL, D_FF), D_MODEL),
        "ffn2_w_up": dense(ks[17], (L, D_MODEL, D_FF), D_MODEL),
        "ffn2_w_down": dense(ks[18], (L, D_FF, D_MODEL), D_FF),
    }


def reference(x, positions, ffn1_norm, ffn1_w_gate, ffn1_w_up, ffn1_w_down,
              mix_norm, w_in, a_q_norm, a_k_norm, b_gate_w2, b_gate_bias, b_out_norm,
              w_a_up, w_b_up, w_out, ffn2_norm, ffn2_w_gate, ffn2_w_up, ffn2_w_down):
    B, S, _ = x.shape
    split_points = np.cumsum(IN_SPLITS)[:-1].tolist()
    for l in range(DEPTH):
        x = x + 0.5 * swiglu(rms_norm(x, ffn1_norm[l]), ffn1_w_gate[l], ffn1_w_up[l], ffn1_w_down[l])
        h = rms_norm(x, mix_norm[l])
        proj = h @ w_in[l]
        aq, ak, av, bq, bk, bv, br, bz, ga, gb = jnp.split(proj, split_points, axis=-1)
        aq = partial_rope(rms_norm(aq.reshape(B, S, A_HEADS, A_HEAD_DIM), a_q_norm[l]), positions)
        ak = partial_rope(rms_norm(ak.reshape(B, S, A_HEADS, A_HEAD_DIM), a_k_norm[l]), positions)
        av = av.reshape(B, S, A_HEADS, A_HEAD_DIM)
        o_a = dilated_attention(aq, ak, av).reshape(B, S, A_WIDTH)
        log_a = jax.nn.log_sigmoid((bz @ b_gate_w2[l] + b_gate_bias[l]).astype(jnp.float32)) / GATE_NORMALIZER
        o_b = gla_chunked(bq.reshape(B, S, B_HEADS, B_KEY_DIM) * (B_KEY_DIM ** -0.5),
                          bk.reshape(B, S, B_HEADS, B_KEY_DIM),
                          bv.reshape(B, S, B_HEADS, B_VAL_DIM),
                          log_a.reshape(B, S, B_HEADS, B_KEY_DIM))
        o_b = rms_norm(o_b, b_out_norm[l]).reshape(B, S, B_VAL_WIDTH) * jax.nn.silu(br)
        y = jax.nn.sigmoid(ga) * (o_a @ w_a_up[l]) + jax.nn.sigmoid(gb) * (o_b @ w_b_up[l])
        x = x + y @ w_out[l]
        x = x + 0.5 * swiglu(rms_norm(x, ffn2_norm[l]), ffn2_w_gate[l], ffn2_w_up[l], ffn2_w_down[l])
    return x
```

```python
import functools

import jax
import jax.numpy as jnp
import numpy as np
from jax import lax
from jax.experimental import pallas as pl
from jax.experimental.pallas import tpu as pltpu

F32 = jnp.float32
BF16 = jnp.bfloat16

D_MODEL = 2048
D_FF = 5632
RMS_EPS = 1e-6
ROPE_THETA = 500000.0
A_HEAD_DIM = 128
A_HEADS = 8
A_WIDTH = A_HEADS * A_HEAD_DIM
ROPE_DIM = A_HEAD_DIM // 4
ROPE_HALF = ROPE_DIM // 2
DILATIONS = (1, 4, 16)
A_SPAN = 128
B_HEADS = 4
B_VAL_DIM = 256
B_KEY_DIM = 128
B_KEY_WIDTH = B_HEADS * B_KEY_DIM
B_VAL_WIDTH = B_HEADS * B_VAL_DIM
GATE_RANK = 16
GATE_NORMALIZER = 16.0
GLA_CHUNK = 64
GLA_SUB = 16
LANES = 128

VMEM_LIMIT = 56 * 1024 * 1024


def _params(sem):
    return pltpu.CompilerParams(dimension_semantics=sem, vmem_limit_bytes=VMEM_LIMIT)


def _rms(x, g):
    return x * lax.rsqrt(jnp.mean(x * x, axis=-1, keepdims=True) + RMS_EPS) * g


def _dot(a, b):
    return jnp.dot(a, b, preferred_element_type=F32)


def _dot_nt(a, b):
    return lax.dot_general(a, b, (((1,), (1,)), ((), ())), preferred_element_type=F32)


def _dot_tn(a, b):
    return lax.dot_general(a, b, (((0,), (0,)), ((), ())), preferred_element_type=F32)


def _ffn_kernel(x_ref, g_ref, wg_ref, wu_ref, wd_ref, *rest, emit_next):
    if emit_next:
        gn_ref, o_ref, hn_ref, h_scr = rest
    else:
        o_ref, h_scr = rest
    f = pl.program_id(1)

    @pl.when(f == 0)
    def _():
        x = x_ref[...]
        h_scr[...] = _rms(x, g_ref[...]).astype(BF16)
        o_ref[...] = x

    h = h_scr[...]
    gate = _dot(h, wg_ref[...])
    up = _dot(h, wu_ref[...])
    act = (0.5 * (gate * jax.nn.sigmoid(gate)) * up).astype(BF16)
    o_ref[...] += _dot(act, wd_ref[...])

    if emit_next:
        @pl.when(f == pl.num_programs(1) - 1)
        def _():
            hn_ref[...] = _rms(o_ref[...], gn_ref[...]).astype(BF16)


def _ffn(x, gain, wg, wu, wd, next_gain=None, *, tm=512, tf=512):
    S, D = x.shape
    F = wg.shape[1]
    emit_next = next_gain is not None
    row = lambda i, f: (i, 0)
    fixed = lambda i, f: (0, 0)
    in_specs = [
        pl.BlockSpec((tm, D), row),
        pl.BlockSpec((1, D), fixed),
        pl.BlockSpec((D, tf), lambda i, f: (0, f)),
        pl.BlockSpec((D, tf), lambda i, f: (0, f)),
        pl.BlockSpec((tf, D), lambda i, f: (f, 0)),
    ]
    args = [x, gain, wg, wu, wd]
    out_shape = [jax.ShapeDtypeStruct((S, D), F32)]
    out_specs = [pl.BlockSpec((tm, D), row)]
    if emit_next:
        in_specs.append(pl.BlockSpec((1, D), fixed))
        args.append(next_gain)
        out_shape.append(jax.ShapeDtypeStruct((S, D), BF16))
        out_specs.append(pl.BlockSpec((tm, D), row))
    res = pl.pallas_call(
        functools.partial(_ffn_kernel, emit_next=emit_next),
        grid=(S // tm, F // tf),
        in_specs=in_specs,
        out_specs=out_specs,
        out_shape=out_shape,
        scratch_shapes=[pltpu.VMEM((tm, D), BF16)],
        compiler_params=_params(("parallel", "arbitrary")),
        name="ffn_next" if emit_next else "ffn",
    )(*args)
    return res if emit_next else res[0]


def _proj_kernel(a_ref, w_ref, o_ref):
    o_ref[...] = _dot(a_ref[...], w_ref[...]).astype(o_ref.dtype)


def _proj(a, w, out_dtype, *, tm=1024, tn=512):
    S, K = a.shape
    N = w.shape[1]
    return pl.pallas_call(
        _proj_kernel,
        grid=(S // tm, N // tn),
        in_specs=[pl.BlockSpec((tm, K), lambda i, j: (i, 0)),
                  pl.BlockSpec((K, tn), lambda i, j: (0, j))],
        out_specs=pl.BlockSpec((tm, tn), lambda i, j: (i, j)),
        out_shape=jax.ShapeDtypeStruct((S, N), out_dtype),
        compiler_params=_params(("parallel", "arbitrary")),
        name="proj_" + jnp.dtype(out_dtype).name,
    )(a, w)


def _proj_qk_kernel(a_ref, w_ref, g_ref, pos_ref, inv_ref, o_ref, cos_scr, sin_scr):
    @pl.when(pl.program_id(1) == 0)
    def _():
        ang = pos_ref[...].astype(F32) * inv_ref[...]
        lane = lax.broadcasted_iota(jnp.int32, ang.shape, 1)
        cos_scr[...] = jnp.cos(ang)
        sin_scr[...] = jnp.where(lane < ROPE_HALF, -jnp.sin(ang), jnp.sin(ang))

    acc = _dot(a_ref[...], w_ref[...])
    cosf = cos_scr[...]
    sinf = sin_scr[...]
    lane = lax.broadcasted_iota(jnp.int32, cosf.shape, 1)
    for h in range(acc.shape[1] // A_HEAD_DIM):
        cols = slice(h * A_HEAD_DIM, (h + 1) * A_HEAD_DIM)
        y = _rms(acc[:, cols], g_ref[:, cols])
        partner = jnp.where(lane < ROPE_HALF,
                            pltpu.roll(y, A_HEAD_DIM - ROPE_HALF, 1),
                            pltpu.roll(y, ROPE_HALF, 1))
        o_ref[:, cols] = (y * cosf + partner * sinf).astype(o_ref.dtype)


def _proj_qk(a, w, gains, pos, inv, *, tm=1024, tn=512):
    S, K = a.shape
    N = w.shape[1]
    return pl.pallas_call(
        _proj_qk_kernel,
        grid=(S // tm, N // tn),
        in_specs=[pl.BlockSpec((tm, K), lambda i, j: (i, 0)),
                  pl.BlockSpec((K, tn), lambda i, j: (0, j)),
                  pl.BlockSpec((1, tn), lambda i, j: (0, j)),
                  pl.BlockSpec((tm, 1), lambda i, j: (i, 0)),
                  pl.BlockSpec((1, A_HEAD_DIM), lambda i, j: (0, 0))],
        out_specs=pl.BlockSpec((tm, tn), lambda i, j: (i, j)),
        out_shape=jax.ShapeDtypeStruct((S, N), BF16),
        scratch_shapes=[pltpu.VMEM((tm, A_HEAD_DIM), F32), pltpu.VMEM((tm, A_HEAD_DIM), F32)],
        compiler_params=_params(("parallel", "arbitrary")),
        name="proj_qk",
    )(a, w, gains, pos, inv)


def _gate_kernel(a_ref, wz_ref, w2_ref, bias_ref, o_ref):
    z = _dot(a_ref[...], wz_ref[...])
    pre = _dot(z.astype(BF16), w2_ref[...]) + bias_ref[...]
    g = (jnp.minimum(pre, 0.0) - jnp.log1p(jnp.exp(-jnp.abs(pre)))) / GATE_NORMALIZER
    r = lax.broadcasted_iota(jnp.int32, g.shape, 0) % GLA_CHUNK
    shift = 1
    while shift < GLA_CHUNK:
        g = g + jnp.where(r >= shift, pltpu.roll(g, shift, 0), 0.0)
        shift *= 2
    o_ref[...] = g


def _gate(a, wz, w2, bias, *, tm=1024):
    S, K = a.shape
    return pl.pallas_call(
        _gate_kernel,
        grid=(S // tm,),
        in_specs=[pl.BlockSpec((tm, K), lambda i: (i, 0)),
                  pl.BlockSpec(wz.shape, lambda i: (0, 0)),
                  pl.BlockSpec(w2.shape, lambda i: (0, 0)),
                  pl.BlockSpec(bias.shape, lambda i: (0, 0))],
        out_specs=pl.BlockSpec((tm, B_KEY_WIDTH), lambda i: (i, 0)),
        out_shape=jax.ShapeDtypeStruct((S, B_KEY_WIDTH), F32),
        compiler_params=_params(("parallel",)),
        name="gate",
    )(a, wz, w2, bias)


A_BLK = 128
ML_L = A_HEADS


def _attn_kernel(*refs, first, last, nsub):
    q_ref, kp_ref, kc_ref, vp_ref, vc_ref = refs[:5]
    refs = refs[5:]
    if not first:
        acc_in_ref, ml_in_ref = refs[:2]
        refs = refs[2:]
    if last:
        (o_ref,) = refs
    else:
        acc_out_ref, ml_out_ref = refs
    n = pl.program_id(1)
    scale = A_HEAD_DIM ** -0.5
    row = lax.broadcasted_iota(jnp.int32, (A_BLK, A_BLK), 0)
    col = lax.broadcasted_iota(jnp.int32, (A_BLK, A_BLK), 1)
    mask_prev = col >= row
    mask_cur = row >= col
    for sb in range(nsub):
        rows = slice(sb * A_BLK, (sb + 1) * A_BLK)
        prows = slice((sb - 1) * A_BLK, sb * A_BLK)
        if sb == 0:
            mprev = jnp.logical_and(mask_prev, n > 0)
        else:
            mprev = mask_prev
        if not first:
            ml_in = ml_in_ref[rows, :]
        ml_new = jnp.zeros((A_BLK, LANES), F32)
        for h in range(A_HEADS):
            cols = slice(h * A_HEAD_DIM, (h + 1) * A_HEAD_DIM)
            q = q_ref[rows, cols]
            if sb == 0:
                kp, vp = kp_ref[:, cols], vp_ref[:, cols]
            else:
                kp, vp = kc_ref[prows, cols], vc_ref[prows, cols]
            kc, vc = kc_ref[rows, cols], vc_ref[rows, cols]
            sp = jnp.where(mprev, _dot_nt(q, kp) * scale, -jnp.inf)
            sc = jnp.where(mask_cur, _dot_nt(q, kc) * scale, -jnp.inf)
            m = jnp.maximum(jnp.max(sp, axis=-1, keepdims=True), jnp.max(sc, axis=-1, keepdims=True))
            if not first:
                m_in = ml_in[:, h:h + 1]
                l_in = ml_in[:, ML_L + h:ML_L + h + 1]
                m = jnp.maximum(m, m_in)
            pp = jnp.exp(sp - m)
            pc = jnp.exp(sc - m)
            l = jnp.sum(pp, axis=-1, keepdims=True) + jnp.sum(pc, axis=-1, keepdims=True)
            acc = _dot(pp.astype(BF16), vp) + _dot(pc.astype(BF16), vc)
            if not first:
                alpha = jnp.exp(m_in - m)
                l = l + alpha * l_in
                acc = acc + alpha * acc_in_ref[rows, cols]
            if last:
                o_ref[rows, cols] = (acc / l).astype(o_ref.dtype)
            else:
                acc_out_ref[rows, cols] = acc
                lane = lax.broadcasted_iota(jnp.int32, (A_BLK, LANES), 1)
                ml_new = jnp.where(lane == h, m, jnp.where(lane == ML_L + h, l, ml_new))
        if not last:
            ml_out_ref[rows, :] = ml_new


def _attn_pattern(pqk, pv, state, d, *, last, nsub=2):
    S = pqk.shape[0]
    M = S // d
    T = nsub * A_BLK
    first = state is None
    qk_v = pqk.reshape(M, d * 2 * A_WIDTH)
    v_v = pv.reshape(M, d * pv.shape[1])
    vblocks = pv.shape[1] // A_WIDTH
    prev = lambda r, n: jnp.maximum(n * nsub - 1, 0)
    in_specs = [
        pl.BlockSpec((T, A_WIDTH), lambda r, n: (n, 2 * r)),
        pl.BlockSpec((A_BLK, A_WIDTH), lambda r, n: (prev(r, n), 2 * r + 1)),
        pl.BlockSpec((T, A_WIDTH), lambda r, n: (n, 2 * r + 1)),
        pl.BlockSpec((A_BLK, A_WIDTH), lambda r, n: (prev(r, n), vblocks * r)),
        pl.BlockSpec((T, A_WIDTH), lambda r, n: (n, vblocks * r)),
    ]
    args = [qk_v, qk_v, qk_v, v_v, v_v]
    acc_spec = pl.BlockSpec((T, A_WIDTH), lambda r, n: (n, r))
    ml_spec = pl.BlockSpec((T, LANES), lambda r, n: (n, r))
    if not first:
        acc_in, ml_in = state
        in_specs += [acc_spec, ml_spec]
        args += [acc_in.reshape(M, d * A_WIDTH), ml_in.reshape(M, d * LANES)]
    if last:
        out_shape = [jax.ShapeDtypeStruct((M, d * A_WIDTH), BF16)]
        out_specs = [acc_spec]
    else:
        out_shape = [jax.ShapeDtypeStruct((M, d * A_WIDTH), F32),
                     jax.ShapeDtypeStruct((M, d * LANES), F32)]
        out_specs = [acc_spec, ml_spec]
    res = pl.pallas_call(
        functools.partial(_attn_kernel, first=first, last=last, nsub=nsub),
        grid=(d, M // T),
        in_specs=in_specs,
        out_specs=out_specs,
        out_shape=out_shape,
        compiler_params=_params(("parallel", "arbitrary")),
        name=f"attn_d{d}",
    )(*args)
    if last:
        return res[0].reshape(S, A_WIDTH)
    return res[0].reshape(S, A_WIDTH), res[1].reshape(S, LANES)


def _gla_kernel(q_ref, k_ref, v_ref, b_ref, r_ref, gn_ref, o_ref, st_ref, *, nchunk):
    C, Cs = GLA_CHUNK, GLA_SUB
    NS = C // Cs

    @pl.when(pl.program_id(0) == 0)
    def _():
        st_ref[...] = jnp.zeros_like(st_ref)

    row = lax.broadcasted_iota(jnp.int32, (C, C), 0)
    col = lax.broadcasted_iota(jnp.int32, (C, C), 1)
    sub_start = (row // Cs) * Cs
    sub_row = lax.broadcasted_iota(jnp.int32, (NS, Cs, B_KEY_DIM), 1)
    for c in range(nchunk):
        rows = slice(c * C, (c + 1) * C)
        for h in range(B_HEADS):
            kcols = slice(h * B_KEY_DIM, (h + 1) * B_KEY_DIM)
            vcols = slice(h * B_VAL_DIM, (h + 1) * B_VAL_DIM)
            q = q_ref[rows, kcols] * (B_KEY_DIM ** -0.5)
            k = k_ref[rows, kcols]
            b = b_ref[rows, kcols]
            v = v_ref[rows, vcols]
            b_last = b[C - 1:C, :]
            st = st_ref[h]
            o = _dot_nt((q * jnp.exp(b)).astype(BF16), st.astype(BF16))
            k_end = (k * jnp.exp(b_last - b)).astype(BF16)
            st_ref[h] = st * jnp.exp(b_last) + _dot_tn(v, k_end)

            blocks = [jnp.zeros((Cs, C), F32)]
            for s in range(1, NS):
                srows = slice(s * Cs, (s + 1) * Cs)
                ref_s = b[s * Cs - 1:s * Cs, :]
                q_ref_s = (q[srows, :] * jnp.exp(b[srows, :] - ref_s)).astype(BF16)
                k_ref_s = (k * jnp.exp(jnp.minimum(ref_s - b, 0.0))).astype(BF16)
                blocks.append(_dot_nt(q_ref_s, k_ref_s))
            a = jnp.where(col < sub_start, jnp.concatenate(blocks, axis=0), 0.0)

            q3 = q.reshape(NS, Cs, B_KEY_DIM)
            k3 = k.reshape(NS, Cs, B_KEY_DIM)
            b3 = b.reshape(NS, Cs, B_KEY_DIM)
            for j in range(Cs):
                kb = jnp.broadcast_to(k3[:, j:j + 1, :], q3.shape)
                bb = jnp.broadcast_to(b3[:, j:j + 1, :], q3.shape)
                w = jnp.exp(jnp.where(sub_row >= j, b3 - bb, -jnp.inf))
                dj = jnp.sum((q3 * kb * w).reshape(C, B_KEY_DIM), axis=-1, keepdims=True)
                a = jnp.where(col == sub_start + j, dj, a)
            o = o + _dot(a.astype(BF16), v)
            r = r_ref[rows, vcols]
            o_ref[rows, vcols] = (_rms(o, gn_ref[...]) * (r * jax.nn.sigmoid(r))).astype(o_ref.dtype)


def _gla(pf, pv, b, gain, *, rb=256):
    S = pf.shape[0]
    return pl.pallas_call(
        functools.partial(_gla_kernel, nchunk=rb // GLA_CHUNK),
        grid=(S // rb,),
        in_specs=[pl.BlockSpec((rb, B_KEY_WIDTH), lambda i: (i, 0)),
                  pl.BlockSpec((rb, B_KEY_WIDTH), lambda i: (i, 1)),
                  pl.BlockSpec((rb, B_VAL_WIDTH), lambda i: (i, 1)),
                  pl.BlockSpec((rb, B_KEY_WIDTH), lambda i: (i, 0)),
                  pl.BlockSpec((rb, B_VAL_WIDTH), lambda i: (i, 1)),
                  pl.BlockSpec((1, B_VAL_DIM), lambda i: (0, 0))],
        out_specs=pl.BlockSpec((rb, B_VAL_WIDTH), lambda i: (i, 0)),
        out_shape=jax.ShapeDtypeStruct((S, B_VAL_WIDTH), BF16),
        scratch_shapes=[pltpu.VMEM((B_HEADS, B_VAL_DIM, B_KEY_DIM), F32)],
        compiler_params=_params(("arbitrary",)),
        name="gla",
    )(pf, pf, pv, b, pf, gain)


def _merge_out_kernel(x_ref, oa_ref, ob_ref, ga_ref, gb_ref, wa_ref, wb_ref, wo_ref, o_ref):
    @pl.when(pl.program_id(1) == 0)
    def _():
        o_ref[...] = x_ref[...]

    ya = _dot(oa_ref[...], wa_ref[...])
    yb = _dot(ob_ref[...], wb_ref[...])
    y = jax.nn.sigmoid(ga_ref[...]) * ya + jax.nn.sigmoid(gb_ref[...]) * yb
    o_ref[...] += _dot(y.astype(BF16), wo_ref[...])


def _merge_out(x, oa, ob, pf, wa, wb, wo, *, tm=512, tn=512):
    S, D = x.shape
    ga_blk = 2048 // tn
    gb_blk = 4096 // tn
    row = lambda i, j: (i, 0)
    return pl.pallas_call(
        _merge_out_kernel,
        grid=(S // tm, D // tn),
        in_specs=[pl.BlockSpec((tm, D), row),
                  pl.BlockSpec((tm, A_WIDTH), row),
                  pl.BlockSpec((tm, B_VAL_WIDTH), row),
                  pl.BlockSpec((tm, tn), lambda i, j: (i, ga_blk + j)),
                  pl.BlockSpec((tm, tn), lambda i, j: (i, gb_blk + j)),
                  pl.BlockSpec((A_WIDTH, tn), lambda i, j: (0, j)),
                  pl.BlockSpec((B_VAL_WIDTH, tn), lambda i, j: (0, j)),
                  pl.BlockSpec((tn, D), lambda i, j: (j, 0))],
        out_specs=pl.BlockSpec((tm, D), row),
        out_shape=jax.ShapeDtypeStruct((S, D), F32),
        compiler_params=_params(("parallel", "arbitrary")),
        name="merge_out",
    )(x, oa, ob, pf, pf, wa, wb, wo)


def _layer(x, pos, p):
    x1, h = _ffn(x, p["ffn1_norm"], p["ffn1_wg"], p["ffn1_wu"], p["ffn1_wd"], p["mix_norm"])
    pqk = _proj_qk(h, p["w_qk"], p["qk_gain"], pos, p["rope_inv"])
    pv = _proj(h, p["w_v"], BF16)
    pf = _proj(h, p["w_f"], F32)
    b = _gate(h, p["w_z"], p["w_2"], p["gate_bias"])
    state = None
    for d in DILATIONS[:-1]:
        state = _attn_pattern(pqk, pv, state, d, last=False)
    o_a = _attn_pattern(pqk, pv, state, DILATIONS[-1], last=True)
    o_b = _gla(pf, pv, b, p["b_out_norm"])
    x2 = _merge_out(x1, o_a, o_b, pf, p["w_a_up"], p["w_b_up"], p["w_out"])
    return _ffn(x2, p["ffn2_norm"], p["ffn2_wg"], p["ffn2_wu"], p["ffn2_wd"])


def kernel(x, positions, ffn1_norm, ffn1_w_gate, ffn1_w_up, ffn1_w_down, mix_norm, w_in, a_q_norm, a_k_norm, b_gate_w2, b_gate_bias, b_out_norm, w_a_up, w_b_up, w_out, ffn2_norm, ffn2_w_gate, ffn2_w_up, ffn2_w_down):
    B, S, D = x.shape
    assert D == D_MODEL and S % (A_BLK * max(DILATIONS) * 2) == 0
    depth = w_in.shape[0]
    c = np.cumsum([0, A_WIDTH, A_WIDTH, A_WIDTH, B_KEY_WIDTH, B_KEY_WIDTH, B_VAL_WIDTH, B_VAL_WIDTH,
                   GATE_RANK, D_MODEL, D_MODEL]).tolist()
    inv = jnp.power(ROPE_THETA, -(jnp.arange(ROPE_HALF, dtype=F32) * 2.0 / ROPE_DIM))
    rope_inv = jnp.concatenate([inv, inv, jnp.zeros((A_HEAD_DIM - ROPE_DIM,), F32)])[None, :]
    outs = []
    for bi in range(B):
        xb = x[bi]
        pos = positions[bi][:, None]
        for l in range(depth):
            w = w_in[l]
            p = {
                "ffn1_norm": ffn1_norm[l][None, :], "mix_norm": mix_norm[l][None, :],
                "ffn2_norm": ffn2_norm[l][None, :],
                "ffn1_wg": ffn1_w_gate[l].astype(BF16), "ffn1_wu": ffn1_w_up[l].astype(BF16),
                "ffn1_wd": ffn1_w_down[l].astype(BF16),
                "ffn2_wg": ffn2_w_gate[l].astype(BF16), "ffn2_wu": ffn2_w_up[l].astype(BF16),
                "ffn2_wd": ffn2_w_down[l].astype(BF16),
                "w_qk": w[:, c[0]:c[2]].astype(BF16),
                "w_v": jnp.concatenate([w[:, c[2]:c[3]], w[:, c[5]:c[6]]], axis=1).astype(BF16),
                "w_f": jnp.concatenate([w[:, c[3]:c[5]], w[:, c[6]:c[7]], w[:, c[8]:c[10]]], axis=1).astype(BF16),
                "w_z": jnp.pad(w[:, c[7]:c[8]], ((0, 0), (0, LANES - GATE_RANK))).astype(BF16),
                "w_2": jnp.pad(b_gate_w2[l], ((0, LANES - GATE_RANK), (0, 0))).astype(BF16),
                "gate_bias": b_gate_bias[l][None, :],
                "qk_gain": jnp.concatenate([jnp.tile(a_q_norm[l], A_HEADS), jnp.tile(a_k_norm[l], A_HEADS)])[None, :],
                "rope_inv": rope_inv,
                "b_out_norm": b_out_norm[l][None, :],
                "w_a_up": w_a_up[l].astype(BF16), "w_b_up": w_b_up[l].astype(BF16), "w_out": w_out[l].astype(BF16),
            }
            xb = _layer(xb, pos, p)
        outs.append(xb)
    return jnp.stack(outs, axis=0)
```

```python
import functools

import jax
import jax.numpy as jnp
import numpy as np
from jax import lax
from jax.experimental import pallas as pl
from jax.experimental.pallas import tpu as pltpu

F32 = jnp.float32
BF16 = jnp.bfloat16

D_MODEL = 2048
D_FF = 5632
RMS_EPS = 1e-6
ROPE_THETA = 500000.0
A_HEAD_DIM = 128
A_HEADS = 8
A_WIDTH = A_HEADS * A_HEAD_DIM
ROPE_DIM = A_HEAD_DIM // 4
ROPE_HALF = ROPE_DIM // 2
DILATIONS = (1, 4, 16)
A_SPAN = 128
B_HEADS = 4
B_VAL_DIM = 256
B_KEY_DIM = 128
B_KEY_WIDTH = B_HEADS * B_KEY_DIM
B_VAL_WIDTH = B_HEADS * B_VAL_DIM
GATE_RANK = 16
GATE_NORMALIZER = 16.0
GLA_CHUNK = 64
GLA_SUB = 16
LANES = 128
MXU_N = 256

VMEM_LIMIT = 56 * 1024 * 1024


def _params(sem):
    return pltpu.CompilerParams(dimension_semantics=sem, vmem_limit_bytes=VMEM_LIMIT)


def _rms(x, g):
    return x * lax.rsqrt(jnp.mean(x * x, axis=-1, keepdims=True) + RMS_EPS) * g


def _dot(a, b):
    return jnp.dot(a, b, preferred_element_type=F32)


def _dot_nt(a, b):
    return lax.dot_general(a, b, (((1,), (1,)), ((), ())), preferred_element_type=F32)


def _dot_tn(a, b):
    return lax.dot_general(a, b, (((0,), (0,)), ((), ())), preferred_element_type=F32)


def _ffn_kernel(x_ref, g_ref, wg_ref, wu_ref, wd_ref, *rest, emit_next):
    if emit_next:
        gn_ref, o_ref, hn_ref, h_scr = rest
    else:
        o_ref, h_scr = rest
    f = pl.program_id(1)

    @pl.when(f == 0)
    def _():
        x = x_ref[...]
        h_scr[...] = _rms(x, g_ref[...]).astype(BF16)
        o_ref[...] = x

    h = h_scr[...]
    gate = _dot(h, wg_ref[...])
    up = _dot(h, wu_ref[...])
    act = (0.5 * (gate * jax.nn.sigmoid(gate)) * up).astype(BF16)
    o_ref[...] += _dot(act, wd_ref[...])

    if emit_next:
        @pl.when(f == pl.num_programs(1) - 1)
        def _():
            hn_ref[...] = _rms(o_ref[...], gn_ref[...]).astype(BF16)


def _ffn(x, gain, wg, wu, wd, next_gain=None, *, tm=512, tf=512):
    S, D = x.shape
    F = wg.shape[1]
    emit_next = next_gain is not None
    row = lambda i, f: (i, 0)
    fixed = lambda i, f: (0, 0)
    in_specs = [
        pl.BlockSpec((tm, D), row),
        pl.BlockSpec((1, D), fixed),
        pl.BlockSpec((D, tf), lambda i, f: (0, f)),
        pl.BlockSpec((D, tf), lambda i, f: (0, f)),
        pl.BlockSpec((tf, D), lambda i, f: (f, 0)),
    ]
    args = [x, gain, wg, wu, wd]
    out_shape = [jax.ShapeDtypeStruct((S, D), F32)]
    out_specs = [pl.BlockSpec((tm, D), row)]
    if emit_next:
        in_specs.append(pl.BlockSpec((1, D), fixed))
        args.append(next_gain)
        out_shape.append(jax.ShapeDtypeStruct((S, D), BF16))
        out_specs.append(pl.BlockSpec((tm, D), row))
    res = pl.pallas_call(
        functools.partial(_ffn_kernel, emit_next=emit_next),
        grid=(S // tm, F // tf),
        in_specs=in_specs,
        out_specs=out_specs,
        out_shape=out_shape,
        scratch_shapes=[pltpu.VMEM((tm, D), BF16)],
        compiler_params=_params(("parallel", "arbitrary")),
        name="ffn_next" if emit_next else "ffn",
    )(*args)
    return res if emit_next else res[0]


def _proj_kernel(a_ref, w_ref, o_ref):
    o_ref[...] = _dot(a_ref[...], w_ref[...]).astype(o_ref.dtype)


def _proj(a, w, out_dtype, *, tm=1024, tn=512):
    S, K = a.shape
    N = w.shape[1]
    return pl.pallas_call(
        _proj_kernel,
        grid=(S // tm, N // tn),
        in_specs=[pl.BlockSpec((tm, K), lambda i, j: (i, 0)),
                  pl.BlockSpec((K, tn), lambda i, j: (0, j))],
        out_specs=pl.BlockSpec((tm, tn), lambda i, j: (i, j)),
        out_shape=jax.ShapeDtypeStruct((S, N), out_dtype),
        compiler_params=_params(("parallel", "arbitrary")),
        name="proj_" + jnp.dtype(out_dtype).name,
    )(a, w)


ROPE_GAP = A_HEAD_DIM // 2
HEAD_PERM = np.concatenate([np.arange(0, ROPE_HALF), np.arange(ROPE_DIM, ROPE_DIM + ROPE_GAP - ROPE_HALF),
                            np.arange(ROPE_HALF, ROPE_DIM), np.arange(ROPE_DIM + ROPE_GAP - ROPE_HALF, A_HEAD_DIM)])


def _proj_qk_kernel(a_ref, w_ref, g_ref, pos_ref, inv_ref, o_ref):
    a = a_ref[...]
    ang = pos_ref[...].astype(F32) * inv_ref[...]
    lane = lax.broadcasted_iota(jnp.int32, ang.shape, 1)
    cosf = jnp.cos(ang)
    sinf = jnp.where(lane < ROPE_GAP, -jnp.sin(ang), jnp.sin(ang))
    heads_per_dot = MXU_N // A_HEAD_DIM
    for g in range(o_ref.shape[0] // heads_per_dot):
        acc = _dot(a, w_ref[:, g * MXU_N:(g + 1) * MXU_N])
        for hh in range(heads_per_dot):
            h = g * heads_per_dot + hh
            y = _rms(acc[:, hh * A_HEAD_DIM:(hh + 1) * A_HEAD_DIM], g_ref[:, h * A_HEAD_DIM:(h + 1) * A_HEAD_DIM])
            o_ref[h] = (y * cosf + pltpu.roll(y, ROPE_GAP, 1) * sinf).astype(o_ref.dtype)


def _proj_qk(a, w, gains, pos, inv, *, tm=1024):
    S, K = a.shape
    N = w.shape[1]
    return pl.pallas_call(
        _proj_qk_kernel,
        grid=(S // tm,),
        in_specs=[pl.BlockSpec((tm, K), lambda i: (i, 0)),
                  pl.BlockSpec((K, N), lambda i: (0, 0)),
                  pl.BlockSpec((1, N), lambda i: (0, 0)),
                  pl.BlockSpec((tm, 1), lambda i: (i, 0)),
                  pl.BlockSpec((1, A_HEAD_DIM), lambda i: (0, 0))],
        out_specs=pl.BlockSpec((N // A_HEAD_DIM, tm, A_HEAD_DIM), lambda i: (0, i, 0)),
        out_shape=jax.ShapeDtypeStruct((N // A_HEAD_DIM, S, A_HEAD_DIM), BF16),
        compiler_params=_params(("parallel",)),
        name="proj_qk",
    )(a, w, gains, pos, inv)


def _proj_v_kernel(a_ref, w_ref, oa_ref, ob_ref):
    a = a_ref[...]
    heads_per_dot = MXU_N // A_HEAD_DIM
    for g in range(A_WIDTH // MXU_N):
        acc = _dot(a, w_ref[:, g * MXU_N:(g + 1) * MXU_N])
        for hh in range(heads_per_dot):
            oa_ref[g * heads_per_dot + hh] = acc[:, hh * A_HEAD_DIM:(hh + 1) * A_HEAD_DIM].astype(oa_ref.dtype)
    for g in range(B_VAL_WIDTH // MXU_N):
        cols = slice(g * MXU_N, (g + 1) * MXU_N)
        ob_ref[:, cols] = _dot(a, w_ref[:, A_WIDTH + g * MXU_N:A_WIDTH + (g + 1) * MXU_N]).astype(ob_ref.dtype)


def _proj_v(a, w, *, tm=1024):
    S, K = a.shape
    return pl.pallas_call(
        _proj_v_kernel,
        grid=(S // tm,),
        in_specs=[pl.BlockSpec((tm, K), lambda i: (i, 0)),
                  pl.BlockSpec(w.shape, lambda i: (0, 0))],
        out_specs=[pl.BlockSpec((A_HEADS, tm, A_HEAD_DIM), lambda i: (0, i, 0)),
                   pl.BlockSpec((tm, B_VAL_WIDTH), lambda i: (i, 0))],
        out_shape=[jax.ShapeDtypeStruct((A_HEADS, S, A_HEAD_DIM), BF16),
                   jax.ShapeDtypeStruct((S, B_VAL_WIDTH), BF16)],
        compiler_params=_params(("parallel",)),
        name="proj_v",
    )(a, w)


def _gate_kernel(a_ref, wz_ref, w2_ref, bias_ref, o_ref):
    z = _dot(a_ref[...], wz_ref[...])
    pre = _dot(z.astype(BF16), w2_ref[...]) + bias_ref[...]
    g = (jnp.minimum(pre, 0.0) - jnp.log1p(jnp.exp(-jnp.abs(pre)))) / GATE_NORMALIZER
    r = lax.broadcasted_iota(jnp.int32, g.shape, 0) % GLA_CHUNK
    shift = 1
    while shift < GLA_CHUNK:
        g = g + jnp.where(r >= shift, pltpu.roll(g, shift, 0), 0.0)
        shift *= 2
    o_ref[...] = g


def _gate(a, wz, w2, bias, *, tm=1024):
    S, K = a.shape
    return pl.pallas_call(
        _gate_kernel,
        grid=(S // tm,),
        in_specs=[pl.BlockSpec((tm, K), lambda i: (i, 0)),
                  pl.BlockSpec(wz.shape, lambda i: (0, 0)),
                  pl.BlockSpec(w2.shape, lambda i: (0, 0)),
                  pl.BlockSpec(bias.shape, lambda i: (0, 0))],
        out_specs=pl.BlockSpec((tm, B_KEY_WIDTH), lambda i: (i, 0)),
        out_shape=jax.ShapeDtypeStruct((S, B_KEY_WIDTH), F32),
        compiler_params=_params(("parallel",)),
        name="gate",
    )(a, wz, w2, bias)


A_BLK = 128
A_CHUNK = A_BLK * max(DILATIONS)


def _attn_kernel(q_ref, kp_ref, kc_ref, vp_ref, vc_ref, o_ref, qf, kf, vf, acc_s, m_s, l_s):
    L = A_CHUNK
    c = pl.program_id(1)
    qf[...] = q_ref[...].astype(F32)
    kf[0:L, :] = kp_ref[...].astype(F32)
    kf[L:2 * L, :] = kc_ref[...].astype(F32)
    vf[0:L, :] = vp_ref[...].astype(F32)
    vf[L:2 * L, :] = vc_ref[...].astype(F32)
    scale = A_HEAD_DIM ** -0.5
    row = lax.broadcasted_iota(jnp.int32, (A_BLK, A_BLK), 0)
    col = lax.broadcasted_iota(jnp.int32, (A_BLK, A_BLK), 1)
    mask_prev = col >= row
    mask_cur = row >= col
    mask_prev_first = jnp.logical_and(mask_prev, c > 0)
    for pi, d in enumerate(DILATIONS):
        for r in range(d):
            for n in range(L // (A_BLK * d)):
                start = n * A_BLK * d + r
                cur = pl.ds(start, A_BLK, stride=d)
                kcur = pl.ds(L + start, A_BLK, stride=d)
                kprev = pl.ds(L + start - A_BLK * d, A_BLK, stride=d)
                q = qf[cur, :].astype(BF16)
                sp = _dot_nt(q, kf[kprev, :].astype(BF16)) * scale
                sc = _dot_nt(q, kf[kcur, :].astype(BF16)) * scale
                sp = jnp.where(mask_prev_first if n == 0 else mask_prev, sp, -jnp.inf)
                sc = jnp.where(mask_cur, sc, -jnp.inf)
                m = jnp.maximum(jnp.max(sp, axis=-1, keepdims=True), jnp.max(sc, axis=-1, keepdims=True))
                pp = jnp.exp(sp - m)
                pc = jnp.exp(sc - m)
                l = jnp.sum(pp, axis=-1, keepdims=True) + jnp.sum(pc, axis=-1, keepdims=True)
                acc_s[pi, cur, :] = (_dot(pp.astype(BF16), vf[kprev, :].astype(BF16))
                                     + _dot(pc.astype(BF16), vf[kcur, :].astype(BF16)))
                m_s[pi, cur, :] = jnp.broadcast_to(m, (A_BLK, A_HEAD_DIM))
                l_s[pi, cur, :] = jnp.broadcast_to(l, (A_BLK, A_HEAD_DIM))
    ms = [m_s[pi] for pi in range(len(DILATIONS))]
    m = functools.reduce(jnp.maximum, ms)
    ws = [jnp.exp(mi - m) for mi in ms]
    num = sum(w * acc_s[pi] for pi, w in enumerate(ws))
    den = sum(w * l_s[pi] for pi, w in enumerate(ws))
    o_ref[...] = (num / den).astype(o_ref.dtype)


def _attn(qk, v):
    S = v.shape[1]
    L = A_CHUNK
    npat = len(DILATIONS)
    blk = (None, L, A_HEAD_DIM)
    prev = lambda c: jnp.maximum(c - 1, 0)
    return pl.pallas_call(
        _attn_kernel,
        grid=(A_HEADS, S // L),
        in_specs=[pl.BlockSpec(blk, lambda h, c: (h, c, 0)),
                  pl.BlockSpec(blk, lambda h, c: (A_HEADS + h, prev(c), 0)),
                  pl.BlockSpec(blk, lambda h, c: (A_HEADS + h, c, 0)),
                  pl.BlockSpec(blk, lambda h, c: (h, prev(c), 0)),
                  pl.BlockSpec(blk, lambda h, c: (h, c, 0))],
        out_specs=pl.BlockSpec((L, A_HEAD_DIM), lambda h, c: (c, h)),
        out_shape=jax.ShapeDtypeStruct((S, A_WIDTH), BF16),
        scratch_shapes=[pltpu.VMEM((L, A_HEAD_DIM), F32),
                        pltpu.VMEM((2 * L, A_HEAD_DIM), F32),
                        pltpu.VMEM((2 * L, A_HEAD_DIM), F32),
                        pltpu.VMEM((npat, L, A_HEAD_DIM), F32),
                        pltpu.VMEM((npat, L, A_HEAD_DIM), F32),
                        pltpu.VMEM((npat, L, A_HEAD_DIM), F32)],
        compiler_params=_params(("parallel", "arbitrary")),
        name="attn",
    )(qk, qk, qk, v, v)


def _gla_kernel(q_ref, k_ref, v_ref, b_ref, r_ref, gn_ref, o_ref, st_ref, *, nchunk):
    C, Cs = GLA_CHUNK, GLA_SUB
    NS = C // Cs

    @pl.when(pl.program_id(0) == 0)
    def _():
        st_ref[...] = jnp.zeros_like(st_ref)

    row = lax.broadcasted_iota(jnp.int32, (C, C), 0)
    col = lax.broadcasted_iota(jnp.int32, (C, C), 1)
    sub_start = (row // Cs) * Cs
    sub_row = lax.broadcasted_iota(jnp.int32, (NS, Cs, B_KEY_DIM), 1)
    for c in range(nchunk):
        rows = slice(c * C, (c + 1) * C)
        for h in range(B_HEADS):
            kcols = slice(h * B_KEY_DIM, (h + 1) * B_KEY_DIM)
            vcols = slice(h * B_VAL_DIM, (h + 1) * B_VAL_DIM)
            q = q_ref[rows, kcols] * (B_KEY_DIM ** -0.5)
            k = k_ref[rows, kcols]
            b = b_ref[rows, kcols]
            v = v_ref[rows, vcols]
            b_last = b[C - 1:C, :]
            st = st_ref[h]
            o = _dot_nt((q * jnp.exp(b)).astype(BF16), st.astype(BF16))
            k_end = (k * jnp.exp(b_last - b)).astype(BF16)
            st_ref[h] = st * jnp.exp(b_last) + _dot_tn(v, k_end)

            blocks = [jnp.zeros((Cs, C), F32)]
            for s in range(1, NS):
                srows = slice(s * Cs, (s + 1) * Cs)
                ref_s = b[s * Cs - 1:s * Cs, :]
                q_ref_s = (q[srows, :] * jnp.exp(b[srows, :] - ref_s)).astype(BF16)
                k_ref_s = (k * jnp.exp(jnp.minimum(ref_s - b, 0.0))).astype(BF16)
                blocks.append(_dot_nt(q_ref_s, k_ref_s))
            a = jnp.where(col < sub_start, jnp.concatenate(blocks, axis=0), 0.0)

            q3 = q.reshape(NS, Cs, B_KEY_DIM)
            k3 = k.reshape(NS, Cs, B_KEY_DIM)
            b3 = b.reshape(NS, Cs, B_KEY_DIM)
            for j in range(Cs):
                kb = jnp.broadcast_to(k3[:, j:j + 1, :], q3.shape)
                bb = jnp.broadcast_to(b3[:, j:j + 1, :], q3.shape)
                w = jnp.exp(jnp.where(sub_row >= j, b3 - bb, -jnp.inf))
                dj = jnp.sum((q3 * kb * w).reshape(C, B_KEY_DIM), axis=-1, keepdims=True)
                a = jnp.where(col == sub_start + j, dj, a)
            o = o + _dot(a.astype(BF16), v)
            r = r_ref[rows, vcols]
            o_ref[rows, vcols] = (_rms(o, gn_ref[...]) * (r * jax.nn.sigmoid(r))).astype(o_ref.dtype)


def _gla(pf, bv, b, gain, *, rb=256):
    S = pf.shape[0]
    return pl.pallas_call(
        functools.partial(_gla_kernel, nchunk=rb // GLA_CHUNK),
        grid=(S // rb,),
        in_specs=[pl.BlockSpec((rb, B_KEY_WIDTH), lambda i: (i, 0)),
                  pl.BlockSpec((rb, B_KEY_WIDTH), lambda i: (i, 1)),
                  pl.BlockSpec((rb, B_VAL_WIDTH), lambda i: (i, 0)),
                  pl.BlockSpec((rb, B_KEY_WIDTH), lambda i: (i, 0)),
                  pl.BlockSpec((rb, B_VAL_WIDTH), lambda i: (i, 1)),
                  pl.BlockSpec((1, B_VAL_DIM), lambda i: (0, 0))],
        out_specs=pl.BlockSpec((rb, B_VAL_WIDTH), lambda i: (i, 0)),
        out_shape=jax.ShapeDtypeStruct((S, B_VAL_WIDTH), BF16),
        scratch_shapes=[pltpu.VMEM((B_HEADS, B_VAL_DIM, B_KEY_DIM), F32)],
        compiler_params=_params(("arbitrary",)),
        name="gla",
    )(pf, pf, bv, b, pf, gain)


def _merge_out_kernel(x_ref, oa_ref, ob_ref, ga_ref, gb_ref, wa_ref, wb_ref, wo_ref, o_ref):
    @pl.when(pl.program_id(1) == 0)
    def _():
        o_ref[...] = x_ref[...]

    ya = _dot(oa_ref[...], wa_ref[...])
    yb = _dot(ob_ref[...], wb_ref[...])
    y = jax.nn.sigmoid(ga_ref[...]) * ya + jax.nn.sigmoid(gb_ref[...]) * yb
    o_ref[...] += _dot(y.astype(BF16), wo_ref[...])


def _merge_out(x, oa, ob, pf, wa, wb, wo, *, tm=512, tn=512):
    S, D = x.shape
    ga_blk = 2048 // tn
    gb_blk = 4096 // tn
    row = lambda i, j: (i, 0)
    return pl.pallas_call(
        _merge_out_kernel,
        grid=(S // tm, D // tn),
        in_specs=[pl.BlockSpec((tm, D), row),
                  pl.BlockSpec((tm, A_WIDTH), row),
                  pl.BlockSpec((tm, B_VAL_WIDTH), row),
                  pl.BlockSpec((tm, tn), lambda i, j: (i, ga_blk + j)),
                  pl.BlockSpec((tm, tn), lambda i, j: (i, gb_blk + j)),
                  pl.BlockSpec((A_WIDTH, tn), lambda i, j: (0, j)),
                  pl.BlockSpec((B_VAL_WIDTH, tn), lambda i, j: (0, j)),
                  pl.BlockSpec((tn, D), lambda i, j: (j, 0))],
        out_specs=pl.BlockSpec((tm, D), row),
        out_shape=jax.ShapeDtypeStruct((S, D), F32),
        compiler_params=_params(("parallel", "arbitrary")),
        name="merge_out",
    )(x, oa, ob, pf, pf, wa, wb, wo)


def _layer(x, pos, p):
    x1, h = _ffn(x, p["ffn1_norm"], p["ffn1_wg"], p["ffn1_wu"], p["ffn1_wd"], p["mix_norm"])
    qk = _proj_qk(h, p["w_qk"], p["qk_gain"], pos, p["rope_inv"])
    av, bv = _proj_v(h, p["w_v"])
    pf = _proj(h, p["w_f"], F32)
    b = _gate(h, p["w_z"], p["w_2"], p["gate_bias"])
    o_a = _attn(qk, av)
    o_b = _gla(pf, bv, b, p["b_out_norm"])
    x2 = _merge_out(x1, o_a, o_b, pf, p["w_a_up"], p["w_b_up"], p["w_out"])
    return _ffn(x2, p["ffn2_norm"], p["ffn2_wg"], p["ffn2_wu"], p["ffn2_wd"])


def kernel(x, positions, ffn1_norm, ffn1_w_gate, ffn1_w_up, ffn1_w_down, mix_norm, w_in, a_q_norm, a_k_norm, b_gate_w2, b_gate_bias, b_out_norm, w_a_up, w_b_up, w_out, ffn2_norm, ffn2_w_gate, ffn2_w_up, ffn2_w_down):
    B, S, D = x.shape
    assert D == D_MODEL and S % A_CHUNK == 0
    depth = w_in.shape[0]
    c = np.cumsum([0, A_WIDTH, A_WIDTH, A_WIDTH, B_KEY_WIDTH, B_KEY_WIDTH, B_VAL_WIDTH, B_VAL_WIDTH,
                   GATE_RANK, D_MODEL, D_MODEL]).tolist()
    inv = jnp.power(ROPE_THETA, -(jnp.arange(ROPE_HALF, dtype=F32) * 2.0 / ROPE_DIM))
    rope_inv = jnp.concatenate([inv, inv, jnp.zeros((A_HEAD_DIM - ROPE_DIM,), F32)])[HEAD_PERM][None, :]
    qk_cols = (np.arange(2 * A_HEADS)[:, None] * A_HEAD_DIM + HEAD_PERM[None, :]).reshape(-1)
    outs = []
    for bi in range(B):
        xb = x[bi]
        pos = positions[bi][:, None]
        for l in range(depth):
            w = w_in[l]
            p = {
                "ffn1_norm": ffn1_norm[l][None, :], "mix_norm": mix_norm[l][None, :],
                "ffn2_norm": ffn2_norm[l][None, :],
                "ffn1_wg": ffn1_w_gate[l].astype(BF16), "ffn1_wu": ffn1_w_up[l].astype(BF16),
                "ffn1_wd": ffn1_w_down[l].astype(BF16),
                "ffn2_wg": ffn2_w_gate[l].astype(BF16), "ffn2_wu": ffn2_w_up[l].astype(BF16),
                "ffn2_wd": ffn2_w_down[l].astype(BF16),
                "w_qk": w[:, c[0]:c[2]][:, qk_cols].astype(BF16),
                "w_v": jnp.concatenate([w[:, c[2]:c[3]], w[:, c[5]:c[6]]], axis=1).astype(BF16),
                "w_f": jnp.concatenate([w[:, c[3]:c[5]], w[:, c[6]:c[7]], w[:, c[8]:c[10]]], axis=1).astype(BF16),
                "w_z": jnp.pad(w[:, c[7]:c[8]], ((0, 0), (0, LANES - GATE_RANK))).astype(BF16),
                "w_2": jnp.pad(b_gate_w2[l], ((0, LANES - GATE_RANK), (0, 0))).astype(BF16),
                "gate_bias": b_gate_bias[l][None, :],
                "qk_gain": jnp.concatenate([jnp.tile(a_q_norm[l][HEAD_PERM], A_HEADS),
                                            jnp.tile(a_k_norm[l][HEAD_PERM], A_HEADS)])[None, :],
                "rope_inv": rope_inv,
                "b_out_norm": b_out_norm[l][None, :],
                "w_a_up": w_a_up[l].astype(BF16), "w_b_up": w_b_up[l].astype(BF16), "w_out": w_out[l].astype(BF16),
            }
            xb = _layer(xb, pos, p)
        outs.append(xb)
    return jnp.stack(outs, axis=0)
```

```python
import functools

import jax
import jax.numpy as jnp
import numpy as np
from jax import lax
from jax.experimental import pallas as pl
from jax.experimental.pallas import tpu as pltpu

F32 = jnp.float32
BF16 = jnp.bfloat16

D_MODEL = 2048
D_FF = 5632
RMS_EPS = 1e-6
ROPE_THETA = 500000.0
A_HEAD_DIM = 128
A_HEADS = 8
A_WIDTH = A_HEADS * A_HEAD_DIM
ROPE_DIM = A_HEAD_DIM // 4
ROPE_HALF = ROPE_DIM // 2
DILATIONS = (1, 4, 16)
A_SPAN = 128
B_HEADS = 4
B_VAL_DIM = 256
B_KEY_DIM = 128
B_KEY_WIDTH = B_HEADS * B_KEY_DIM
B_VAL_WIDTH = B_HEADS * B_VAL_DIM
GATE_RANK = 16
GATE_NORMALIZER = 16.0
GLA_CHUNK = 64
GLA_SUB = 16
LANES = 128
MXU_N = 256

VMEM_LIMIT = 56 * 1024 * 1024


def _params(sem):
    return pltpu.CompilerParams(dimension_semantics=sem, vmem_limit_bytes=VMEM_LIMIT)


def _rms(x, g):
    return x * lax.rsqrt(jnp.mean(x * x, axis=-1, keepdims=True) + RMS_EPS) * g


def _dot(a, b):
    return jnp.dot(a, b, preferred_element_type=F32)


def _dot_nt(a, b):
    return lax.dot_general(a, b, (((1,), (1,)), ((), ())), preferred_element_type=F32)


def _dot_tn(a, b):
    return lax.dot_general(a, b, (((0,), (0,)), ((), ())), preferred_element_type=F32)


def _ffn_kernel(x_ref, g_ref, wg_ref, wu_ref, wd_ref, *rest, emit_next):
    if emit_next:
        gn_ref, o_ref, hn_ref, h_scr = rest
    else:
        o_ref, h_scr = rest
    f = pl.program_id(1)

    @pl.when(f == 0)
    def _():
        x = x_ref[...]
        h_scr[...] = _rms(x, g_ref[...]).astype(BF16)
        o_ref[...] = x

    h = h_scr[...]
    gate = _dot(h, wg_ref[...])
    up = _dot(h, wu_ref[...])
    act = (0.5 * (gate * jax.nn.sigmoid(gate)) * up).astype(BF16)
    o_ref[...] += _dot(act, wd_ref[...])

    if emit_next:
        @pl.when(f == pl.num_programs(1) - 1)
        def _():
            hn_ref[...] = _rms(o_ref[...], gn_ref[...]).astype(BF16)


def _ffn(x, gain, wg, wu, wd, next_gain=None, *, tm=512, tf=512):
    S, D = x.shape
    F = wg.shape[1]
    emit_next = next_gain is not None
    row = lambda i, f: (i, 0)
    fixed = lambda i, f: (0, 0)
    in_specs = [
        pl.BlockSpec((tm, D), row),
        pl.BlockSpec((1, D), fixed),
        pl.BlockSpec((D, tf), lambda i, f: (0, f)),
        pl.BlockSpec((D, tf), lambda i, f: (0, f)),
        pl.BlockSpec((tf, D), lambda i, f: (f, 0)),
    ]
    args = [x, gain, wg, wu, wd]
    out_shape = [jax.ShapeDtypeStruct((S, D), F32)]
    out_specs = [pl.BlockSpec((tm, D), row)]
    if emit_next:
        in_specs.append(pl.BlockSpec((1, D), fixed))
        args.append(next_gain)
        out_shape.append(jax.ShapeDtypeStruct((S, D), BF16))
        out_specs.append(pl.BlockSpec((tm, D), row))
    res = pl.pallas_call(
        functools.partial(_ffn_kernel, emit_next=emit_next),
        grid=(S // tm, F // tf),
        in_specs=in_specs,
        out_specs=out_specs,
        out_shape=out_shape,
        scratch_shapes=[pltpu.VMEM((tm, D), BF16)],
        compiler_params=_params(("parallel", "arbitrary")),
        name="ffn_next" if emit_next else "ffn",
    )(*args)
    return res if emit_next else res[0]


def _proj_kernel(a_ref, w_ref, o_ref):
    o_ref[...] = _dot(a_ref[...], w_ref[...]).astype(o_ref.dtype)


def _proj(a, w, out_dtype, *, tm=1024, tn=512):
    S, K = a.shape
    N = w.shape[1]
    return pl.pallas_call(
        _proj_kernel,
        grid=(S // tm, N // tn),
        in_specs=[pl.BlockSpec((tm, K), lambda i, j: (i, 0)),
                  pl.BlockSpec((K, tn), lambda i, j: (0, j))],
        out_specs=pl.BlockSpec((tm, tn), lambda i, j: (i, j)),
        out_shape=jax.ShapeDtypeStruct((S, N), out_dtype),
        compiler_params=_params(("parallel", "arbitrary")),
        name="proj_" + jnp.dtype(out_dtype).name,
    )(a, w)


ROPE_GAP = A_HEAD_DIM // 2
HEAD_PERM = np.concatenate([np.arange(0, ROPE_HALF), np.arange(ROPE_DIM, ROPE_DIM + ROPE_GAP - ROPE_HALF),
                            np.arange(ROPE_HALF, ROPE_DIM), np.arange(ROPE_DIM + ROPE_GAP - ROPE_HALF, A_HEAD_DIM)])


def _proj_qk_kernel(a_ref, w_ref, g_ref, pos_ref, inv_ref, o_ref):
    a = a_ref[...]
    ang = pos_ref[...].astype(F32) * inv_ref[...]
    lane = lax.broadcasted_iota(jnp.int32, ang.shape, 1)
    cosf = jnp.cos(ang)
    sinf = jnp.where(lane < ROPE_GAP, -jnp.sin(ang), jnp.sin(ang))
    heads_per_dot = MXU_N // A_HEAD_DIM
    for g in range(o_ref.shape[0] // heads_per_dot):
        acc = _dot(a, w_ref[:, g * MXU_N:(g + 1) * MXU_N])
        for hh in range(heads_per_dot):
            h = g * heads_per_dot + hh
            y = _rms(acc[:, hh * A_HEAD_DIM:(hh + 1) * A_HEAD_DIM], g_ref[:, h * A_HEAD_DIM:(h + 1) * A_HEAD_DIM])
            o_ref[h] = (y * cosf + pltpu.roll(y, ROPE_GAP, 1) * sinf).astype(o_ref.dtype)


def _proj_qk(a, w, gains, pos, inv, *, tm=1024):
    S, K = a.shape
    N = w.shape[1]
    return pl.pallas_call(
        _proj_qk_kernel,
        grid=(S // tm,),
        in_specs=[pl.BlockSpec((tm, K), lambda i: (i, 0)),
                  pl.BlockSpec((K, N), lambda i: (0, 0)),
                  pl.BlockSpec((1, N), lambda i: (0, 0)),
                  pl.BlockSpec((tm, 1), lambda i: (i, 0)),
                  pl.BlockSpec((1, A_HEAD_DIM), lambda i: (0, 0))],
        out_specs=pl.BlockSpec((N // A_HEAD_DIM, tm, A_HEAD_DIM), lambda i: (0, i, 0)),
        out_shape=jax.ShapeDtypeStruct((N // A_HEAD_DIM, S, A_HEAD_DIM), BF16),
        compiler_params=_params(("parallel",)),
        name="proj_qk",
    )(a, w, gains, pos, inv)


def _proj_v_kernel(a_ref, w_ref, oa_ref, ob_ref):
    a = a_ref[...]
    heads_per_dot = MXU_N // A_HEAD_DIM
    for g in range(A_WIDTH // MXU_N):
        acc = _dot(a, w_ref[:, g * MXU_N:(g + 1) * MXU_N])
        for hh in range(heads_per_dot):
            oa_ref[g * heads_per_dot + hh] = acc[:, hh * A_HEAD_DIM:(hh + 1) * A_HEAD_DIM].astype(oa_ref.dtype)
    for g in range(B_VAL_WIDTH // MXU_N):
        cols = slice(g * MXU_N, (g + 1) * MXU_N)
        ob_ref[:, cols] = _dot(a, w_ref[:, A_WIDTH + g * MXU_N:A_WIDTH + (g + 1) * MXU_N]).astype(ob_ref.dtype)


def _proj_v(a, w, *, tm=1024):
    S, K = a.shape
    return pl.pallas_call(
        _proj_v_kernel,
        grid=(S // tm,),
        in_specs=[pl.BlockSpec((tm, K), lambda i: (i, 0)),
                  pl.BlockSpec(w.shape, lambda i: (0, 0))],
        out_specs=[pl.BlockSpec((A_HEADS, tm, A_HEAD_DIM), lambda i: (0, i, 0)),
                   pl.BlockSpec((tm, B_VAL_WIDTH), lambda i: (i, 0))],
        out_shape=[jax.ShapeDtypeStruct((A_HEADS, S, A_HEAD_DIM), BF16),
                   jax.ShapeDtypeStruct((S, B_VAL_WIDTH), BF16)],
        compiler_params=_params(("parallel",)),
        name="proj_v",
    )(a, w)


def _gate_kernel(a_ref, wz_ref, w2_ref, bias_ref, o_ref):
    z = _dot(a_ref[...], wz_ref[...])
    pre = _dot(z.astype(BF16), w2_ref[...]) + bias_ref[...]
    g = (jnp.minimum(pre, 0.0) - jnp.log1p(jnp.exp(-jnp.abs(pre)))) / GATE_NORMALIZER
    r = lax.broadcasted_iota(jnp.int32, g.shape, 0) % GLA_CHUNK
    shift = 1
    while shift < GLA_CHUNK:
        g = g + jnp.where(r >= shift, pltpu.roll(g, shift, 0), 0.0)
        shift *= 2
    o_ref[...] = g


def _gate(a, wz, w2, bias, *, tm=1024):
    S, K = a.shape
    return pl.pallas_call(
        _gate_kernel,
        grid=(S // tm,),
        in_specs=[pl.BlockSpec((tm, K), lambda i: (i, 0)),
                  pl.BlockSpec(wz.shape, lambda i: (0, 0)),
                  pl.BlockSpec(w2.shape, lambda i: (0, 0)),
                  pl.BlockSpec(bias.shape, lambda i: (0, 0))],
        out_specs=pl.BlockSpec((tm, B_KEY_WIDTH), lambda i: (i, 0)),
        out_shape=jax.ShapeDtypeStruct((S, B_KEY_WIDTH), F32),
        compiler_params=_params(("parallel",)),
        name="gate",
    )(a, wz, w2, bias)


A_BLK = 128
A_CHUNK = A_BLK * max(DILATIONS)


def _attn_kernel(q_ref, kp_ref, kc_ref, vp_ref, vc_ref, o_ref, qf, kf, vf, acc_s, m_s, l_s, *, group):
    L = A_CHUNK
    c = pl.program_id(1)
    qf[...] = q_ref[...].astype(F32)
    kf[0:L, :] = kp_ref[...].astype(F32)
    kf[L:2 * L, :] = kc_ref[...].astype(F32)
    vf[0:L, :] = vp_ref[...].astype(F32)
    vf[L:2 * L, :] = vc_ref[...].astype(F32)
    scale = A_HEAD_DIM ** -0.5
    row = lax.broadcasted_iota(jnp.int32, (1, A_BLK, 2 * A_BLK), 1)
    col = lax.broadcasted_iota(jnp.int32, (1, A_BLK, 2 * A_BLK), 2)
    band = jnp.logical_and(col >= row, col <= row + A_SPAN)
    first_chunk = (c == 0).astype(jnp.int32)
    for pi, d in enumerate(DILATIONS):
        tiles = [(r, n) for r in range(d) for n in range(L // (A_BLK * d))]
        for g0 in range(0, len(tiles), group):
            grp = tiles[g0:g0 + group]
            starts = [n * A_BLK * d + r for r, n in grp]
            keys = [pl.ds(L + st - A_BLK * d, 2 * A_BLK, stride=d) for st in starts]
            s = jnp.concatenate(
                [_dot_nt(qf[pl.ds(st, A_BLK, stride=d), :].astype(BF16), kf[ks, :].astype(BF16))
                 for st, ks in zip(starts, keys)], axis=0).reshape(len(grp), A_BLK, 2 * A_BLK)
            gidx = lax.broadcasted_iota(jnp.int32, (len(grp), 1, 1), 0)
            first = functools.reduce(jnp.logical_or, [gidx == gi for gi, (r, n) in enumerate(grp) if n == 0],
                                     gidx < 0)
            first_col = jnp.where(first, A_BLK, 0) * first_chunk
            s = jnp.where(jnp.logical_and(band, col >= first_col), s * scale, -jnp.inf)
            m = jnp.max(s, axis=-1, keepdims=True)
            p = jnp.exp(s - m)
            l = jnp.sum(p, axis=-1, keepdims=True)
            pb = p.astype(BF16)
            mb = jnp.broadcast_to(m, (len(grp), A_BLK, A_HEAD_DIM))
            lb = jnp.broadcast_to(l, (len(grp), A_BLK, A_HEAD_DIM))
            for gi, (st, ks) in enumerate(zip(starts, keys)):
                cur = pl.ds(st, A_BLK, stride=d)
                acc_s[pi, cur, :] = _dot(pb[gi], vf[ks, :].astype(BF16))
                m_s[pi, cur, :] = mb[gi]
                l_s[pi, cur, :] = lb[gi]
    ms = [m_s[pi] for pi in range(len(DILATIONS))]
    m = functools.reduce(jnp.maximum, ms)
    ws = [jnp.exp(mi - m) for mi in ms]
    num = sum(w * acc_s[pi] for pi, w in enumerate(ws))
    den = sum(w * l_s[pi] for pi, w in enumerate(ws))
    o_ref[...] = (num / den).astype(o_ref.dtype)


def _attn(qk, v, *, group=8):
    S = v.shape[1]
    L = A_CHUNK
    npat = len(DILATIONS)
    blk = (None, L, A_HEAD_DIM)
    prev = lambda c: jnp.maximum(c - 1, 0)
    return pl.pallas_call(
        functools.partial(_attn_kernel, group=group),
        grid=(A_HEADS, S // L),
        in_specs=[pl.BlockSpec(blk, lambda h, c: (h, c, 0)),
                  pl.BlockSpec(blk, lambda h, c: (A_HEADS + h, prev(c), 0)),
                  pl.BlockSpec(blk, lambda h, c: (A_HEADS + h, c, 0)),
                  pl.BlockSpec(blk, lambda h, c: (h, prev(c), 0)),
                  pl.BlockSpec(blk, lambda h, c: (h, c, 0))],
        out_specs=pl.BlockSpec((L, A_HEAD_DIM), lambda h, c: (c, h)),
        out_shape=jax.ShapeDtypeStruct((S, A_WIDTH), BF16),
        scratch_shapes=[pltpu.VMEM((L, A_HEAD_DIM), F32),
                        pltpu.VMEM((2 * L, A_HEAD_DIM), F32),
                        pltpu.VMEM((2 * L, A_HEAD_DIM), F32),
                        pltpu.VMEM((npat, L, A_HEAD_DIM), F32),
                        pltpu.VMEM((npat, L, A_HEAD_DIM), F32),
                        pltpu.VMEM((npat, L, A_HEAD_DIM), F32)],
        compiler_params=_params(("parallel", "arbitrary")),
        name="attn",
    )(qk, qk, qk, v, v)


def _gla_kernel(q_ref, k_ref, v_ref, b_ref, r_ref, gn_ref, o_ref, st_ref, *, nchunk):
    C, Cs = GLA_CHUNK, GLA_SUB
    NS = C // Cs

    @pl.when(pl.program_id(0) == 0)
    def _():
        st_ref[...] = jnp.zeros_like(st_ref)

    row = lax.broadcasted_iota(jnp.int32, (C, C), 0)
    col = lax.broadcasted_iota(jnp.int32, (C, C), 1)
    sub_start = (row // Cs) * Cs
    sub_row = lax.broadcasted_iota(jnp.int32, (NS, Cs, B_KEY_DIM), 1)
    for c in range(nchunk):
        rows = slice(c * C, (c + 1) * C)
        for h in range(B_HEADS):
            kcols = slice(h * B_KEY_DIM, (h + 1) * B_KEY_DIM)
            vcols = slice(h * B_VAL_DIM, (h + 1) * B_VAL_DIM)
            q = q_ref[rows, kcols] * (B_KEY_DIM ** -0.5)
            k = k_ref[rows, kcols]
            b = b_ref[rows, kcols]
            v = v_ref[rows, vcols]
            b_last = b[C - 1:C, :]
            st = st_ref[h]
            o = _dot_nt((q * jnp.exp(b)).astype(BF16), st.astype(BF16))
            k_end = (k * jnp.exp(b_last - b)).astype(BF16)
            st_ref[h] = st * jnp.exp(b_last) + _dot_tn(v, k_end)

            blocks = [jnp.zeros((Cs, C), F32)]
            for s in range(1, NS):
                srows = slice(s * Cs, (s + 1) * Cs)
                ref_s = b[s * Cs - 1:s * Cs, :]
                q_ref_s = (q[srows, :] * jnp.exp(b[srows, :] - ref_s)).astype(BF16)
                k_ref_s = (k * jnp.exp(jnp.minimum(ref_s - b, 0.0))).astype(BF16)
                blocks.append(_dot_nt(q_ref_s, k_ref_s))
            a = jnp.where(col < sub_start, jnp.concatenate(blocks, axis=0), 0.0)

            q3 = q.reshape(NS, Cs, B_KEY_DIM)
            k3 = k.reshape(NS, Cs, B_KEY_DIM)
            b3 = b.reshape(NS, Cs, B_KEY_DIM)
            for j in range(Cs):
                kb = jnp.broadcast_to(k3[:, j:j + 1, :], q3.shape)
                bb = jnp.broadcast_to(b3[:, j:j + 1, :], q3.shape)
                w = jnp.exp(jnp.where(sub_row >= j, b3 - bb, -jnp.inf))
                dj = jnp.sum((q3 * kb * w).reshape(C, B_KEY_DIM), axis=-1, keepdims=True)
                a = jnp.where(col == sub_start + j, dj, a)
            o = o + _dot(a.astype(BF16), v)
            r = r_ref[rows, vcols]
            o_ref[rows, vcols] = (_rms(o, gn_ref[...]) * (r * jax.nn.sigmoid(r))).astype(o_ref.dtype)


def _gla(pf, bv, b, gain, *, rb=256):
    S = pf.shape[0]
    return pl.pallas_call(
        functools.partial(_gla_kernel, nchunk=rb // GLA_CHUNK),
        grid=(S // rb,),
        in_specs=[pl.BlockSpec((rb, B_KEY_WIDTH), lambda i: (i, 0)),
                  pl.BlockSpec((rb, B_KEY_WIDTH), lambda i: (i, 1)),
                  pl.BlockSpec((rb, B_VAL_WIDTH), lambda i: (i, 0)),
                  pl.BlockSpec((rb, B_KEY_WIDTH), lambda i: (i, 0)),
                  pl.BlockSpec((rb, B_VAL_WIDTH), lambda i: (i, 1)),
                  pl.BlockSpec((1, B_VAL_DIM), lambda i: (0, 0))],
        out_specs=pl.BlockSpec((rb, B_VAL_WIDTH), lambda i: (i, 0)),
        out_shape=jax.ShapeDtypeStruct((S, B_VAL_WIDTH), BF16),
        scratch_shapes=[pltpu.VMEM((B_HEADS, B_VAL_DIM, B_KEY_DIM), F32)],
        compiler_params=_params(("arbitrary",)),
        name="gla",
    )(pf, pf, bv, b, pf, gain)


def _merge_out_kernel(x_ref, oa_ref, ob_ref, ga_ref, gb_ref, wa_ref, wb_ref, wo_ref, o_ref):
    @pl.when(pl.program_id(1) == 0)
    def _():
        o_ref[...] = x_ref[...]

    ya = _dot(oa_ref[...], wa_ref[...])
    yb = _dot(ob_ref[...], wb_ref[...])
    y = jax.nn.sigmoid(ga_ref[...]) * ya + jax.nn.sigmoid(gb_ref[...]) * yb
    o_ref[...] += _dot(y.astype(BF16), wo_ref[...])


def _merge_out(x, oa, ob, pf, wa, wb, wo, *, tm=512, tn=512):
    S, D = x.shape
    ga_blk = 2048 // tn
    gb_blk = 4096 // tn
    row = lambda i, j: (i, 0)
    return pl.pallas_call(
        _merge_out_kernel,
        grid=(S // tm, D // tn),
        in_specs=[pl.BlockSpec((tm, D), row),
                  pl.BlockSpec((tm, A_WIDTH), row),
                  pl.BlockSpec((tm, B_VAL_WIDTH), row),
                  pl.BlockSpec((tm, tn), lambda i, j: (i, ga_blk + j)),
                  pl.BlockSpec((tm, tn), lambda i, j: (i, gb_blk + j)),
                  pl.BlockSpec((A_WIDTH, tn), lambda i, j: (0, j)),
                  pl.BlockSpec((B_VAL_WIDTH, tn), lambda i, j: (0, j)),
                  pl.BlockSpec((tn, D), lambda i, j: (j, 0))],
        out_specs=pl.BlockSpec((tm, D), row),
        out_shape=jax.ShapeDtypeStruct((S, D), F32),
        compiler_params=_params(("parallel", "arbitrary")),
        name="merge_out",
    )(x, oa, ob, pf, pf, wa, wb, wo)


def _layer(x, pos, p):
    x1, h = _ffn(x, p["ffn1_norm"], p["ffn1_wg"], p["ffn1_wu"], p["ffn1_wd"], p["mix_norm"])
    qk = _proj_qk(h, p["w_qk"], p["qk_gain"], pos, p["rope_inv"])
    av, bv = _proj_v(h, p["w_v"])
    pf = _proj(h, p["w_f"], F32)
    b = _gate(h, p["w_z"], p["w_2"], p["gate_bias"])
    o_a = _attn(qk, av)
    o_b = _gla(pf, bv, b, p["b_out_norm"])
    x2 = _merge_out(x1, o_a, o_b, pf, p["w_a_up"], p["w_b_up"], p["w_out"])
    return _ffn(x2, p["ffn2_norm"], p["ffn2_wg"], p["ffn2_wu"], p["ffn2_wd"])


def kernel(x, positions, ffn1_norm, ffn1_w_gate, ffn1_w_up, ffn1_w_down, mix_norm, w_in, a_q_norm, a_k_norm, b_gate_w2, b_gate_bias, b_out_norm, w_a_up, w_b_up, w_out, ffn2_norm, ffn2_w_gate, ffn2_w_up, ffn2_w_down):
    B, S, D = x.shape
    assert D == D_MODEL and S % A_CHUNK == 0
    depth = w_in.shape[0]
    c = np.cumsum([0, A_WIDTH, A_WIDTH, A_WIDTH, B_KEY_WIDTH, B_KEY_WIDTH, B_VAL_WIDTH, B_VAL_WIDTH,
                   GATE_RANK, D_MODEL, D_MODEL]).tolist()
    inv = jnp.power(ROPE_THETA, -(jnp.arange(ROPE_HALF, dtype=F32) * 2.0 / ROPE_DIM))
    rope_inv = jnp.concatenate([inv, inv, jnp.zeros((A_HEAD_DIM - ROPE_DIM,), F32)])[HEAD_PERM][None, :]
    qk_cols = (np.arange(2 * A_HEADS)[:, None] * A_HEAD_DIM + HEAD_PERM[None, :]).reshape(-1)
    outs = []
    for bi in range(B):
        xb = x[bi]
        pos = positions[bi][:, None]
        for l in range(depth):
            w = w_in[l]
            p = {
                "ffn1_norm": ffn1_norm[l][None, :], "mix_norm": mix_norm[l][None, :],
                "ffn2_norm": ffn2_norm[l][None, :],
                "ffn1_wg": ffn1_w_gate[l].astype(BF16), "ffn1_wu": ffn1_w_up[l].astype(BF16),
                "ffn1_wd": ffn1_w_down[l].astype(BF16),
                "ffn2_wg": ffn2_w_gate[l].astype(BF16), "ffn2_wu": ffn2_w_up[l].astype(BF16),
                "ffn2_wd": ffn2_w_down[l].astype(BF16),
                "w_qk": w[:, c[0]:c[2]][:, qk_cols].astype(BF16),
                "w_v": jnp.concatenate([w[:, c[2]:c[3]], w[:, c[5]:c[6]]], axis=1).astype(BF16),
                "w_f": jnp.concatenate([w[:, c[3]:c[5]], w[:, c[6]:c[7]], w[:, c[8]:c[10]]], axis=1).astype(BF16),
                "w_z": jnp.pad(w[:, c[7]:c[8]], ((0, 0), (0, LANES - GATE_RANK))).astype(BF16),
                "w_2": jnp.pad(b_gate_w2[l], ((0, LANES - GATE_RANK), (0, 0))).astype(BF16),
                "gate_bias": b_gate_bias[l][None, :],
                "qk_gain": jnp.concatenate([jnp.tile(a_q_norm[l][HEAD_PERM], A_HEADS),
                                            jnp.tile(a_k_norm[l][HEAD_PERM], A_HEADS)])[None, :],
                "rope_inv": rope_inv,
                "b_out_norm": b_out_norm[l][None, :],
                "w_a_up": w_a_up[l].astype(BF16), "w_b_up": w_b_up[l].astype(BF16), "w_out": w_out[l].astype(BF16),
            }
            xb = _layer(xb, pos, p)
        outs.append(xb)
    return jnp.stack(outs, axis=0)
```

```python
import functools

import jax
import jax.numpy as jnp
import numpy as np
from jax import lax
from jax.experimental import pallas as pl
from jax.experimental.pallas import tpu as pltpu

F32 = jnp.float32
BF16 = jnp.bfloat16

D_MODEL = 2048
D_FF = 5632
RMS_EPS = 1e-6
ROPE_THETA = 500000.0
A_HEAD_DIM = 128
A_HEADS = 8
A_WIDTH = A_HEADS * A_HEAD_DIM
ROPE_DIM = A_HEAD_DIM // 4
ROPE_HALF = ROPE_DIM // 2
DILATIONS = (1, 4, 16)
A_SPAN = 128
B_HEADS = 4
B_VAL_DIM = 256
B_KEY_DIM = 128
B_KEY_WIDTH = B_HEADS * B_KEY_DIM
B_VAL_WIDTH = B_HEADS * B_VAL_DIM
GATE_RANK = 16
GATE_NORMALIZER = 16.0
GLA_CHUNK = 64
GLA_SUB = 8
LANES = 128
MXU_N = 256

VMEM_LIMIT = 56 * 1024 * 1024


def _params(sem):
    return pltpu.CompilerParams(dimension_semantics=sem, vmem_limit_bytes=VMEM_LIMIT)


def _rms(x, g):
    return x * lax.rsqrt(jnp.mean(x * x, axis=-1, keepdims=True) + RMS_EPS) * g


def _dot(a, b):
    return jnp.dot(a, b, preferred_element_type=F32)


def _dot_nt(a, b):
    return lax.dot_general(a, b, (((1,), (1,)), ((), ())), preferred_element_type=F32)


def _dot_tn(a, b):
    return lax.dot_general(a, b, (((0,), (0,)), ((), ())), preferred_element_type=F32)


def _ffn_kernel(x_ref, g_ref, wg_ref, wu_ref, wd_ref, *rest, emit_next):
    if emit_next:
        gn_ref, o_ref, hn_ref, h_scr = rest
    else:
        o_ref, h_scr = rest
    f = pl.program_id(1)

    @pl.when(f == 0)
    def _():
        x = x_ref[...]
        h_scr[...] = _rms(x, g_ref[...]).astype(BF16)
        o_ref[...] = x

    h = h_scr[...]
    gate = _dot(h, wg_ref[...])
    up = _dot(h, wu_ref[...])
    act = (0.5 * (gate * jax.nn.sigmoid(gate)) * up).astype(BF16)
    o_ref[...] += _dot(act, wd_ref[...])

    if emit_next:
        @pl.when(f == pl.num_programs(1) - 1)
        def _():
            hn_ref[...] = _rms(o_ref[...], gn_ref[...]).astype(BF16)


def _ffn(x, gain, wg, wu, wd, next_gain=None, *, tm=512, tf=512):
    S, D = x.shape
    F = wg.shape[1]
    emit_next = next_gain is not None
    row = lambda i, f: (i, 0)
    fixed = lambda i, f: (0, 0)
    in_specs = [
        pl.BlockSpec((tm, D), row),
        pl.BlockSpec((1, D), fixed),
        pl.BlockSpec((D, tf), lambda i, f: (0, f)),
        pl.BlockSpec((D, tf), lambda i, f: (0, f)),
        pl.BlockSpec((tf, D), lambda i, f: (f, 0)),
    ]
    args = [x, gain, wg, wu, wd]
    out_shape = [jax.ShapeDtypeStruct((S, D), F32)]
    out_specs = [pl.BlockSpec((tm, D), row)]
    if emit_next:
        in_specs.append(pl.BlockSpec((1, D), fixed))
        args.append(next_gain)
        out_shape.append(jax.ShapeDtypeStruct((S, D), BF16))
        out_specs.append(pl.BlockSpec((tm, D), row))
    res = pl.pallas_call(
        functools.partial(_ffn_kernel, emit_next=emit_next),
        grid=(S // tm, F // tf),
        in_specs=in_specs,
        out_specs=out_specs,
        out_shape=out_shape,
        scratch_shapes=[pltpu.VMEM((tm, D), BF16)],
        compiler_params=_params(("parallel", "arbitrary")),
        name="ffn_next" if emit_next else "ffn",
    )(*args)
    return res if emit_next else res[0]


def _proj_kernel(a_ref, w_ref, o_ref):
    o_ref[...] = _dot(a_ref[...], w_ref[...]).astype(o_ref.dtype)


def _proj(a, w, out_dtype, *, tm=1024, tn=512):
    S, K = a.shape
    N = w.shape[1]
    return pl.pallas_call(
        _proj_kernel,
        grid=(S // tm, N // tn),
        in_specs=[pl.BlockSpec((tm, K), lambda i, j: (i, 0)),
                  pl.BlockSpec((K, tn), lambda i, j: (0, j))],
        out_specs=pl.BlockSpec((tm, tn), lambda i, j: (i, j)),
        out_shape=jax.ShapeDtypeStruct((S, N), out_dtype),
        compiler_params=_params(("parallel", "arbitrary")),
        name="proj_" + jnp.dtype(out_dtype).name,
    )(a, w)


ROPE_GAP = A_HEAD_DIM // 2
HEAD_PERM = np.concatenate([np.arange(0, ROPE_HALF), np.arange(ROPE_DIM, ROPE_DIM + ROPE_GAP - ROPE_HALF),
                            np.arange(ROPE_HALF, ROPE_DIM), np.arange(ROPE_DIM + ROPE_GAP - ROPE_HALF, A_HEAD_DIM)])


def _proj_qk_kernel(a_ref, w_ref, g_ref, pos_ref, inv_ref, o_ref, raw):
    i = pl.program_id(0)

    @pl.when(i == 0)
    def _():
        raw[1] = jnp.zeros(raw.shape[1:], raw.dtype)

    cur = i % 2
    a = a_ref[...]
    for g in range(raw.shape[2] // MXU_N):
        cols = slice(g * MXU_N, (g + 1) * MXU_N)
        raw[cur, :, cols] = _dot(a, w_ref[:, cols])

    ang = pos_ref[...].astype(F32) * inv_ref[...]
    lane = lax.broadcasted_iota(jnp.int32, ang.shape, 1)
    cosf = jnp.cos(ang)
    sinf = jnp.where(lane < ROPE_GAP, -jnp.sin(ang), jnp.sin(ang))
    nh = o_ref.shape[0]
    t = jnp.stack([raw[1 - cur, :, h * A_HEAD_DIM:(h + 1) * A_HEAD_DIM] for h in range(nh)], axis=0)
    g = jnp.stack([g_ref[:, h * A_HEAD_DIM:(h + 1) * A_HEAD_DIM] for h in range(nh)], axis=0)
    y = _rms(t, g)
    o_ref[...] = (y * cosf[None] + pltpu.roll(y, ROPE_GAP, 2) * sinf[None]).astype(o_ref.dtype)


def _proj_qk(a, w, gains, pos, inv, *, tm=512):
    S, K = a.shape
    N = w.shape[1]
    nt = S // tm
    lag = lambda i: jnp.maximum(i - 1, 0)
    return pl.pallas_call(
        _proj_qk_kernel,
        grid=(nt + 1,),
        in_specs=[pl.BlockSpec((tm, K), lambda i: (jnp.minimum(i, nt - 1), 0)),
                  pl.BlockSpec((K, N), lambda i: (0, 0)),
                  pl.BlockSpec((1, N), lambda i: (0, 0)),
                  pl.BlockSpec((tm, 1), lambda i: (lag(i), 0)),
                  pl.BlockSpec((1, A_HEAD_DIM), lambda i: (0, 0))],
        out_specs=pl.BlockSpec((N // A_HEAD_DIM, tm, A_HEAD_DIM), lambda i: (0, lag(i), 0)),
        out_shape=jax.ShapeDtypeStruct((N // A_HEAD_DIM, S, A_HEAD_DIM), BF16),
        scratch_shapes=[pltpu.VMEM((2, tm, N), F32)],
        compiler_params=_params(("arbitrary",)),
        name="proj_qk",
    )(a, w, gains, pos, inv)


def _proj_v_kernel(a_ref, w_ref, oa_ref, ob_ref):
    a = a_ref[...]
    heads_per_dot = MXU_N // A_HEAD_DIM
    for g in range(A_WIDTH // MXU_N):
        acc = _dot(a, w_ref[:, g * MXU_N:(g + 1) * MXU_N])
        for hh in range(heads_per_dot):
            oa_ref[g * heads_per_dot + hh] = acc[:, hh * A_HEAD_DIM:(hh + 1) * A_HEAD_DIM].astype(oa_ref.dtype)
    for g in range(B_VAL_WIDTH // MXU_N):
        cols = slice(g * MXU_N, (g + 1) * MXU_N)
        ob_ref[:, cols] = _dot(a, w_ref[:, A_WIDTH + g * MXU_N:A_WIDTH + (g + 1) * MXU_N]).astype(ob_ref.dtype)


def _proj_v(a, w, *, tm=1024):
    S, K = a.shape
    return pl.pallas_call(
        _proj_v_kernel,
        grid=(S // tm,),
        in_specs=[pl.BlockSpec((tm, K), lambda i: (i, 0)),
                  pl.BlockSpec(w.shape, lambda i: (0, 0))],
        out_specs=[pl.BlockSpec((A_HEADS, tm, A_HEAD_DIM), lambda i: (0, i, 0)),
                   pl.BlockSpec((tm, B_VAL_WIDTH), lambda i: (i, 0))],
        out_shape=[jax.ShapeDtypeStruct((A_HEADS, S, A_HEAD_DIM), BF16),
                   jax.ShapeDtypeStruct((S, B_VAL_WIDTH), BF16)],
        compiler_params=_params(("parallel",)),
        name="proj_v",
    )(a, w)


def _gate_kernel(a_ref, wz_ref, w2_ref, bias_ref, o_ref):
    z = _dot(a_ref[...], wz_ref[...])
    pre = _dot(z.astype(BF16), w2_ref[...]) + bias_ref[...]
    g = (jnp.minimum(pre, 0.0) - jnp.log1p(jnp.exp(-jnp.abs(pre)))) / GATE_NORMALIZER
    r = lax.broadcasted_iota(jnp.int32, g.shape, 0) % GLA_CHUNK
    shift = 1
    while shift < GLA_CHUNK:
        g = g + jnp.where(r >= shift, pltpu.roll(g, shift, 0), 0.0)
        shift *= 2
    o_ref[...] = g


def _gate(a, wz, w2, bias, *, tm=1024):
    S, K = a.shape
    return pl.pallas_call(
        _gate_kernel,
        grid=(S // tm,),
        in_specs=[pl.BlockSpec((tm, K), lambda i: (i, 0)),
                  pl.BlockSpec(wz.shape, lambda i: (0, 0)),
                  pl.BlockSpec(w2.shape, lambda i: (0, 0)),
                  pl.BlockSpec(bias.shape, lambda i: (0, 0))],
        out_specs=pl.BlockSpec((tm, B_KEY_WIDTH), lambda i: (i, 0)),
        out_shape=jax.ShapeDtypeStruct((S, B_KEY_WIDTH), F32),
        compiler_params=_params(("parallel",)),
        name="gate",
    )(a, wz, w2, bias)


A_BLK = 128
A_CHUNK = A_BLK * max(DILATIONS)


def _attn_kernel(q_ref, kp_ref, kc_ref, vp_ref, vc_ref, o_ref, qf, kf, vf, acc_s, m_s, l_s, *, group):
    L = A_CHUNK
    c = pl.program_id(1)
    qf[...] = q_ref[...].astype(F32)
    kf[0:L, :] = kp_ref[...].astype(F32)
    kf[L:2 * L, :] = kc_ref[...].astype(F32)
    vf[0:L, :] = vp_ref[...].astype(F32)
    vf[L:2 * L, :] = vc_ref[...].astype(F32)
    scale = A_HEAD_DIM ** -0.5
    row = lax.broadcasted_iota(jnp.int32, (1, A_BLK, 2 * A_BLK), 1)
    col = lax.broadcasted_iota(jnp.int32, (1, A_BLK, 2 * A_BLK), 2)
    band = jnp.logical_and(col >= row, col <= row + A_SPAN)
    first_chunk = (c == 0).astype(jnp.int32)
    for pi, d in enumerate(DILATIONS):
        tiles = [(r, n) for r in range(d) for n in range(L // (A_BLK * d))]
        for g0 in range(0, len(tiles), group):
            grp = tiles[g0:g0 + group]
            starts = [n * A_BLK * d + r for r, n in grp]
            keys = [pl.ds(L + st - A_BLK * d, 2 * A_BLK, stride=d) for st in starts]
            s = jnp.concatenate(
                [_dot_nt(qf[pl.ds(st, A_BLK, stride=d), :].astype(BF16), kf[ks, :].astype(BF16))
                 for st, ks in zip(starts, keys)], axis=0).reshape(len(grp), A_BLK, 2 * A_BLK)
            gidx = lax.broadcasted_iota(jnp.int32, (len(grp), 1, 1), 0)
            first = functools.reduce(jnp.logical_or, [gidx == gi for gi, (r, n) in enumerate(grp) if n == 0],
                                     gidx < 0)
            first_col = jnp.where(first, A_BLK, 0) * first_chunk
            s = jnp.where(jnp.logical_and(band, col >= first_col), s * scale, -jnp.inf)
            m = jnp.max(s, axis=-1, keepdims=True)
            p = jnp.exp(s - m)
            l = jnp.sum(p, axis=-1, keepdims=True)
            pb = p.astype(BF16)
            mb = jnp.broadcast_to(m, (len(grp), A_BLK, A_HEAD_DIM))
            lb = jnp.broadcast_to(l, (len(grp), A_BLK, A_HEAD_DIM))
            for gi, (st, ks) in enumerate(zip(starts, keys)):
                cur = pl.ds(st, A_BLK, stride=d)
                acc_s[pi, cur, :] = _dot(pb[gi], vf[ks, :].astype(BF16))
                m_s[pi, cur, :] = mb[gi]
                l_s[pi, cur, :] = lb[gi]
    ms = [m_s[pi] for pi in range(len(DILATIONS))]
    m = functools.reduce(jnp.maximum, ms)
    ws = [jnp.exp(mi - m) for mi in ms]
    num = sum(w * acc_s[pi] for pi, w in enumerate(ws))
    den = sum(w * l_s[pi] for pi, w in enumerate(ws))
    o_ref[...] = (num / den).astype(o_ref.dtype)


def _attn(qk, v, *, group=8):
    S = v.shape[1]
    L = A_CHUNK
    npat = len(DILATIONS)
    blk = (None, L, A_HEAD_DIM)
    prev = lambda c: jnp.maximum(c - 1, 0)
    return pl.pallas_call(
        functools.partial(_attn_kernel, group=group),
        grid=(A_HEADS, S // L),
        in_specs=[pl.BlockSpec(blk, lambda h, c: (h, c, 0)),
                  pl.BlockSpec(blk, lambda h, c: (A_HEADS + h, prev(c), 0)),
                  pl.BlockSpec(blk, lambda h, c: (A_HEADS + h, c, 0)),
                  pl.BlockSpec(blk, lambda h, c: (h, prev(c), 0)),
                  pl.BlockSpec(blk, lambda h, c: (h, c, 0))],
        out_specs=pl.BlockSpec((L, A_HEAD_DIM), lambda h, c: (c, h)),
        out_shape=jax.ShapeDtypeStruct((S, A_WIDTH), BF16),
        scratch_shapes=[pltpu.VMEM((L, A_HEAD_DIM), F32),
                        pltpu.VMEM((2 * L, A_HEAD_DIM), F32),
                        pltpu.VMEM((2 * L, A_HEAD_DIM), F32),
                        pltpu.VMEM((npat, L, A_HEAD_DIM), F32),
                        pltpu.VMEM((npat, L, A_HEAD_DIM), F32),
                        pltpu.VMEM((npat, L, A_HEAD_DIM), F32)],
        compiler_params=_params(("parallel", "arbitrary")),
        name="attn",
    )(qk, qk, qk, v, v)


def _gla_kernel(q_ref, k_ref, v_ref, b_ref, r_ref, gn_ref, o_ref, st_ref, *, nchunk):
    C, Cs = GLA_CHUNK, GLA_SUB
    NS = C // Cs

    @pl.when(pl.program_id(0) == 0)
    def _():
        st_ref[...] = jnp.zeros_like(st_ref)

    row = lax.broadcasted_iota(jnp.int32, (C, C), 0)
    col = lax.broadcasted_iota(jnp.int32, (C, C), 1)
    sub_start = (row // Cs) * Cs
    sub_row = lax.broadcasted_iota(jnp.int32, (NS, Cs, B_KEY_DIM), 1)
    levels = []
    z = C // 2
    while z >= Cs:
        levels.append((z, jnp.logical_and((row // z) % 2 == 1, col // z == row // z - 1)))
        z //= 2
    diag_masks = [col == sub_start + j for j in range(Cs)]
    for c in range(nchunk):
        rows = slice(c * C, (c + 1) * C)
        for h in range(B_HEADS):
            kcols = slice(h * B_KEY_DIM, (h + 1) * B_KEY_DIM)
            vcols = slice(h * B_VAL_DIM, (h + 1) * B_VAL_DIM)
            q = q_ref[rows, kcols] * (B_KEY_DIM ** -0.5)
            k = k_ref[rows, kcols]
            b = b_ref[rows, kcols]
            v = v_ref[rows, vcols]
            b_last = b[C - 1:C, :]
            st = st_ref[h]
            o = _dot_nt((q * jnp.exp(b)).astype(BF16), st.astype(BF16))
            k_end = (k * jnp.exp(b_last - b)).astype(BF16)
            st_ref[h] = st * jnp.exp(b_last) + _dot_tn(v, k_end)

            a = jnp.zeros((C, C), F32)
            for z, mask in levels:
                bz = b.reshape(C // z, z, B_KEY_DIM)
                ends = bz[:, z - 1:z, :]
                starts = jnp.concatenate([jnp.zeros_like(ends[:1]), ends[:-1]], axis=0)
                q_z = (q * jnp.exp(bz - starts).reshape(C, B_KEY_DIM)).astype(BF16)
                k_z = (k * jnp.exp(ends - bz).reshape(C, B_KEY_DIM)).astype(BF16)
                a = jnp.where(mask, _dot_nt(q_z, k_z), a)

            q3 = q.reshape(NS, Cs, B_KEY_DIM)
            k3 = k.reshape(NS, Cs, B_KEY_DIM)
            b3 = b.reshape(NS, Cs, B_KEY_DIM)
            for j in range(Cs):
                kb = jnp.broadcast_to(k3[:, j:j + 1, :], q3.shape)
                bb = jnp.broadcast_to(b3[:, j:j + 1, :], q3.shape)
                w = jnp.exp(jnp.where(sub_row >= j, b3 - bb, -jnp.inf))
                dj = jnp.sum((q3 * kb * w).reshape(C, B_KEY_DIM), axis=-1, keepdims=True)
                a = jnp.where(diag_masks[j], dj, a)
            o = o + _dot(a.astype(BF16), v)
            r = r_ref[rows, vcols]
            o_ref[rows, vcols] = (_rms(o, gn_ref[...]) * (r * jax.nn.sigmoid(r))).astype(o_ref.dtype)


def _gla(pf, bv, b, gain, *, rb=256):
    S = pf.shape[0]
    return pl.pallas_call(
        functools.partial(_gla_kernel, nchunk=rb // GLA_CHUNK),
        grid=(S // rb,),
        in_specs=[pl.BlockSpec((rb, B_KEY_WIDTH), lambda i: (i, 0)),
                  pl.BlockSpec((rb, B_KEY_WIDTH), lambda i: (i, 1)),
                  pl.BlockSpec((rb, B_VAL_WIDTH), lambda i: (i, 0)),
                  pl.BlockSpec((rb, B_KEY_WIDTH), lambda i: (i, 0)),
                  pl.BlockSpec((rb, B_VAL_WIDTH), lambda i: (i, 1)),
                  pl.BlockSpec((1, B_VAL_DIM), lambda i: (0, 0))],
        out_specs=pl.BlockSpec((rb, B_VAL_WIDTH), lambda i: (i, 0)),
        out_shape=jax.ShapeDtypeStruct((S, B_VAL_WIDTH), BF16),
        scratch_shapes=[pltpu.VMEM((B_HEADS, B_VAL_DIM, B_KEY_DIM), F32)],
        compiler_params=_params(("arbitrary",)),
        name="gla",
    )(pf, pf, bv, b, pf, gain)


def _merge_out_kernel(x_ref, oa_ref, ob_ref, ga_ref, gb_ref, wa_ref, wb_ref, wo_ref, o_ref):
    @pl.when(pl.program_id(1) == 0)
    def _():
        o_ref[...] = x_ref[...]

    ya = _dot(oa_ref[...], wa_ref[...])
    yb = _dot(ob_ref[...], wb_ref[...])
    y = jax.nn.sigmoid(ga_ref[...]) * ya + jax.nn.sigmoid(gb_ref[...]) * yb
    o_ref[...] += _dot(y.astype(BF16), wo_ref[...])


def _merge_out(x, oa, ob, pf, wa, wb, wo, *, tm=512, tn=512):
    S, D = x.shape
    ga_blk = 2048 // tn
    gb_blk = 4096 // tn
    row = lambda i, j: (i, 0)
    return pl.pallas_call(
        _merge_out_kernel,
        grid=(S // tm, D // tn),
        in_specs=[pl.BlockSpec((tm, D), row),
                  pl.BlockSpec((tm, A_WIDTH), row),
                  pl.BlockSpec((tm, B_VAL_WIDTH), row),
                  pl.BlockSpec((tm, tn), lambda i, j: (i, ga_blk + j)),
                  pl.BlockSpec((tm, tn), lambda i, j: (i, gb_blk + j)),
                  pl.BlockSpec((A_WIDTH, tn), lambda i, j: (0, j)),
                  pl.BlockSpec((B_VAL_WIDTH, tn), lambda i, j: (0, j)),
                  pl.BlockSpec((tn, D), lambda i, j: (j, 0))],
        out_specs=pl.BlockSpec((tm, D), row),
        out_shape=jax.ShapeDtypeStruct((S, D), F32),
        compiler_params=_params(("parallel", "arbitrary")),
        name="merge_out",
    )(x, oa, ob, pf, pf, wa, wb, wo)


def _layer(x, pos, p):
    x1, h = _ffn(x, p["ffn1_norm"], p["ffn1_wg"], p["ffn1_wu"], p["ffn1_wd"], p["mix_norm"])
    qk = _proj_qk(h, p["w_qk"], p["qk_gain"], pos, p["rope_inv"])
    av, bv = _proj_v(h, p["w_v"])
    pf = _proj(h, p["w_f"], F32)
    b = _gate(h, p["w_z"], p["w_2"], p["gate_bias"])
    o_a = _attn(qk, av)
    o_b = _gla(pf, bv, b, p["b_out_norm"])
    x2 = _merge_out(x1, o_a, o_b, pf, p["w_a_up"], p["w_b_up"], p["w_out"])
    return _ffn(x2, p["ffn2_norm"], p["ffn2_wg"], p["ffn2_wu"], p["ffn2_wd"])


def kernel(x, positions, ffn1_norm, ffn1_w_gate, ffn1_w_up, ffn1_w_down, mix_norm, w_in, a_q_norm, a_k_norm, b_gate_w2, b_gate_bias, b_out_norm, w_a_up, w_b_up, w_out, ffn2_norm, ffn2_w_gate, ffn2_w_up, ffn2_w_down):
    B, S, D = x.shape
    assert D == D_MODEL and S % A_CHUNK == 0
    depth = w_in.shape[0]
    c = np.cumsum([0, A_WIDTH, A_WIDTH, A_WIDTH, B_KEY_WIDTH, B_KEY_WIDTH, B_VAL_WIDTH, B_VAL_WIDTH,
                   GATE_RANK, D_MODEL, D_MODEL]).tolist()
    inv = jnp.power(ROPE_THETA, -(jnp.arange(ROPE_HALF, dtype=F32) * 2.0 / ROPE_DIM))
    rope_inv = jnp.concatenate([inv, inv, jnp.zeros((A_HEAD_DIM - ROPE_DIM,), F32)])[HEAD_PERM][None, :]
    qk_cols = (np.arange(2 * A_HEADS)[:, None] * A_HEAD_DIM + HEAD_PERM[None, :]).reshape(-1)
    outs = []
    for bi in range(B):
        xb = x.reshape(S, D) if B == 1 else x[bi]
        pos = positions.reshape(S, 1) if B == 1 else positions[bi][:, None]
        for l in range(depth):
            w = w_in[l]
            p = {
                "ffn1_norm": ffn1_norm[l][None, :], "mix_norm": mix_norm[l][None, :],
                "ffn2_norm": ffn2_norm[l][None, :],
                "ffn1_wg": ffn1_w_gate[l].astype(BF16), "ffn1_wu": ffn1_w_up[l].astype(BF16),
                "ffn1_wd": ffn1_w_down[l].astype(BF16),
                "ffn2_wg": ffn2_w_gate[l].astype(BF16), "ffn2_wu": ffn2_w_up[l].astype(BF16),
                "ffn2_wd": ffn2_w_down[l].astype(BF16),
                "w_qk": w[:, c[0]:c[2]][:, qk_cols].astype(BF16),
                "w_v": jnp.concatenate([w[:, c[2]:c[3]], w[:, c[5]:c[6]]], axis=1).astype(BF16),
                "w_f": jnp.concatenate([w[:, c[3]:c[5]], w[:, c[6]:c[7]], w[:, c[8]:c[10]]], axis=1).astype(BF16),
                "w_z": jnp.pad(w[:, c[7]:c[8]], ((0, 0), (0, LANES - GATE_RANK))).astype(BF16),
                "w_2": jnp.pad(b_gate_w2[l], ((0, LANES - GATE_RANK), (0, 0))).astype(BF16),
                "gate_bias": b_gate_bias[l][None, :],
                "qk_gain": jnp.concatenate([jnp.tile(a_q_norm[l][HEAD_PERM], A_HEADS),
                                            jnp.tile(a_k_norm[l][HEAD_PERM], A_HEADS)])[None, :],
                "rope_inv": rope_inv,
                "b_out_norm": b_out_norm[l][None, :],
                "w_a_up": w_a_up[l].astype(BF16), "w_b_up": w_b_up[l].astype(BF16), "w_out": w_out[l].astype(BF16),
            }
            xb = _layer(xb, pos, p)
        outs.append(xb)
    return outs[0].reshape(B, S, D) if B == 1 else jnp.stack(outs, axis=0)
```

```python
import functools

import jax
import jax.numpy as jnp
import numpy as np
from jax import lax
from jax.experimental import pallas as pl
from jax.experimental.pallas import tpu as pltpu

F32 = jnp.float32
BF16 = jnp.bfloat16

D_MODEL = 2048
D_FF = 5632
RMS_EPS = 1e-6
ROPE_THETA = 500000.0
A_HEAD_DIM = 128
A_HEADS = 8
A_WIDTH = A_HEADS * A_HEAD_DIM
ROPE_DIM = A_HEAD_DIM // 4
ROPE_HALF = ROPE_DIM // 2
DILATIONS = (1, 4, 16)
A_SPAN = 128
B_HEADS = 4
B_VAL_DIM = 256
B_KEY_DIM = 128
B_KEY_WIDTH = B_HEADS * B_KEY_DIM
B_VAL_WIDTH = B_HEADS * B_VAL_DIM
GATE_RANK = 16
GATE_NORMALIZER = 16.0
GLA_CHUNK = 64
GLA_SUB = 8
LANES = 128
MXU_N = 256

VMEM_LIMIT = 56 * 1024 * 1024


def _params(sem):
    return pltpu.CompilerParams(dimension_semantics=sem, vmem_limit_bytes=VMEM_LIMIT)


def _rms(x, g):
    return x * lax.rsqrt(jnp.mean(x * x, axis=-1, keepdims=True) + RMS_EPS) * g


def _dot(a, b):
    return jnp.dot(a, b, preferred_element_type=F32)


def _dot_nt(a, b):
    return lax.dot_general(a, b, (((1,), (1,)), ((), ())), preferred_element_type=F32)


def _dot_tn(a, b):
    return lax.dot_general(a, b, (((0,), (0,)), ((), ())), preferred_element_type=F32)


def _ffn_kernel(x_ref, g_ref, wg_ref, wu_ref, wd_ref, *rest, emit_next):
    if emit_next:
        gn_ref, o_ref, hn_ref, h_scr = rest
    else:
        o_ref, h_scr = rest
    f = pl.program_id(1)

    @pl.when(f == 0)
    def _():
        x = x_ref[...]
        h_scr[...] = _rms(x, g_ref[...]).astype(BF16)
        o_ref[...] = x

    h = h_scr[...]
    gate = _dot(h, wg_ref[...])
    up = _dot(h, wu_ref[...])
    act = (0.5 * (gate * jax.nn.sigmoid(gate)) * up).astype(BF16)
    o_ref[...] += _dot(act, wd_ref[...])

    if emit_next:
        @pl.when(f == pl.num_programs(1) - 1)
        def _():
            hn_ref[...] = _rms(o_ref[...], gn_ref[...]).astype(BF16)


def _ffn(x, gain, wg, wu, wd, next_gain=None, *, tm=1024, tf=256):
    S, D = x.shape
    F = wg.shape[1]
    emit_next = next_gain is not None
    row = lambda i, f: (i, 0)
    fixed = lambda i, f: (0, 0)
    in_specs = [
        pl.BlockSpec((tm, D), row),
        pl.BlockSpec((1, D), fixed),
        pl.BlockSpec((D, tf), lambda i, f: (0, f)),
        pl.BlockSpec((D, tf), lambda i, f: (0, f)),
        pl.BlockSpec((tf, D), lambda i, f: (f, 0)),
    ]
    args = [x, gain, wg, wu, wd]
    out_shape = [jax.ShapeDtypeStruct((S, D), F32)]
    out_specs = [pl.BlockSpec((tm, D), row)]
    if emit_next:
        in_specs.append(pl.BlockSpec((1, D), fixed))
        args.append(next_gain)
        out_shape.append(jax.ShapeDtypeStruct((S, D), BF16))
        out_specs.append(pl.BlockSpec((tm, D), row))
    res = pl.pallas_call(
        functools.partial(_ffn_kernel, emit_next=emit_next),
        grid=(S // tm, F // tf),
        in_specs=in_specs,
        out_specs=out_specs,
        out_shape=out_shape,
        scratch_shapes=[pltpu.VMEM((tm, D), BF16)],
        compiler_params=_params(("parallel", "arbitrary")),
        name="ffn_next" if emit_next else "ffn",
    )(*args)
    return res if emit_next else res[0]


def _proj_kernel(a_ref, w_ref, o_ref):
    o_ref[...] = _dot(a_ref[...], w_ref[...]).astype(o_ref.dtype)


def _proj(a, w, out_dtype, *, tm=1024, tn=512):
    S, K = a.shape
    N = w.shape[1]
    return pl.pallas_call(
        _proj_kernel,
        grid=(S // tm, N // tn),
        in_specs=[pl.BlockSpec((tm, K), lambda i, j: (i, 0)),
                  pl.BlockSpec((K, tn), lambda i, j: (0, j))],
        out_specs=pl.BlockSpec((tm, tn), lambda i, j: (i, j)),
        out_shape=jax.ShapeDtypeStruct((S, N), out_dtype),
        compiler_params=_params(("parallel", "arbitrary")),
        name="proj_" + jnp.dtype(out_dtype).name,
    )(a, w)


ROPE_GAP = A_HEAD_DIM // 2
HEAD_PERM = np.concatenate([np.arange(0, ROPE_HALF), np.arange(ROPE_DIM, ROPE_DIM + ROPE_GAP - ROPE_HALF),
                            np.arange(ROPE_HALF, ROPE_DIM), np.arange(ROPE_DIM + ROPE_GAP - ROPE_HALF, A_HEAD_DIM)])


def _proj_qk_kernel(a_ref, w_ref, g_ref, pos_ref, inv_ref, o_ref, raw):
    i = pl.program_id(0)

    @pl.when(i == 0)
    def _():
        raw[1] = jnp.zeros(raw.shape[1:], raw.dtype)

    cur = i % 2
    a = a_ref[...]
    for g in range(raw.shape[2] // MXU_N):
        cols = slice(g * MXU_N, (g + 1) * MXU_N)
        raw[cur, :, cols] = _dot(a, w_ref[:, cols])

    ang = pos_ref[...].astype(F32) * inv_ref[...]
    lane = lax.broadcasted_iota(jnp.int32, ang.shape, 1)
    cosf = jnp.cos(ang)
    sinf = jnp.where(lane < ROPE_GAP, -jnp.sin(ang), jnp.sin(ang))
    nh = o_ref.shape[0]
    t = jnp.stack([raw[1 - cur, :, h * A_HEAD_DIM:(h + 1) * A_HEAD_DIM] for h in range(nh)], axis=0)
    g = jnp.stack([g_ref[:, h * A_HEAD_DIM:(h + 1) * A_HEAD_DIM] for h in range(nh)], axis=0)
    y = _rms(t, g)
    o_ref[...] = (y * cosf[None] + pltpu.roll(y, ROPE_GAP, 2) * sinf[None]).astype(o_ref.dtype)


def _proj_qk(a, w, gains, pos, inv, *, tm=512):
    S, K = a.shape
    N = w.shape[1]
    nt = S // tm
    lag = lambda i: jnp.maximum(i - 1, 0)
    return pl.pallas_call(
        _proj_qk_kernel,
        grid=(nt + 1,),
        in_specs=[pl.BlockSpec((tm, K), lambda i: (jnp.minimum(i, nt - 1), 0)),
                  pl.BlockSpec((K, N), lambda i: (0, 0)),
                  pl.BlockSpec((1, N), lambda i: (0, 0)),
                  pl.BlockSpec((tm, 1), lambda i: (lag(i), 0)),
                  pl.BlockSpec((1, A_HEAD_DIM), lambda i: (0, 0))],
        out_specs=pl.BlockSpec((N // A_HEAD_DIM, tm, A_HEAD_DIM), lambda i: (0, lag(i), 0)),
        out_shape=jax.ShapeDtypeStruct((N // A_HEAD_DIM, S, A_HEAD_DIM), BF16),
        scratch_shapes=[pltpu.VMEM((2, tm, N), F32)],
        compiler_params=_params(("arbitrary",)),
        name="proj_qk",
    )(a, w, gains, pos, inv)


def _proj_v_kernel(a_ref, w_ref, oa_ref, ob_ref):
    a = a_ref[...]
    heads_per_dot = MXU_N // A_HEAD_DIM
    for g in range(A_WIDTH // MXU_N):
        acc = _dot(a, w_ref[:, g * MXU_N:(g + 1) * MXU_N])
        for hh in range(heads_per_dot):
            oa_ref[g * heads_per_dot + hh] = acc[:, hh * A_HEAD_DIM:(hh + 1) * A_HEAD_DIM].astype(oa_ref.dtype)
    for g in range(B_VAL_WIDTH // MXU_N):
        cols = slice(g * MXU_N, (g + 1) * MXU_N)
        ob_ref[:, cols] = _dot(a, w_ref[:, A_WIDTH + g * MXU_N:A_WIDTH + (g + 1) * MXU_N]).astype(ob_ref.dtype)


def _proj_v(a, w, *, tm=1024):
    S, K = a.shape
    return pl.pallas_call(
        _proj_v_kernel,
        grid=(S // tm,),
        in_specs=[pl.BlockSpec((tm, K), lambda i: (i, 0)),
                  pl.BlockSpec(w.shape, lambda i: (0, 0))],
        out_specs=[pl.BlockSpec((A_HEADS, tm, A_HEAD_DIM), lambda i: (0, i, 0)),
                   pl.BlockSpec((tm, B_VAL_WIDTH), lambda i: (i, 0))],
        out_shape=[jax.ShapeDtypeStruct((A_HEADS, S, A_HEAD_DIM), BF16),
                   jax.ShapeDtypeStruct((S, B_VAL_WIDTH), BF16)],
        compiler_params=_params(("parallel",)),
        name="proj_v",
    )(a, w)


def _gate_kernel(a_ref, wz_ref, w2_ref, bias_ref, o_ref):
    z = _dot(a_ref[...], wz_ref[...])
    pre = _dot(z.astype(BF16), w2_ref[...]) + bias_ref[...]
    g = (jnp.minimum(pre, 0.0) - jnp.log1p(jnp.exp(-jnp.abs(pre)))) / GATE_NORMALIZER
    r = lax.broadcasted_iota(jnp.int32, g.shape, 0) % GLA_CHUNK
    shift = 1
    while shift < GLA_CHUNK:
        g = g + jnp.where(r >= shift, pltpu.roll(g, shift, 0), 0.0)
        shift *= 2
    o_ref[...] = g


def _gate(a, wz, w2, bias, *, tm=1024):
    S, K = a.shape
    return pl.pallas_call(
        _gate_kernel,
        grid=(S // tm,),
        in_specs=[pl.BlockSpec((tm, K), lambda i: (i, 0)),
                  pl.BlockSpec(wz.shape, lambda i: (0, 0)),
                  pl.BlockSpec(w2.shape, lambda i: (0, 0)),
                  pl.BlockSpec(bias.shape, lambda i: (0, 0))],
        out_specs=pl.BlockSpec((tm, B_KEY_WIDTH), lambda i: (i, 0)),
        out_shape=jax.ShapeDtypeStruct((S, B_KEY_WIDTH), F32),
        compiler_params=_params(("parallel",)),
        name="gate",
    )(a, wz, w2, bias)


A_BLK = 128
A_CHUNK = A_BLK * max(DILATIONS)


def _attn_kernel(q_ref, kp_ref, kc_ref, vp_ref, vc_ref, o_ref, qf, kf, vf, acc_s, m_s, l_s, *, group):
    L = A_CHUNK
    c = pl.program_id(1)
    qf[...] = q_ref[...].astype(F32)
    kf[0:L, :] = kp_ref[...].astype(F32)
    kf[L:2 * L, :] = kc_ref[...].astype(F32)
    vf[0:L, :] = vp_ref[...].astype(F32)
    vf[L:2 * L, :] = vc_ref[...].astype(F32)
    scale = A_HEAD_DIM ** -0.5
    row = lax.broadcasted_iota(jnp.int32, (1, A_BLK, 2 * A_BLK), 1)
    col = lax.broadcasted_iota(jnp.int32, (1, A_BLK, 2 * A_BLK), 2)
    band = jnp.logical_and(col >= row, col <= row + A_SPAN)
    first_chunk = (c == 0).astype(jnp.int32)
    for pi, d in enumerate(DILATIONS):
        tiles = [(r, n) for r in range(d) for n in range(L // (A_BLK * d))]
        for g0 in range(0, len(tiles), group):
            grp = tiles[g0:g0 + group]
            starts = [n * A_BLK * d + r for r, n in grp]
            keys = [pl.ds(L + st - A_BLK * d, 2 * A_BLK, stride=d) for st in starts]
            s = jnp.concatenate(
                [_dot_nt(qf[pl.ds(st, A_BLK, stride=d), :].astype(BF16), kf[ks, :].astype(BF16))
                 for st, ks in zip(starts, keys)], axis=0).reshape(len(grp), A_BLK, 2 * A_BLK)
            gidx = lax.broadcasted_iota(jnp.int32, (len(grp), 1, 1), 0)
            first = functools.reduce(jnp.logical_or, [gidx == gi for gi, (r, n) in enumerate(grp) if n == 0],
                                     gidx < 0)
            first_col = jnp.where(first, A_BLK, 0) * first_chunk
            s = jnp.where(jnp.logical_and(band, col >= first_col), s * scale, -jnp.inf)
            m = jnp.max(s, axis=-1, keepdims=True)
            p = jnp.exp(s - m)
            l = jnp.sum(p, axis=-1, keepdims=True)
            pb = p.astype(BF16)
            mb = jnp.broadcast_to(m, (len(grp), A_BLK, A_HEAD_DIM))
            lb = jnp.broadcast_to(l, (len(grp), A_BLK, A_HEAD_DIM))
            for gi, (st, ks) in enumerate(zip(starts, keys)):
                cur = pl.ds(st, A_BLK, stride=d)
                acc_s[pi, cur, :] = _dot(pb[gi], vf[ks, :].astype(BF16))
                m_s[pi, cur, :] = mb[gi]
                l_s[pi, cur, :] = lb[gi]
    ms = [m_s[pi] for pi in range(len(DILATIONS))]
    m = functools.reduce(jnp.maximum, ms)
    ws = [jnp.exp(mi - m) for mi in ms]
    num = sum(w * acc_s[pi] for pi, w in enumerate(ws))
    den = sum(w * l_s[pi] for pi, w in enumerate(ws))
    o_ref[...] = (num / den).astype(o_ref.dtype)


def _attn(qk, v, *, group=8):
    S = v.shape[1]
    L = A_CHUNK
    npat = len(DILATIONS)
    blk = (None, L, A_HEAD_DIM)
    prev = lambda c: jnp.maximum(c - 1, 0)
    return pl.pallas_call(
        functools.partial(_attn_kernel, group=group),
        grid=(A_HEADS, S // L),
        in_specs=[pl.BlockSpec(blk, lambda h, c: (h, c, 0)),
                  pl.BlockSpec(blk, lambda h, c: (A_HEADS + h, prev(c), 0)),
                  pl.BlockSpec(blk, lambda h, c: (A_HEADS + h, c, 0)),
                  pl.BlockSpec(blk, lambda h, c: (h, prev(c), 0)),
                  pl.BlockSpec(blk, lambda h, c: (h, c, 0))],
        out_specs=pl.BlockSpec((L, A_HEAD_DIM), lambda h, c: (c, h)),
        out_shape=jax.ShapeDtypeStruct((S, A_WIDTH), BF16),
        scratch_shapes=[pltpu.VMEM((L, A_HEAD_DIM), F32),
                        pltpu.VMEM((2 * L, A_HEAD_DIM), F32),
                        pltpu.VMEM((2 * L, A_HEAD_DIM), F32),
                        pltpu.VMEM((npat, L, A_HEAD_DIM), F32),
                        pltpu.VMEM((npat, L, A_HEAD_DIM), F32),
                        pltpu.VMEM((npat, L, A_HEAD_DIM), F32)],
        compiler_params=_params(("parallel", "arbitrary")),
        name="attn",
    )(qk, qk, qk, v, v)


def _gla_kernel(q_ref, k_ref, v_ref, b_ref, r_ref, gn_ref, o_ref, st_ref, *, nchunk):
    C, Cs = GLA_CHUNK, GLA_SUB
    NS = C // Cs

    @pl.when(pl.program_id(0) == 0)
    def _():
        st_ref[...] = jnp.zeros_like(st_ref)

    row = lax.broadcasted_iota(jnp.int32, (C, C), 0)
    col = lax.broadcasted_iota(jnp.int32, (C, C), 1)
    sub_start = (row // Cs) * Cs
    sub_row = lax.broadcasted_iota(jnp.int32, (NS, Cs, B_KEY_DIM), 1)
    levels = []
    z = C // 2
    while z >= Cs:
        levels.append((z, jnp.logical_and((row // z) % 2 == 1, col // z == row // z - 1)))
        z //= 2
    diag_masks = [col == sub_start + j for j in range(Cs)]
    for c in range(nchunk):
        rows = slice(c * C, (c + 1) * C)
        for h in range(B_HEADS):
            kcols = slice(h * B_KEY_DIM, (h + 1) * B_KEY_DIM)
            vcols = slice(h * B_VAL_DIM, (h + 1) * B_VAL_DIM)
            q = q_ref[rows, kcols] * (B_KEY_DIM ** -0.5)
            k = k_ref[rows, kcols]
            b = b_ref[rows, kcols]
            v = v_ref[rows, vcols]
            b_last = b[C - 1:C, :]
            st = st_ref[h]
            o = _dot_nt((q * jnp.exp(b)).astype(BF16), st.astype(BF16))
            k_end = (k * jnp.exp(b_last - b)).astype(BF16)
            st_ref[h] = st * jnp.exp(b_last) + _dot_tn(v, k_end)

            a = jnp.zeros((C, C), F32)
            for z, mask in levels:
                bz = b.reshape(C // z, z, B_KEY_DIM)
                ends = bz[:, z - 1:z, :]
                starts = jnp.concatenate([jnp.zeros_like(ends[:1]), ends[:-1]], axis=0)
                q_z = (q * jnp.exp(bz - starts).reshape(C, B_KEY_DIM)).astype(BF16)
                k_z = (k * jnp.exp(ends - bz).reshape(C, B_KEY_DIM)).astype(BF16)
                a = jnp.where(mask, _dot_nt(q_z, k_z), a)

            q3 = q.reshape(NS, Cs, B_KEY_DIM)
            k3 = k.reshape(NS, Cs, B_KEY_DIM)
            b3 = b.reshape(NS, Cs, B_KEY_DIM)
            for j in range(Cs):
                kb = jnp.broadcast_to(k3[:, j:j + 1, :], q3.shape)
                bb = jnp.broadcast_to(b3[:, j:j + 1, :], q3.shape)
                w = jnp.exp(jnp.where(sub_row >= j, b3 - bb, -jnp.inf))
                dj = jnp.sum((q3 * kb * w).reshape(C, B_KEY_DIM), axis=-1, keepdims=True)
                a = jnp.where(diag_masks[j], dj, a)
            o = o + _dot(a.astype(BF16), v)
            r = r_ref[rows, vcols]
            o_ref[rows, vcols] = (_rms(o, gn_ref[...]) * (r * jax.nn.sigmoid(r))).astype(o_ref.dtype)


def _gla(pf, bv, b, gain, *, rb=256):
    S = pf.shape[0]
    return pl.pallas_call(
        functools.partial(_gla_kernel, nchunk=rb // GLA_CHUNK),
        grid=(S // rb,),
        in_specs=[pl.BlockSpec((rb, B_KEY_WIDTH), lambda i: (i, 0)),
                  pl.BlockSpec((rb, B_KEY_WIDTH), lambda i: (i, 1)),
                  pl.BlockSpec((rb, B_VAL_WIDTH), lambda i: (i, 0)),
                  pl.BlockSpec((rb, B_KEY_WIDTH), lambda i: (i, 0)),
                  pl.BlockSpec((rb, B_VAL_WIDTH), lambda i: (i, 1)),
                  pl.BlockSpec((1, B_VAL_DIM), lambda i: (0, 0))],
        out_specs=pl.BlockSpec((rb, B_VAL_WIDTH), lambda i: (i, 0)),
        out_shape=jax.ShapeDtypeStruct((S, B_VAL_WIDTH), BF16),
        scratch_shapes=[pltpu.VMEM((B_HEADS, B_VAL_DIM, B_KEY_DIM), F32)],
        compiler_params=_params(("arbitrary",)),
        name="gla",
    )(pf, pf, bv, b, pf, gain)


def _merge_out_kernel(x_ref, oa_ref, ob_ref, ga_ref, gb_ref, wa_ref, wb_ref, wo_ref, o_ref):
    @pl.when(pl.program_id(1) == 0)
    def _():
        o_ref[...] = x_ref[...]

    ya = _dot(oa_ref[...], wa_ref[...])
    yb = _dot(ob_ref[...], wb_ref[...])
    y = jax.nn.sigmoid(ga_ref[...]) * ya + jax.nn.sigmoid(gb_ref[...]) * yb
    o_ref[...] += _dot(y.astype(BF16), wo_ref[...])


def _merge_out(x, oa, ob, pf, wa, wb, wo, *, tm=512, tn=512):
    S, D = x.shape
    ga_blk = 2048 // tn
    gb_blk = 4096 // tn
    row = lambda i, j: (i, 0)
    return pl.pallas_call(
        _merge_out_kernel,
        grid=(S // tm, D // tn),
        in_specs=[pl.BlockSpec((tm, D), row),
                  pl.BlockSpec((tm, A_WIDTH), row),
                  pl.BlockSpec((tm, B_VAL_WIDTH), row),
                  pl.BlockSpec((tm, tn), lambda i, j: (i, ga_blk + j)),
                  pl.BlockSpec((tm, tn), lambda i, j: (i, gb_blk + j)),
                  pl.BlockSpec((A_WIDTH, tn), lambda i, j: (0, j)),
                  pl.BlockSpec((B_VAL_WIDTH, tn), lambda i, j: (0, j)),
                  pl.BlockSpec((tn, D), lambda i, j: (j, 0))],
        out_specs=pl.BlockSpec((tm, D), row),
        out_shape=jax.ShapeDtypeStruct((S, D), F32),
        compiler_params=_params(("parallel", "arbitrary")),
        name="merge_out",
    )(x, oa, ob, pf, pf, wa, wb, wo)


def _layer(x, pos, p):
    x1, h = _ffn(x, p["ffn1_norm"], p["ffn1_wg"], p["ffn1_wu"], p["ffn1_wd"], p["mix_norm"])
    qk = _proj_qk(h, p["w_qk"], p["qk_gain"], pos, p["rope_inv"])
    av, bv = _proj_v(h, p["w_v"])
    pf = _proj(h, p["w_f"], F32)
    b = _gate(h, p["w_z"], p["w_2"], p["gate_bias"])
    o_a = _attn(qk, av)
    o_b = _gla(pf, bv, b, p["b_out_norm"])
    x2 = _merge_out(x1, o_a, o_b, pf, p["w_a_up"], p["w_b_up"], p["w_out"])
    return _ffn(x2, p["ffn2_norm"], p["ffn2_wg"], p["ffn2_wu"], p["ffn2_wd"])


def kernel(x, positions, ffn1_norm, ffn1_w_gate, ffn1_w_up, ffn1_w_down, mix_norm, w_in, a_q_norm, a_k_norm, b_gate_w2, b_gate_bias, b_out_norm, w_a_up, w_b_up, w_out, ffn2_norm, ffn2_w_gate, ffn2_w_up, ffn2_w_down):
    B, S, D = x.shape
    assert D == D_MODEL and S % A_CHUNK == 0
    depth = w_in.shape[0]
    c = np.cumsum([0, A_WIDTH, A_WIDTH, A_WIDTH, B_KEY_WIDTH, B_KEY_WIDTH, B_VAL_WIDTH, B_VAL_WIDTH,
                   GATE_RANK, D_MODEL, D_MODEL]).tolist()
    inv = jnp.power(ROPE_THETA, -(jnp.arange(ROPE_HALF, dtype=F32) * 2.0 / ROPE_DIM))
    rope_inv = jnp.concatenate([inv, inv, jnp.zeros((A_HEAD_DIM - ROPE_DIM,), F32)])[HEAD_PERM][None, :]
    qk_cols = (np.arange(2 * A_HEADS)[:, None] * A_HEAD_DIM + HEAD_PERM[None, :]).reshape(-1)
    outs = []
    for bi in range(B):
        xb = x.reshape(S, D) if B == 1 else x[bi]
        pos = positions.reshape(S, 1) if B == 1 else positions[bi][:, None]
        for l in range(depth):
            w = w_in[l]
            p = {
                "ffn1_norm": ffn1_norm[l][None, :], "mix_norm": mix_norm[l][None, :],
                "ffn2_norm": ffn2_norm[l][None, :],
                "ffn1_wg": ffn1_w_gate[l].astype(BF16), "ffn1_wu": ffn1_w_up[l].astype(BF16),
                "ffn1_wd": ffn1_w_down[l].astype(BF16),
                "ffn2_wg": ffn2_w_gate[l].astype(BF16), "ffn2_wu": ffn2_w_up[l].astype(BF16),
                "ffn2_wd": ffn2_w_down[l].astype(BF16),
                "w_qk": w[:, c[0]:c[2]][:, qk_cols].astype(BF16),
                "w_v": jnp.concatenate([w[:, c[2]:c[3]], w[:, c[5]:c[6]]], axis=1).astype(BF16),
                "w_f": jnp.concatenate([w[:, c[3]:c[5]], w[:, c[6]:c[7]], w[:, c[8]:c[10]]], axis=1).astype(BF16),
                "w_z": jnp.pad(w[:, c[7]:c[8]], ((0, 0), (0, LANES - GATE_RANK))).astype(BF16),
                "w_2": jnp.pad(b_gate_w2[l], ((0, LANES - GATE_RANK), (0, 0))).astype(BF16),
                "gate_bias": b_gate_bias[l][None, :],
                "qk_gain": jnp.concatenate([jnp.tile(a_q_norm[l][HEAD_PERM], A_HEADS),
                                            jnp.tile(a_k_norm[l][HEAD_PERM], A_HEADS)])[None, :],
                "rope_inv": rope_inv,
                "b_out_norm": b_out_norm[l][None, :],
                "w_a_up": w_a_up[l].astype(BF16), "w_b_up": w_b_up[l].astype(BF16), "w_out": w_out[l].astype(BF16),
            }
            xb = _layer(xb, pos, p)
        outs.append(xb)
    return outs[0].reshape(B, S, D) if B == 1 else jnp.stack(outs, axis=0)
```

```python
import functools

import jax
import jax.numpy as jnp
import numpy as np
from jax import lax
from jax.experimental import pallas as pl
from jax.experimental.pallas import tpu as pltpu

F32 = jnp.float32
BF16 = jnp.bfloat16

D_MODEL = 2048
D_FF = 5632
RMS_EPS = 1e-6
ROPE_THETA = 500000.0
A_HEAD_DIM = 128
A_HEADS = 8
A_WIDTH = A_HEADS * A_HEAD_DIM
ROPE_DIM = A_HEAD_DIM // 4
ROPE_HALF = ROPE_DIM // 2
DILATIONS = (1, 4, 16)
A_SPAN = 128
R4 = 4
B_HEADS = 4
B_VAL_DIM = 256
B_KEY_DIM = 128
B_KEY_WIDTH = B_HEADS * B_KEY_DIM
B_VAL_WIDTH = B_HEADS * B_VAL_DIM
GATE_RANK = 16
GATE_NORMALIZER = 16.0
GLA_CHUNK = 64
GLA_SUB = 8
LANES = 128
MXU_N = 256

VMEM_LIMIT = 56 * 1024 * 1024


def _params(sem):
    return pltpu.CompilerParams(dimension_semantics=sem, vmem_limit_bytes=VMEM_LIMIT)


def _rms(x, g):
    return x * lax.rsqrt(jnp.mean(x * x, axis=-1, keepdims=True) + RMS_EPS) * g


def _dot(a, b):
    return jnp.dot(a, b, preferred_element_type=F32)


def _dot_nt(a, b):
    return lax.dot_general(a, b, (((1,), (1,)), ((), ())), preferred_element_type=F32)


def _dot_tn(a, b):
    return lax.dot_general(a, b, (((0,), (0,)), ((), ())), preferred_element_type=F32)


def _ffn_kernel(x_ref, g_ref, wg_ref, wu_ref, wd_ref, *rest, emit_next):
    if emit_next:
        gn_ref, o_ref, hn_ref, h_scr = rest
    else:
        o_ref, h_scr = rest
    f = pl.program_id(1)

    @pl.when(f == 0)
    def _():
        x = x_ref[...]
        h_scr[...] = _rms(x, g_ref[...]).astype(BF16)
        o_ref[...] = x

    h = h_scr[...]
    gate = _dot(h, wg_ref[...])
    up = _dot(h, wu_ref[...])
    act = (0.5 * (gate * jax.nn.sigmoid(gate)) * up).astype(BF16)
    o_ref[...] += _dot(act, wd_ref[...])

    if emit_next:
        @pl.when(f == pl.num_programs(1) - 1)
        def _():
            hn_ref[...] = _rms(o_ref[...], gn_ref[...]).astype(BF16)


def _ffn(x, gain, wg, wu, wd, next_gain=None, *, tm=1024, tf=256):
    S, D = x.shape
    F = wg.shape[1]
    emit_next = next_gain is not None
    row = lambda i, f: (i, 0)
    fixed = lambda i, f: (0, 0)
    in_specs = [
        pl.BlockSpec((tm, D), row),
        pl.BlockSpec((1, D), fixed),
        pl.BlockSpec((D, tf), lambda i, f: (0, f)),
        pl.BlockSpec((D, tf), lambda i, f: (0, f)),
        pl.BlockSpec((tf, D), lambda i, f: (f, 0)),
    ]
    args = [x, gain, wg, wu, wd]
    out_shape = [jax.ShapeDtypeStruct((S, D), F32)]
    out_specs = [pl.BlockSpec((tm, D), row)]
    if emit_next:
        in_specs.append(pl.BlockSpec((1, D), fixed))
        args.append(next_gain)
        out_shape.append(jax.ShapeDtypeStruct((S, D), BF16))
        out_specs.append(pl.BlockSpec((tm, D), row))
    res = pl.pallas_call(
        functools.partial(_ffn_kernel, emit_next=emit_next),
        grid=(S // tm, F // tf),
        in_specs=in_specs,
        out_specs=out_specs,
        out_shape=out_shape,
        scratch_shapes=[pltpu.VMEM((tm, D), BF16)],
        compiler_params=_params(("parallel", "arbitrary")),
        name="ffn_next" if emit_next else "ffn",
    )(*args)
    return res if emit_next else res[0]


def _proj_kernel(a_ref, w_ref, o_ref):
    o_ref[...] = _dot(a_ref[...], w_ref[...]).astype(o_ref.dtype)


def _proj(a, w, out_dtype, *, tm=1024, tn=512):
    S, K = a.shape
    N = w.shape[1]
    return pl.pallas_call(
        _proj_kernel,
        grid=(S // tm, N // tn),
        in_specs=[pl.BlockSpec((tm, K), lambda i, j: (i, 0)),
                  pl.BlockSpec((K, tn), lambda i, j: (0, j))],
        out_specs=pl.BlockSpec((tm, tn), lambda i, j: (i, j)),
        out_shape=jax.ShapeDtypeStruct((S, N), out_dtype),
        compiler_params=_params(("parallel", "arbitrary")),
        name="proj_" + jnp.dtype(out_dtype).name,
    )(a, w)


ROPE_GAP = A_HEAD_DIM // 2
HEAD_PERM = np.concatenate([np.arange(0, ROPE_HALF), np.arange(ROPE_DIM, ROPE_DIM + ROPE_GAP - ROPE_HALF),
                            np.arange(ROPE_HALF, ROPE_DIM), np.arange(ROPE_DIM + ROPE_GAP - ROPE_HALF, A_HEAD_DIM)])


def _proj_qk_kernel(a_ref, w_ref, g_ref, pos_ref, inv_ref, o_ref, raw, rot):
    i = pl.program_id(0)
    nh = o_ref.shape[0]
    heads_per_dot = MXU_N // A_HEAD_DIM

    @pl.when(i == 0)
    def _():
        raw[1] = jnp.zeros(raw.shape[1:], raw.dtype)

    cur = i % 2
    a = a_ref[...]
    for g in range(nh // heads_per_dot):
        acc = _dot(a, w_ref[:, g * MXU_N:(g + 1) * MXU_N])
        for hh in range(heads_per_dot):
            raw[cur, g * heads_per_dot + hh] = acc[:, hh * A_HEAD_DIM:(hh + 1) * A_HEAD_DIM]

    ang = pos_ref[...].astype(F32) * inv_ref[...]
    lane = lax.broadcasted_iota(jnp.int32, ang.shape, 1)
    rot[0] = jnp.cos(ang)
    rot[1] = jnp.where(lane < ROPE_GAP, -jnp.sin(ang), jnp.sin(ang))
    prev = raw.at[1 - cur]
    tq = raw.shape[2] // R4
    gain = jnp.stack([g_ref[:, h * A_HEAD_DIM:(h + 1) * A_HEAD_DIM] for h in range(nh)], axis=0)
    for r in range(R4):
        rows = pl.ds(r, tq, stride=R4)
        y = _rms(jnp.stack([prev[h, rows, :] for h in range(nh)], axis=0), gain)
        o_ref[:, r] = (y * rot[0, rows, :][None] + pltpu.roll(y, ROPE_GAP, 2) * rot[1, rows, :][None]).astype(o_ref.dtype)


def _proj_qk(a, w, gains, pos, inv, *, tm=512):
    S, K = a.shape
    N = w.shape[1]
    nt = S // tm
    lag = lambda i: jnp.maximum(i - 1, 0)
    return pl.pallas_call(
        _proj_qk_kernel,
        grid=(nt + 1,),
        in_specs=[pl.BlockSpec((tm, K), lambda i: (jnp.minimum(i, nt - 1), 0)),
                  pl.BlockSpec((K, N), lambda i: (0, 0)),
                  pl.BlockSpec((1, N), lambda i: (0, 0)),
                  pl.BlockSpec((tm, 1), lambda i: (lag(i), 0)),
                  pl.BlockSpec((1, A_HEAD_DIM), lambda i: (0, 0))],
        out_specs=pl.BlockSpec((N // A_HEAD_DIM, R4, tm // R4, A_HEAD_DIM), lambda i: (0, 0, lag(i), 0)),
        out_shape=jax.ShapeDtypeStruct((N // A_HEAD_DIM, R4, S // R4, A_HEAD_DIM), BF16),
        scratch_shapes=[pltpu.VMEM((2, N // A_HEAD_DIM, tm, A_HEAD_DIM), F32),
                        pltpu.VMEM((2, tm, A_HEAD_DIM), F32)],
        compiler_params=_params(("arbitrary",)),
        name="proj_qk",
    )(a, w, gains, pos, inv)


def _proj_v_kernel(a_ref, w_ref, oa_ref, ob_ref, raw):
    a = a_ref[...]
    heads_per_dot = MXU_N // A_HEAD_DIM
    for g in range(A_WIDTH // MXU_N):
        acc = _dot(a, w_ref[:, g * MXU_N:(g + 1) * MXU_N])
        for hh in range(heads_per_dot):
            raw[g * heads_per_dot + hh] = acc[:, hh * A_HEAD_DIM:(hh + 1) * A_HEAD_DIM]
    for g in range(B_VAL_WIDTH // MXU_N):
        cols = slice(g * MXU_N, (g + 1) * MXU_N)
        ob_ref[:, cols] = _dot(a, w_ref[:, A_WIDTH + g * MXU_N:A_WIDTH + (g + 1) * MXU_N]).astype(ob_ref.dtype)
    tq = raw.shape[1] // R4
    for h in range(A_HEADS):
        for r in range(R4):
            oa_ref[h, r] = raw[h, pl.ds(r, tq, stride=R4), :].astype(oa_ref.dtype)


def _proj_v(a, w, *, tm=1024):
    S, K = a.shape
    return pl.pallas_call(
        _proj_v_kernel,
        grid=(S // tm,),
        in_specs=[pl.BlockSpec((tm, K), lambda i: (i, 0)),
                  pl.BlockSpec(w.shape, lambda i: (0, 0))],
        out_specs=[pl.BlockSpec((A_HEADS, R4, tm // R4, A_HEAD_DIM), lambda i: (0, 0, i, 0)),
                   pl.BlockSpec((tm, B_VAL_WIDTH), lambda i: (i, 0))],
        out_shape=[jax.ShapeDtypeStruct((A_HEADS, R4, S // R4, A_HEAD_DIM), BF16),
                   jax.ShapeDtypeStruct((S, B_VAL_WIDTH), BF16)],
        scratch_shapes=[pltpu.VMEM((A_HEADS, tm, A_HEAD_DIM), F32)],
        compiler_params=_params(("parallel",)),
        name="proj_v",
    )(a, w)


def _gate_kernel(a_ref, wz_ref, w2_ref, bias_ref, o_ref):
    z = _dot(a_ref[...], wz_ref[...])
    pre = _dot(z.astype(BF16), w2_ref[...]) + bias_ref[...]
    g = (jnp.minimum(pre, 0.0) - jnp.log1p(jnp.exp(-jnp.abs(pre)))) / GATE_NORMALIZER
    r = lax.broadcasted_iota(jnp.int32, g.shape, 0) % GLA_CHUNK
    shift = 1
    while shift < GLA_CHUNK:
        g = g + jnp.where(r >= shift, pltpu.roll(g, shift, 0), 0.0)
        shift *= 2
    o_ref[...] = g


def _gate(a, wz, w2, bias, *, tm=1024):
    S, K = a.shape
    return pl.pallas_call(
        _gate_kernel,
        grid=(S // tm,),
        in_specs=[pl.BlockSpec((tm, K), lambda i: (i, 0)),
                  pl.BlockSpec(wz.shape, lambda i: (0, 0)),
                  pl.BlockSpec(w2.shape, lambda i: (0, 0)),
                  pl.BlockSpec(bias.shape, lambda i: (0, 0))],
        out_specs=pl.BlockSpec((tm, B_KEY_WIDTH), lambda i: (i, 0)),
        out_shape=jax.ShapeDtypeStruct((S, B_KEY_WIDTH), F32),
        compiler_params=_params(("parallel",)),
        name="gate",
    )(a, wz, w2, bias)


A_BLK = 128
A_CHUNK = A_BLK * max(DILATIONS)
A_PLANE = A_CHUNK // R4


def _attn_kernel(q_ref, kp_ref, kc_ref, vp_ref, vc_ref, o_ref, qf, kf, vf, acc_s, m_s, l_s, o_nat, *, group):
    c = pl.program_id(1)
    first_chunk = (c == 0).astype(jnp.int32)
    scale = A_HEAD_DIM ** -0.5
    row = lax.broadcasted_iota(jnp.int32, (1, A_BLK, 2 * A_BLK), 1)
    col = lax.broadcasted_iota(jnp.int32, (1, A_BLK, 2 * A_BLK), 2)
    dist_strided = row + A_SPAN - col
    prev_strided = col < A_BLK
    qn, kn = A_BLK // R4, 2 * A_BLK // R4
    dist_mixed = (R4 * (row % qn) + row // qn) - (R4 * (col % kn) + col // kn) + A_SPAN
    prev_mixed = (col % kn) < kn // 2

    def rows_of(prev_ref, cur_ref, plane, lo, n_rows):
        if lo >= 0:
            return cur_ref[plane, lo:lo + n_rows, :]
        return jnp.concatenate([prev_ref[plane, A_PLANE + lo:A_PLANE, :], cur_ref[plane, 0:lo + n_rows, :]], axis=0)

    def run_group(mixed, tiles):
        dist = dist_mixed if mixed else dist_strided
        band = jnp.logical_and(dist >= 0, dist <= A_SPAN)
        s = jnp.concatenate([_dot_nt(t[0], t[1]) for t in tiles], axis=0).reshape(len(tiles), A_BLK, 2 * A_BLK)
        s = jnp.where(band, s * scale, -jnp.inf)
        firsts = [gi for gi, t in enumerate(tiles) if t[3]]
        if firsts:
            gidx = lax.broadcasted_iota(jnp.int32, (len(tiles), 1, 1), 0)
            is_first = functools.reduce(jnp.logical_or, [gidx == gi for gi in firsts]).astype(jnp.int32) * first_chunk
            s = jnp.where(jnp.logical_and(prev_mixed if mixed else prev_strided, is_first > 0), -jnp.inf, s)
        m = jnp.max(s, axis=-1, keepdims=True)
        p = jnp.exp(s - m)
        l = jnp.sum(p, axis=-1, keepdims=True)
        pb = p.astype(BF16)
        mb = jnp.broadcast_to(m, (len(tiles), A_BLK, A_HEAD_DIM))
        lb = jnp.broadcast_to(l, (len(tiles), A_BLK, A_HEAD_DIM))
        for gi, t in enumerate(tiles):
            t[4](_dot(pb[gi], t[2]), mb[gi], lb[gi])

    def grouped(mixed, tiles):
        for g0 in range(0, len(tiles), group):
            run_group(mixed, tiles[g0:g0 + group])

    tiles = []
    for n in range(A_CHUNK // A_BLK):
        q = jnp.concatenate([q_ref[r, qn * n:qn * (n + 1), :] for r in range(R4)], axis=0)
        k = jnp.concatenate([rows_of(kp_ref, kc_ref, r, qn * (n - 1), kn) for r in range(R4)], axis=0)
        v = jnp.concatenate([rows_of(vp_ref, vc_ref, r, qn * (n - 1), kn) for r in range(R4)], axis=0)

        def store(acc, mb, lb, n=n):
            for r in range(R4):
                dst = slice(qn * n, qn * (n + 1))
                src = slice(qn * r, qn * (r + 1))
                acc_s[0, r, dst, :] = acc[src]
                m_s[0, r, dst, :] = mb[src]
                l_s[0, r, dst, :] = lb[src]
        tiles.append((q, k, v, n == 0, store))
    grouped(True, tiles)

    tiles = []
    for r in range(R4):
        for n in range(A_PLANE // A_BLK):
            q = q_ref[r, A_BLK * n:A_BLK * (n + 1), :]
            k = rows_of(kp_ref, kc_ref, r, A_BLK * (n - 1), 2 * A_BLK)
            v = rows_of(vp_ref, vc_ref, r, A_BLK * (n - 1), 2 * A_BLK)

            def store(acc, mb, lb, r=r, n=n):
                dst = slice(A_BLK * n, A_BLK * (n + 1))
                acc_s[1, r, dst, :] = acc
                m_s[1, r, dst, :] = mb
                l_s[1, r, dst, :] = lb
            tiles.append((q, k, v, n == 0, store))
    grouped(False, tiles)

    qf[...] = q_ref[...].astype(F32)
    kf[:, 0:A_PLANE, :] = kp_ref[...].astype(F32)
    kf[:, A_PLANE:2 * A_PLANE, :] = kc_ref[...].astype(F32)
    vf[:, 0:A_PLANE, :] = vp_ref[...].astype(F32)
    vf[:, A_PLANE:2 * A_PLANE, :] = vc_ref[...].astype(F32)
    tiles = []
    for r in range(R4):
        for g in range(R4):
            own = pl.ds(g, A_BLK, stride=R4)
            both = pl.ds(g, 2 * A_BLK, stride=R4)

            def store(acc, mb, lb, r=r, own=own):
                acc_s[2, r, own, :] = acc
                m_s[2, r, own, :] = mb
                l_s[2, r, own, :] = lb
            tiles.append((qf[r, own, :].astype(BF16), kf[r, both, :].astype(BF16), vf[r, both, :].astype(BF16),
                          True, store))
    grouped(False, tiles)

    ms = [m_s[pi] for pi in range(len(DILATIONS))]
    m = functools.reduce(jnp.maximum, ms)
    ws = [jnp.exp(mi - m) for mi in ms]
    num = sum(w * acc_s[pi] for pi, w in enumerate(ws))
    den = sum(w * l_s[pi] for pi, w in enumerate(ws))
    o = num / den
    for r in range(R4):
        o_nat[pl.ds(r, A_PLANE, stride=R4), :] = o[r]
    o_ref[...] = o_nat[...].astype(o_ref.dtype)


def _attn(qk, v, *, group=8):
    S = v.shape[2] * R4
    npat = len(DILATIONS)
    blk = (None, R4, A_PLANE, A_HEAD_DIM)
    prev = lambda c: jnp.maximum(c - 1, 0)
    return pl.pallas_call(
        functools.partial(_attn_kernel, group=group),
        grid=(A_HEADS, S // A_CHUNK),
        in_specs=[pl.BlockSpec(blk, lambda h, c: (h, 0, c, 0)),
                  pl.BlockSpec(blk, lambda h, c: (A_HEADS + h, 0, prev(c), 0)),
                  pl.BlockSpec(blk, lambda h, c: (A_HEADS + h, 0, c, 0)),
                  pl.BlockSpec(blk, lambda h, c: (h, 0, prev(c), 0)),
                  pl.BlockSpec(blk, lambda h, c: (h, 0, c, 0))],
        out_specs=pl.BlockSpec((A_CHUNK, A_HEAD_DIM), lambda h, c: (c, h)),
        out_shape=jax.ShapeDtypeStruct((S, A_WIDTH), BF16),
        scratch_shapes=[pltpu.VMEM((R4, A_PLANE, A_HEAD_DIM), F32),
                        pltpu.VMEM((R4, 2 * A_PLANE, A_HEAD_DIM), F32),
                        pltpu.VMEM((R4, 2 * A_PLANE, A_HEAD_DIM), F32),
                        pltpu.VMEM((npat, R4, A_PLANE, A_HEAD_DIM), F32),
                        pltpu.VMEM((npat, R4, A_PLANE, A_HEAD_DIM), F32),
                        pltpu.VMEM((npat, R4, A_PLANE, A_HEAD_DIM), F32),
                        pltpu.VMEM((A_CHUNK, A_HEAD_DIM), F32)],
        compiler_params=_params(("parallel", "arbitrary")),
        name="attn",
    )(qk, qk, qk, v, v)


def _gla_kernel(q_ref, k_ref, v_ref, b_ref, r_ref, gn_ref, o_ref, st_ref, *, nchunk):
    C, Cs = GLA_CHUNK, GLA_SUB
    NS = C // Cs

    @pl.when(pl.program_id(0) == 0)
    def _():
        st_ref[...] = jnp.zeros_like(st_ref)

    row = lax.broadcasted_iota(jnp.int32, (C, C), 0)
    col = lax.broadcasted_iota(jnp.int32, (C, C), 1)
    sub_start = (row // Cs) * Cs
    sub_row = lax.broadcasted_iota(jnp.int32, (NS, Cs, B_KEY_DIM), 1)
    levels = []
    z = C // 2
    while z >= Cs:
        levels.append((z, jnp.logical_and((row // z) % 2 == 1, col // z == row // z - 1)))
        z //= 2
    diag_masks = [col == sub_start + j for j in range(Cs)]
    for c in range(nchunk):
        rows = slice(c * C, (c + 1) * C)
        for h in range(B_HEADS):
            kcols = slice(h * B_KEY_DIM, (h + 1) * B_KEY_DIM)
            vcols = slice(h * B_VAL_DIM, (h + 1) * B_VAL_DIM)
            q = q_ref[rows, kcols] * (B_KEY_DIM ** -0.5)
            k = k_ref[rows, kcols]
            b = b_ref[rows, kcols]
            v = v_ref[rows, vcols]
            b_last = b[C - 1:C, :]
            st = st_ref[h]
            o = _dot_nt((q * jnp.exp(b)).astype(BF16), st.astype(BF16))
            k_end = (k * jnp.exp(b_last - b)).astype(BF16)
            st_ref[h] = st * jnp.exp(b_last) + _dot_tn(v, k_end)

            a = jnp.zeros((C, C), F32)
            for z, mask in levels:
                bz = b.reshape(C // z, z, B_KEY_DIM)
                ends = bz[:, z - 1:z, :]
                starts = jnp.concatenate([jnp.zeros_like(ends[:1]), ends[:-1]], axis=0)
                q_z = (q * jnp.exp(bz - starts).reshape(C, B_KEY_DIM)).astype(BF16)
                k_z = (k * jnp.exp(ends - bz).reshape(C, B_KEY_DIM)).astype(BF16)
                a = jnp.where(mask, _dot_nt(q_z, k_z), a)

            q3 = q.reshape(NS, Cs, B_KEY_DIM)
            k3 = k.reshape(NS, Cs, B_KEY_DIM)
            b3 = b.reshape(NS, Cs, B_KEY_DIM)
            for j in range(Cs):
                kb = jnp.broadcast_to(k3[:, j:j + 1, :], q3.shape)
                bb = jnp.broadcast_to(b3[:, j:j + 1, :], q3.shape)
                w = jnp.exp(jnp.where(sub_row >= j, b3 - bb, -jnp.inf))
                dj = jnp.sum((q3 * kb * w).reshape(C, B_KEY_DIM), axis=-1, keepdims=True)
                a = jnp.where(diag_masks[j], dj, a)
            o = o + _dot(a.astype(BF16), v)
            r = r_ref[rows, vcols]
            o_ref[rows, vcols] = (_rms(o, gn_ref[...]) * (r * jax.nn.sigmoid(r))).astype(o_ref.dtype)


def _gla(pf, bv, b, gain, *, rb=256):
    S = pf.shape[0]
    return pl.pallas_call(
        functools.partial(_gla_kernel, nchunk=rb // GLA_CHUNK),
        grid=(S // rb,),
        in_specs=[pl.BlockSpec((rb, B_KEY_WIDTH), lambda i: (i, 0)),
                  pl.BlockSpec((rb, B_KEY_WIDTH), lambda i: (i, 1)),
                  pl.BlockSpec((rb, B_VAL_WIDTH), lambda i: (i, 0)),
                  pl.BlockSpec((rb, B_KEY_WIDTH), lambda i: (i, 0)),
                  pl.BlockSpec((rb, B_VAL_WIDTH), lambda i: (i, 1)),
                  pl.BlockSpec((1, B_VAL_DIM), lambda i: (0, 0))],
        out_specs=pl.BlockSpec((rb, B_VAL_WIDTH), lambda i: (i, 0)),
        out_shape=jax.ShapeDtypeStruct((S, B_VAL_WIDTH), BF16),
        scratch_shapes=[pltpu.VMEM((B_HEADS, B_VAL_DIM, B_KEY_DIM), F32)],
        compiler_params=_params(("arbitrary",)),
        name="gla",
    )(pf, pf, bv, b, pf, gain)


def _merge_out_kernel(x_ref, oa_ref, ob_ref, ga_ref, gb_ref, wa_ref, wb_ref, wo_ref, o_ref):
    @pl.when(pl.program_id(1) == 0)
    def _():
        o_ref[...] = x_ref[...]

    ya = _dot(oa_ref[...], wa_ref[...])
    yb = _dot(ob_ref[...], wb_ref[...])
    y = jax.nn.sigmoid(ga_ref[...]) * ya + jax.nn.sigmoid(gb_ref[...]) * yb
    o_ref[...] += _dot(y.astype(BF16), wo_ref[...])


def _merge_out(x, oa, ob, pf, wa, wb, wo, *, tm=512, tn=512):
    S, D = x.shape
    ga_blk = 2048 // tn
    gb_blk = 4096 // tn
    row = lambda i, j: (i, 0)
    return pl.pallas_call(
        _merge_out_kernel,
        grid=(S // tm, D // tn),
        in_specs=[pl.BlockSpec((tm, D), row),
                  pl.BlockSpec((tm, A_WIDTH), row),
                  pl.BlockSpec((tm, B_VAL_WIDTH), row),
                  pl.BlockSpec((tm, tn), lambda i, j: (i, ga_blk + j)),
                  pl.BlockSpec((tm, tn), lambda i, j: (i, gb_blk + j)),
                  pl.BlockSpec((A_WIDTH, tn), lambda i, j: (0, j)),
                  pl.BlockSpec((B_VAL_WIDTH, tn), lambda i, j: (0, j)),
                  pl.BlockSpec((tn, D), lambda i, j: (j, 0))],
        out_specs=pl.BlockSpec((tm, D), row),
        out_shape=jax.ShapeDtypeStruct((S, D), F32),
        compiler_params=_params(("parallel", "arbitrary")),
        name="merge_out",
    )(x, oa, ob, pf, pf, wa, wb, wo)


def _layer(x, pos, p):
    x1, h = _ffn(x, p["ffn1_norm"], p["ffn1_wg"], p["ffn1_wu"], p["ffn1_wd"], p["mix_norm"])
    qk = _proj_qk(h, p["w_qk"], p["qk_gain"], pos, p["rope_inv"])
    av, bv = _proj_v(h, p["w_v"])
    pf = _proj(h, p["w_f"], F32)
    b = _gate(h, p["w_z"], p["w_2"], p["gate_bias"])
    o_a = _attn(qk, av)
    o_b = _gla(pf, bv, b, p["b_out_norm"])
    x2 = _merge_out(x1, o_a, o_b, pf, p["w_a_up"], p["w_b_up"], p["w_out"])
    return _ffn(x2, p["ffn2_norm"], p["ffn2_wg"], p["ffn2_wu"], p["ffn2_wd"])


def kernel(x, positions, ffn1_norm, ffn1_w_gate, ffn1_w_up, ffn1_w_down, mix_norm, w_in, a_q_norm, a_k_norm, b_gate_w2, b_gate_bias, b_out_norm, w_a_up, w_b_up, w_out, ffn2_norm, ffn2_w_gate, ffn2_w_up, ffn2_w_down):
    B, S, D = x.shape
    assert D == D_MODEL and S % A_CHUNK == 0
    depth = w_in.shape[0]
    c = np.cumsum([0, A_WIDTH, A_WIDTH, A_WIDTH, B_KEY_WIDTH, B_KEY_WIDTH, B_VAL_WIDTH, B_VAL_WIDTH,
                   GATE_RANK, D_MODEL, D_MODEL]).tolist()
    inv = jnp.power(ROPE_THETA, -(jnp.arange(ROPE_HALF, dtype=F32) * 2.0 / ROPE_DIM))
    rope_inv = jnp.concatenate([inv, inv, jnp.zeros((A_HEAD_DIM - ROPE_DIM,), F32)])[HEAD_PERM][None, :]
    qk_cols = (np.arange(2 * A_HEADS)[:, None] * A_HEAD_DIM + HEAD_PERM[None, :]).reshape(-1)
    outs = []
    for bi in range(B):
        xb = x.reshape(S, D) if B == 1 else x[bi]
        pos = positions.reshape(S, 1) if B == 1 else positions[bi][:, None]
        for l in range(depth):
            w = w_in[l]
            p = {
                "ffn1_norm": ffn1_norm[l][None, :], "mix_norm": mix_norm[l][None, :],
                "ffn2_norm": ffn2_norm[l][None, :],
                "ffn1_wg": ffn1_w_gate[l].astype(BF16), "ffn1_wu": ffn1_w_up[l].astype(BF16),
                "ffn1_wd": ffn1_w_down[l].astype(BF16),
                "ffn2_wg": ffn2_w_gate[l].astype(BF16), "ffn2_wu": ffn2_w_up[l].astype(BF16),
                "ffn2_wd": ffn2_w_down[l].astype(BF16),
                "w_qk": w[:, c[0]:c[2]][:, qk_cols].astype(BF16),
                "w_v": jnp.concatenate([w[:, c[2]:c[3]], w[:, c[5]:c[6]]], axis=1).astype(BF16),
                "w_f": jnp.concatenate([w[:, c[3]:c[5]], w[:, c[6]:c[7]], w[:, c[8]:c[10]]], axis=1).astype(BF16),
                "w_z": jnp.pad(w[:, c[7]:c[8]], ((0, 0), (0, LANES - GATE_RANK))).astype(BF16),
                "w_2": jnp.pad(b_gate_w2[l], ((0, LANES - GATE_RANK), (0, 0))).astype(BF16),
                "gate_bias": b_gate_bias[l][None, :],
                "qk_gain": jnp.concatenate([jnp.tile(a_q_norm[l][HEAD_PERM], A_HEADS),
                                            jnp.tile(a_k_norm[l][HEAD_PERM], A_HEADS)])[None, :],
                "rope_inv": rope_inv,
                "b_out_norm": b_out_norm[l][None, :],
                "w_a_up": w_a_up[l].astype(BF16), "w_b_up": w_b_up[l].astype(BF16), "w_out": w_out[l].astype(BF16),
            }
            xb = _layer(xb, pos, p)
        outs.append(xb)
    return outs[0].reshape(B, S, D) if B == 1 else jnp.stack(outs, axis=0)
```

```python
import functools

import jax
import jax.numpy as jnp
import numpy as np
from jax import lax
from jax.experimental import pallas as pl
from jax.experimental.pallas import tpu as pltpu

F32 = jnp.float32
BF16 = jnp.bfloat16

D_MODEL = 2048
D_FF = 5632
RMS_EPS = 1e-6
ROPE_THETA = 500000.0
A_HEAD_DIM = 128
A_HEADS = 8
A_WIDTH = A_HEADS * A_HEAD_DIM
ROPE_DIM = A_HEAD_DIM // 4
ROPE_HALF = ROPE_DIM // 2
DILATIONS = (1, 4, 16)
A_SPAN = 128
R4 = 4
B_HEADS = 4
B_VAL_DIM = 256
B_KEY_DIM = 128
B_KEY_WIDTH = B_HEADS * B_KEY_DIM
B_VAL_WIDTH = B_HEADS * B_VAL_DIM
GATE_RANK = 16
GATE_NORMALIZER = 16.0
GLA_CHUNK = 64
GLA_SUB = 8
LANES = 128
MXU_N = 256

VMEM_LIMIT = 56 * 1024 * 1024


def _params(sem):
    return pltpu.CompilerParams(dimension_semantics=sem, vmem_limit_bytes=VMEM_LIMIT)


def _rms(x, g):
    return x * lax.rsqrt(jnp.mean(x * x, axis=-1, keepdims=True) + RMS_EPS) * g


def _dot(a, b):
    return jnp.dot(a, b, preferred_element_type=F32)


def _dot_nt(a, b):
    return lax.dot_general(a, b, (((1,), (1,)), ((), ())), preferred_element_type=F32)


def _dot_tn(a, b):
    return lax.dot_general(a, b, (((0,), (0,)), ((), ())), preferred_element_type=F32)


def _ffn_kernel(x_ref, g_ref, wg_ref, wu_ref, wd_ref, *rest, emit_next):
    if emit_next:
        gn_ref, o_ref, hn_ref, h_scr = rest
    else:
        o_ref, h_scr = rest
    f = pl.program_id(1)

    @pl.when(f == 0)
    def _():
        x = x_ref[...]
        h_scr[...] = _rms(x, g_ref[...]).astype(BF16)
        o_ref[...] = x

    h = h_scr[...]
    gate = _dot(h, wg_ref[...])
    up = _dot(h, wu_ref[...])
    act = (0.5 * (gate * jax.nn.sigmoid(gate)) * up).astype(BF16)
    o_ref[...] += _dot(act, wd_ref[...])

    if emit_next:
        @pl.when(f == pl.num_programs(1) - 1)
        def _():
            hn_ref[...] = _rms(o_ref[...], gn_ref[...]).astype(BF16)


def _ffn(x, gain, wg, wu, wd, next_gain=None, *, tm=1024, tf=256):
    S, D = x.shape
    F = wg.shape[1]
    emit_next = next_gain is not None
    row = lambda i, f: (i, 0)
    fixed = lambda i, f: (0, 0)
    in_specs = [
        pl.BlockSpec((tm, D), row),
        pl.BlockSpec((1, D), fixed),
        pl.BlockSpec((D, tf), lambda i, f: (0, f)),
        pl.BlockSpec((D, tf), lambda i, f: (0, f)),
        pl.BlockSpec((tf, D), lambda i, f: (f, 0)),
    ]
    args = [x, gain, wg, wu, wd]
    out_shape = [jax.ShapeDtypeStruct((S, D), F32)]
    out_specs = [pl.BlockSpec((tm, D), row)]
    if emit_next:
        in_specs.append(pl.BlockSpec((1, D), fixed))
        args.append(next_gain)
        out_shape.append(jax.ShapeDtypeStruct((S, D), BF16))
        out_specs.append(pl.BlockSpec((tm, D), row))
    res = pl.pallas_call(
        functools.partial(_ffn_kernel, emit_next=emit_next),
        grid=(S // tm, F // tf),
        in_specs=in_specs,
        out_specs=out_specs,
        out_shape=out_shape,
        scratch_shapes=[pltpu.VMEM((tm, D), BF16)],
        compiler_params=_params(("parallel", "arbitrary")),
        name="ffn_next" if emit_next else "ffn",
    )(*args)
    return res if emit_next else res[0]


def _proj_kernel(a_ref, w_ref, o_ref):
    o_ref[...] = _dot(a_ref[...], w_ref[...]).astype(o_ref.dtype)


def _proj(a, w, out_dtype, *, tm=1024, tn=512):
    S, K = a.shape
    N = w.shape[1]
    return pl.pallas_call(
        _proj_kernel,
        grid=(S // tm, N // tn),
        in_specs=[pl.BlockSpec((tm, K), lambda i, j: (i, 0)),
                  pl.BlockSpec((K, tn), lambda i, j: (0, j))],
        out_specs=pl.BlockSpec((tm, tn), lambda i, j: (i, j)),
        out_shape=jax.ShapeDtypeStruct((S, N), out_dtype),
        compiler_params=_params(("parallel", "arbitrary")),
        name="proj_" + jnp.dtype(out_dtype).name,
    )(a, w)


ROPE_GAP = A_HEAD_DIM // 2
HEAD_PERM = np.concatenate([np.arange(0, ROPE_HALF), np.arange(ROPE_DIM, ROPE_DIM + ROPE_GAP - ROPE_HALF),
                            np.arange(ROPE_HALF, ROPE_DIM), np.arange(ROPE_DIM + ROPE_GAP - ROPE_HALF, A_HEAD_DIM)])


def _proj_qk_kernel(a_ref, w_ref, g_ref, pos_ref, inv_ref, o_ref, raw, rot):
    i = pl.program_id(0)
    nh = o_ref.shape[0]
    heads_per_dot = MXU_N // A_HEAD_DIM

    @pl.when(i == 0)
    def _():
        raw[1] = jnp.zeros(raw.shape[1:], raw.dtype)

    cur = i % 2
    a = a_ref[...]
    for g in range(nh // heads_per_dot):
        acc = _dot(a, w_ref[:, g * MXU_N:(g + 1) * MXU_N])
        for hh in range(heads_per_dot):
            raw[cur, g * heads_per_dot + hh] = acc[:, hh * A_HEAD_DIM:(hh + 1) * A_HEAD_DIM]

    ang = pos_ref[...].astype(F32) * inv_ref[...]
    lane = lax.broadcasted_iota(jnp.int32, ang.shape, 1)
    rot[0] = jnp.cos(ang)
    rot[1] = jnp.where(lane < ROPE_GAP, -jnp.sin(ang), jnp.sin(ang))
    prev = raw.at[1 - cur]
    tq = raw.shape[2] // R4
    gain = jnp.stack([g_ref[:, h * A_HEAD_DIM:(h + 1) * A_HEAD_DIM] for h in range(nh)], axis=0)
    for r in range(R4):
        rows = pl.ds(r, tq, stride=R4)
        y = _rms(jnp.stack([prev[h, rows, :] for h in range(nh)], axis=0), gain)
        o_ref[:, r] = (y * rot[0, rows, :][None] + pltpu.roll(y, ROPE_GAP, 2) * rot[1, rows, :][None]).astype(o_ref.dtype)


def _proj_qk(a, w, gains, pos, inv, *, tm=512):
    S, K = a.shape
    N = w.shape[1]
    nt = S // tm
    lag = lambda i: jnp.maximum(i - 1, 0)
    return pl.pallas_call(
        _proj_qk_kernel,
        grid=(nt + 1,),
        in_specs=[pl.BlockSpec((tm, K), lambda i: (jnp.minimum(i, nt - 1), 0)),
                  pl.BlockSpec((K, N), lambda i: (0, 0)),
                  pl.BlockSpec((1, N), lambda i: (0, 0)),
                  pl.BlockSpec((tm, 1), lambda i: (lag(i), 0)),
                  pl.BlockSpec((1, A_HEAD_DIM), lambda i: (0, 0))],
        out_specs=pl.BlockSpec((N // A_HEAD_DIM, R4, tm // R4, A_HEAD_DIM), lambda i: (0, 0, lag(i), 0)),
        out_shape=jax.ShapeDtypeStruct((N // A_HEAD_DIM, R4, S // R4, A_HEAD_DIM), BF16),
        scratch_shapes=[pltpu.VMEM((2, N // A_HEAD_DIM, tm, A_HEAD_DIM), F32),
                        pltpu.VMEM((2, tm, A_HEAD_DIM), F32)],
        compiler_params=_params(("arbitrary",)),
        name="proj_qk",
    )(a, w, gains, pos, inv)


def _proj_v_kernel(a_ref, w_ref, oa_ref, ob_ref, raw):
    a = a_ref[...]
    heads_per_dot = MXU_N // A_HEAD_DIM
    for g in range(A_WIDTH // MXU_N):
        acc = _dot(a, w_ref[:, g * MXU_N:(g + 1) * MXU_N])
        for hh in range(heads_per_dot):
            raw[g * heads_per_dot + hh] = acc[:, hh * A_HEAD_DIM:(hh + 1) * A_HEAD_DIM]
    for g in range(B_VAL_WIDTH // MXU_N):
        cols = slice(g * MXU_N, (g + 1) * MXU_N)
        ob_ref[:, cols] = _dot(a, w_ref[:, A_WIDTH + g * MXU_N:A_WIDTH + (g + 1) * MXU_N]).astype(ob_ref.dtype)
    tq = raw.shape[1] // R4
    for h in range(A_HEADS):
        for r in range(R4):
            oa_ref[h, r] = raw[h, pl.ds(r, tq, stride=R4), :].astype(oa_ref.dtype)


def _proj_v(a, w, *, tm=1024):
    S, K = a.shape
    return pl.pallas_call(
        _proj_v_kernel,
        grid=(S // tm,),
        in_specs=[pl.BlockSpec((tm, K), lambda i: (i, 0)),
                  pl.BlockSpec(w.shape, lambda i: (0, 0))],
        out_specs=[pl.BlockSpec((A_HEADS, R4, tm // R4, A_HEAD_DIM), lambda i: (0, 0, i, 0)),
                   pl.BlockSpec((tm, B_VAL_WIDTH), lambda i: (i, 0))],
        out_shape=[jax.ShapeDtypeStruct((A_HEADS, R4, S // R4, A_HEAD_DIM), BF16),
                   jax.ShapeDtypeStruct((S, B_VAL_WIDTH), BF16)],
        scratch_shapes=[pltpu.VMEM((A_HEADS, tm, A_HEAD_DIM), F32)],
        compiler_params=_params(("parallel",)),
        name="proj_v",
    )(a, w)


def _gate_kernel(a_ref, wz_ref, w2_ref, bias_ref, o_ref):
    z = _dot(a_ref[...], wz_ref[...])
    pre = _dot(z.astype(BF16), w2_ref[...]) + bias_ref[...]
    g = (jnp.minimum(pre, 0.0) - jnp.log1p(jnp.exp(-jnp.abs(pre)))) / GATE_NORMALIZER
    r = lax.broadcasted_iota(jnp.int32, g.shape, 0) % GLA_CHUNK
    shift = 1
    while shift < GLA_CHUNK:
        g = g + jnp.where(r >= shift, pltpu.roll(g, shift, 0), 0.0)
        shift *= 2
    o_ref[...] = g


def _gate(a, wz, w2, bias, *, tm=1024):
    S, K = a.shape
    return pl.pallas_call(
        _gate_kernel,
        grid=(S // tm,),
        in_specs=[pl.BlockSpec((tm, K), lambda i: (i, 0)),
                  pl.BlockSpec(wz.shape, lambda i: (0, 0)),
                  pl.BlockSpec(w2.shape, lambda i: (0, 0)),
                  pl.BlockSpec(bias.shape, lambda i: (0, 0))],
        out_specs=pl.BlockSpec((tm, B_KEY_WIDTH), lambda i: (i, 0)),
        out_shape=jax.ShapeDtypeStruct((S, B_KEY_WIDTH), F32),
        compiler_params=_params(("parallel",)),
        name="gate",
    )(a, wz, w2, bias)


A_BLK = 128
A_CHUNK = A_BLK * max(DILATIONS)
A_PLANE = A_CHUNK // R4


def _attn_kernel(q_ref, kp_ref, kc_ref, vp_ref, vc_ref, o_ref, qf, kf, vf, acc_s, m_s, l_s, o_nat, *, group):
    c = pl.program_id(1)
    first_chunk = (c == 0).astype(jnp.int32)
    scale = A_HEAD_DIM ** -0.5
    row = lax.broadcasted_iota(jnp.int32, (1, A_BLK, 2 * A_BLK), 1)
    col = lax.broadcasted_iota(jnp.int32, (1, A_BLK, 2 * A_BLK), 2)
    dist_strided = row + A_SPAN - col
    prev_strided = col < A_BLK
    qn, kn = A_BLK // R4, 2 * A_BLK // R4
    dist_mixed = (R4 * (row % qn) + row // qn) - (R4 * (col % kn) + col // kn) + A_SPAN
    prev_mixed = (col % kn) < kn // 2

    def rows_of(prev_ref, cur_ref, plane, lo, n_rows):
        if lo >= 0:
            return cur_ref[plane, lo:lo + n_rows, :]
        return jnp.concatenate([prev_ref[plane, A_PLANE + lo:A_PLANE, :], cur_ref[plane, 0:lo + n_rows, :]], axis=0)

    def run_group(mixed, tiles):
        dist = dist_mixed if mixed else dist_strided
        band = jnp.logical_and(dist >= 0, dist <= A_SPAN)
        s = jnp.concatenate([_dot_nt(t[0], t[1]) for t in tiles], axis=0).reshape(len(tiles), A_BLK, 2 * A_BLK)
        s = jnp.where(band, s * scale, -jnp.inf)
        firsts = [gi for gi, t in enumerate(tiles) if t[3]]
        if firsts:
            gidx = lax.broadcasted_iota(jnp.int32, (len(tiles), 1, 1), 0)
            is_first = functools.reduce(jnp.logical_or, [gidx == gi for gi in firsts]).astype(jnp.int32) * first_chunk
            s = jnp.where(jnp.logical_and(prev_mixed if mixed else prev_strided, is_first > 0), -jnp.inf, s)
        m = jnp.max(s, axis=-1, keepdims=True)
        p = jnp.exp(s - m)
        l = jnp.sum(p, axis=-1, keepdims=True)
        pb = p.astype(BF16)
        mb = jnp.broadcast_to(m, (len(tiles), A_BLK, A_HEAD_DIM))
        lb = jnp.broadcast_to(l, (len(tiles), A_BLK, A_HEAD_DIM))
        for gi, t in enumerate(tiles):
            t[4](_dot(pb[gi], t[2]), mb[gi], lb[gi])

    def grouped(mixed, tiles):
        for g0 in range(0, len(tiles), group):
            run_group(mixed, tiles[g0:g0 + group])

    tiles = []
    for n in range(A_CHUNK // A_BLK):
        q = jnp.concatenate([q_ref[r, qn * n:qn * (n + 1), :] for r in range(R4)], axis=0)
        k = jnp.concatenate([rows_of(kp_ref, kc_ref, r, qn * (n - 1), kn) for r in range(R4)], axis=0)
        v = jnp.concatenate([rows_of(vp_ref, vc_ref, r, qn * (n - 1), kn) for r in range(R4)], axis=0)

        def store(acc, mb, lb, n=n):
            for r in range(R4):
                dst = slice(qn * n, qn * (n + 1))
                src = slice(qn * r, qn * (r + 1))
                acc_s[0, r, dst, :] = acc[src]
                m_s[0, r, dst, :] = mb[src]
                l_s[0, r, dst, :] = lb[src]
        tiles.append((q, k, v, n == 0, store))
    grouped(True, tiles)

    tiles = []
    for r in range(R4):
        for n in range(A_PLANE // A_BLK):
            q = q_ref[r, A_BLK * n:A_BLK * (n + 1), :]
            k = rows_of(kp_ref, kc_ref, r, A_BLK * (n - 1), 2 * A_BLK)
            v = rows_of(vp_ref, vc_ref, r, A_BLK * (n - 1), 2 * A_BLK)

            def store(acc, mb, lb, r=r, n=n):
                dst = slice(A_BLK * n, A_BLK * (n + 1))
                acc_s[1, r, dst, :] = acc
                m_s[1, r, dst, :] = mb
                l_s[1, r, dst, :] = lb
            tiles.append((q, k, v, n == 0, store))
    grouped(False, tiles)

    qf[...] = q_ref[...].astype(F32)
    kf[:, 0:A_PLANE, :] = kp_ref[...].astype(F32)
    kf[:, A_PLANE:2 * A_PLANE, :] = kc_ref[...].astype(F32)
    vf[:, 0:A_PLANE, :] = vp_ref[...].astype(F32)
    vf[:, A_PLANE:2 * A_PLANE, :] = vc_ref[...].astype(F32)
    tiles = []
    for r in range(R4):
        for g in range(R4):
            own = pl.ds(g, A_BLK, stride=R4)
            both = pl.ds(g, 2 * A_BLK, stride=R4)

            def store(acc, mb, lb, r=r, own=own):
                acc_s[2, r, own, :] = acc
                m_s[2, r, own, :] = mb
                l_s[2, r, own, :] = lb
            tiles.append((qf[r, own, :].astype(BF16), kf[r, both, :].astype(BF16), vf[r, both, :].astype(BF16),
                          True, store))
    grouped(False, tiles)

    ms = [m_s[pi] for pi in range(len(DILATIONS))]
    m = functools.reduce(jnp.maximum, ms)
    ws = [jnp.exp(mi - m) for mi in ms]
    num = sum(w * acc_s[pi] for pi, w in enumerate(ws))
    den = sum(w * l_s[pi] for pi, w in enumerate(ws))
    o = num / den
    for r in range(R4):
        o_nat[pl.ds(r, A_PLANE, stride=R4), :] = o[r]
    o_ref[...] = o_nat[...].astype(o_ref.dtype)


def _attn(qk, v, *, group=8):
    S = v.shape[2] * R4
    npat = len(DILATIONS)
    blk = (None, R4, A_PLANE, A_HEAD_DIM)
    prev = lambda c: jnp.maximum(c - 1, 0)
    return pl.pallas_call(
        functools.partial(_attn_kernel, group=group),
        grid=(A_HEADS, S // A_CHUNK),
        in_specs=[pl.BlockSpec(blk, lambda h, c: (h, 0, c, 0)),
                  pl.BlockSpec(blk, lambda h, c: (A_HEADS + h, 0, prev(c), 0)),
                  pl.BlockSpec(blk, lambda h, c: (A_HEADS + h, 0, c, 0)),
                  pl.BlockSpec(blk, lambda h, c: (h, 0, prev(c), 0)),
                  pl.BlockSpec(blk, lambda h, c: (h, 0, c, 0))],
        out_specs=pl.BlockSpec((A_CHUNK, A_HEAD_DIM), lambda h, c: (c, h)),
        out_shape=jax.ShapeDtypeStruct((S, A_WIDTH), BF16),
        scratch_shapes=[pltpu.VMEM((R4, A_PLANE, A_HEAD_DIM), F32),
                        pltpu.VMEM((R4, 2 * A_PLANE, A_HEAD_DIM), F32),
                        pltpu.VMEM((R4, 2 * A_PLANE, A_HEAD_DIM), F32),
                        pltpu.VMEM((npat, R4, A_PLANE, A_HEAD_DIM), F32),
                        pltpu.VMEM((npat, R4, A_PLANE, A_HEAD_DIM), F32),
                        pltpu.VMEM((npat, R4, A_PLANE, A_HEAD_DIM), F32),
                        pltpu.VMEM((A_CHUNK, A_HEAD_DIM), F32)],
        compiler_params=_params(("parallel", "arbitrary")),
        name="attn",
    )(qk, qk, qk, v, v)


def _gla_kernel(q_ref, k_ref, v_ref, b_ref, r_ref, gn_ref, o_ref, st_ref, *, nchunk):
    C, Cs = GLA_CHUNK, GLA_SUB
    NS = C // Cs

    @pl.when(pl.program_id(0) == 0)
    def _():
        st_ref[...] = jnp.zeros_like(st_ref)

    row = lax.broadcasted_iota(jnp.int32, (C, C), 0)
    col = lax.broadcasted_iota(jnp.int32, (C, C), 1)
    sub_start = (row // Cs) * Cs
    sub_row = lax.broadcasted_iota(jnp.int32, (NS, Cs, B_KEY_DIM), 1)
    levels = []
    z = C // 2
    while z >= Cs:
        levels.append((z, jnp.logical_and((row // z) % 2 == 1, col // z == row // z - 1)))
        z //= 2
    diag_masks = [col == sub_start + j for j in range(Cs)]
    for c in range(nchunk):
        rows = slice(c * C, (c + 1) * C)
        for h in range(B_HEADS):
            kcols = slice(h * B_KEY_DIM, (h + 1) * B_KEY_DIM)
            vcols = slice(h * B_VAL_DIM, (h + 1) * B_VAL_DIM)
            q = q_ref[rows, kcols] * (B_KEY_DIM ** -0.5)
            k = k_ref[rows, kcols]
            b = b_ref[rows, kcols]
            v = v_ref[rows, vcols]
            b_last = b[C - 1:C, :]
            st = st_ref[h]
            o = _dot_nt((q * jnp.exp(b)).astype(BF16), st.astype(BF16))
            k_end = (k * jnp.exp(b_last - b)).astype(BF16)
            st_ref[h] = st * jnp.exp(b_last) + _dot_tn(v, k_end)

            a = jnp.zeros((C, C), F32)
            for z, mask in levels:
                bz = b.reshape(C // z, z, B_KEY_DIM)
                ends = bz[:, z - 1:z, :]
                starts = jnp.concatenate([jnp.zeros_like(ends[:1]), ends[:-1]], axis=0)
                q_z = (q * jnp.exp(bz - starts).reshape(C, B_KEY_DIM)).astype(BF16)
                k_z = (k * jnp.exp(ends - bz).reshape(C, B_KEY_DIM)).astype(BF16)
                a = jnp.where(mask, _dot_nt(q_z, k_z), a)

            q3 = q.reshape(NS, Cs, B_KEY_DIM)
            k3 = k.reshape(NS, Cs, B_KEY_DIM)
            b3 = b.reshape(NS, Cs, B_KEY_DIM)
            for j in range(Cs):
                kb = jnp.broadcast_to(k3[:, j:j + 1, :], q3.shape)
                bb = jnp.broadcast_to(b3[:, j:j + 1, :], q3.shape)
                w = jnp.exp(jnp.where(sub_row >= j, b3 - bb, -jnp.inf))
                dj = jnp.sum((q3 * kb * w).reshape(C, B_KEY_DIM), axis=-1, keepdims=True)
                a = jnp.where(diag_masks[j], dj, a)
            o = o + _dot(a.astype(BF16), v)
            r = r_ref[rows, vcols]
            o_ref[rows, vcols] = (_rms(o, gn_ref[...]) * (r * jax.nn.sigmoid(r))).astype(o_ref.dtype)


def _gla(pf, bv, b, gain, *, rb=256):
    S = pf.shape[0]
    return pl.pallas_call(
        functools.partial(_gla_kernel, nchunk=rb // GLA_CHUNK),
        grid=(S // rb,),
        in_specs=[pl.BlockSpec((rb, B_KEY_WIDTH), lambda i: (i, 0)),
                  pl.BlockSpec((rb, B_KEY_WIDTH), lambda i: (i, 1)),
                  pl.BlockSpec((rb, B_VAL_WIDTH), lambda i: (i, 0)),
                  pl.BlockSpec((rb, B_KEY_WIDTH), lambda i: (i, 0)),
                  pl.BlockSpec((rb, B_VAL_WIDTH), lambda i: (i, 1)),
                  pl.BlockSpec((1, B_VAL_DIM), lambda i: (0, 0))],
        out_specs=pl.BlockSpec((rb, B_VAL_WIDTH), lambda i: (i, 0)),
        out_shape=jax.ShapeDtypeStruct((S, B_VAL_WIDTH), BF16),
        scratch_shapes=[pltpu.VMEM((B_HEADS, B_VAL_DIM, B_KEY_DIM), F32)],
        compiler_params=_params(("arbitrary",)),
        name="gla",
    )(pf, pf, bv, b, pf, gain)


def _merge_out_kernel(x_ref, oa_ref, ob_ref, ga_ref, gb_ref, wa_ref, wb_ref, wo_ref, o_ref):
    j = pl.program_id(1)
    tn = ga_ref.shape[1]
    cols = pl.ds(pl.multiple_of(j * tn, tn), tn)

    @pl.when(j == 0)
    def _():
        o_ref[...] = x_ref[...]

    ya = _dot(oa_ref[...], wa_ref[:, cols])
    yb = _dot(ob_ref[...], wb_ref[:, cols])
    y = jax.nn.sigmoid(ga_ref[...]) * ya + jax.nn.sigmoid(gb_ref[...]) * yb
    o_ref[...] += _dot(y.astype(BF16), wo_ref[cols, :])


def _merge_out(x, oa, ob, pf, wa, wb, wo, *, tm=512, tn=512):
    S, D = x.shape
    ga_blk = 2048 // tn
    gb_blk = 4096 // tn
    row = lambda i, j: (i, 0)
    resident = lambda w: pl.BlockSpec(w.shape, lambda i, j: (0, 0), pipeline_mode=pl.Buffered(1))
    return pl.pallas_call(
        _merge_out_kernel,
        grid=(S // tm, D // tn),
        in_specs=[pl.BlockSpec((tm, D), row),
                  pl.BlockSpec((tm, A_WIDTH), row),
                  pl.BlockSpec((tm, B_VAL_WIDTH), row),
                  pl.BlockSpec((tm, tn), lambda i, j: (i, ga_blk + j)),
                  pl.BlockSpec((tm, tn), lambda i, j: (i, gb_blk + j)),
                  resident(wa), resident(wb), resident(wo)],
        out_specs=pl.BlockSpec((tm, D), row),
        out_shape=jax.ShapeDtypeStruct((S, D), F32),
        compiler_params=_params(("parallel", "arbitrary")),
        name="merge_out",
    )(x, oa, ob, pf, pf, wa, wb, wo)


IN_COLS = np.cumsum([0, A_WIDTH, A_WIDTH, A_WIDTH, B_KEY_WIDTH, B_KEY_WIDTH, B_VAL_WIDTH, B_VAL_WIDTH,
                     GATE_RANK, D_MODEL, D_MODEL]).tolist()


def _w_in_kernel(w_ref, tail_ref, qk_ref, v_ref, f_ref, z_ref):
    c = IN_COLS
    lane = lax.broadcasted_iota(jnp.int32, (w_ref.shape[0], LANES), 1)

    def tile(col):
        return w_ref[:, col:col + LANES]

    for h in range(2 * A_HEADS):
        t = tile(h * A_HEAD_DIM)
        up = pltpu.roll(t, A_HEAD_DIM - ROPE_HALF, 1)
        down = pltpu.roll(t, ROPE_GAP - ROPE_HALF, 1)
        t = jnp.where(lane < ROPE_HALF, t, jnp.where(lane < ROPE_GAP, up, jnp.where(lane < ROPE_GAP + ROPE_HALF, down, t)))
        qk_ref[:, h * A_HEAD_DIM:(h + 1) * A_HEAD_DIM] = t.astype(qk_ref.dtype)
    v_ref[:, 0:A_WIDTH] = w_ref[:, c[2]:c[3]].astype(v_ref.dtype)
    v_ref[:, A_WIDTH:] = w_ref[:, c[5]:c[6]].astype(v_ref.dtype)
    f_ref[:, 0:2 * B_KEY_WIDTH] = w_ref[:, c[3]:c[5]].astype(f_ref.dtype)
    f_ref[:, 2 * B_KEY_WIDTH:2 * B_KEY_WIDTH + B_VAL_WIDTH] = w_ref[:, c[6]:c[7]].astype(f_ref.dtype)
    z_ref[...] = jnp.where(lane < GATE_RANK, tile(c[7]), 0.0).astype(z_ref.dtype)
    base = 2 * B_KEY_WIDTH + B_VAL_WIDTH
    ntile = 2 * D_MODEL // LANES
    cur = pltpu.roll(tile(c[7]), LANES - GATE_RANK, 1)
    for n in range(ntile):
        nxt_src = tile(c[7] + (n + 1) * LANES) if n + 1 < ntile else tail_ref[...]
        nxt = pltpu.roll(nxt_src, LANES - GATE_RANK, 1)
        f_ref[:, base + n * LANES:base + (n + 1) * LANES] = jnp.where(lane < LANES - GATE_RANK, cur, nxt).astype(f_ref.dtype)
        cur = nxt


def _w_in_layout(w, *, tk=256):
    K, N = w.shape
    tail_cols = N - (N // LANES) * LANES
    tail = jnp.pad(w[:, N - tail_cols:], ((0, 0), (0, LANES - tail_cols)))
    widths = (2 * A_WIDTH, A_WIDTH + B_VAL_WIDTH, 2 * B_KEY_WIDTH + B_VAL_WIDTH + 2 * D_MODEL, LANES)
    return pl.pallas_call(
        _w_in_kernel,
        grid=(K // tk,),
        in_specs=[pl.BlockSpec((tk, N), lambda i: (i, 0)),
                  pl.BlockSpec((tk, LANES), lambda i: (i, 0))],
        out_specs=[pl.BlockSpec((tk, n), lambda i: (i, 0)) for n in widths],
        out_shape=[jax.ShapeDtypeStruct((K, n), BF16) for n in widths],
        compiler_params=_params(("parallel",)),
        name="w_in_layout",
    )(w, tail)


def _layer(x, pos, p):
    x1, h = _ffn(x, p["ffn1_norm"], p["ffn1_wg"], p["ffn1_wu"], p["ffn1_wd"], p["mix_norm"])
    qk = _proj_qk(h, p["w_qk"], p["qk_gain"], pos, p["rope_inv"])
    av, bv = _proj_v(h, p["w_v"])
    pf = _proj(h, p["w_f"], F32)
    b = _gate(h, p["w_z"], p["w_2"], p["gate_bias"])
    o_a = _attn(qk, av)
    o_b = _gla(pf, bv, b, p["b_out_norm"])
    x2 = _merge_out(x1, o_a, o_b, pf, p["w_a_up"], p["w_b_up"], p["w_out"])
    return _ffn(x2, p["ffn2_norm"], p["ffn2_wg"], p["ffn2_wu"], p["ffn2_wd"])


def kernel(x, positions, ffn1_norm, ffn1_w_gate, ffn1_w_up, ffn1_w_down, mix_norm, w_in, a_q_norm, a_k_norm, b_gate_w2, b_gate_bias, b_out_norm, w_a_up, w_b_up, w_out, ffn2_norm, ffn2_w_gate, ffn2_w_up, ffn2_w_down):
    B, S, D = x.shape
    assert D == D_MODEL and S % A_CHUNK == 0
    depth = w_in.shape[0]
    inv = jnp.power(ROPE_THETA, -(jnp.arange(ROPE_HALF, dtype=F32) * 2.0 / ROPE_DIM))
    rope_inv = jnp.concatenate([inv, inv, jnp.zeros((A_HEAD_DIM - ROPE_DIM,), F32)])[HEAD_PERM][None, :]
    outs = []
    for bi in range(B):
        xb = x.reshape(S, D) if B == 1 else x[bi]
        pos = positions.reshape(S, 1) if B == 1 else positions[bi][:, None]
        for l in range(depth):
            w_qk, w_v, w_f, w_z = _w_in_layout(w_in[l])
            p = {
                "ffn1_norm": ffn1_norm[l][None, :], "mix_norm": mix_norm[l][None, :],
                "ffn2_norm": ffn2_norm[l][None, :],
                "ffn1_wg": ffn1_w_gate[l].astype(BF16), "ffn1_wu": ffn1_w_up[l].astype(BF16),
                "ffn1_wd": ffn1_w_down[l].astype(BF16),
                "ffn2_wg": ffn2_w_gate[l].astype(BF16), "ffn2_wu": ffn2_w_up[l].astype(BF16),
                "ffn2_wd": ffn2_w_down[l].astype(BF16),
                "w_qk": w_qk, "w_v": w_v, "w_f": w_f, "w_z": w_z,
                "w_2": jnp.pad(b_gate_w2[l], ((0, LANES - GATE_RANK), (0, 0))).astype(BF16),
                "gate_bias": b_gate_bias[l][None, :],
                "qk_gain": jnp.concatenate([jnp.tile(a_q_norm[l][HEAD_PERM], A_HEADS),
                                            jnp.tile(a_k_norm[l][HEAD_PERM], A_HEADS)])[None, :],
                "rope_inv": rope_inv,
                "b_out_norm": b_out_norm[l][None, :],
                "w_a_up": w_a_up[l].astype(BF16), "w_b_up": w_b_up[l].astype(BF16), "w_out": w_out[l].astype(BF16),
            }
            xb = _layer(xb, pos, p)
        outs.append(xb)
    return outs[0].reshape(B, S, D) if B == 1 else jnp.stack(outs, axis=0)
```

```python
import functools

import jax
import jax.numpy as jnp
import numpy as np
from jax import lax
from jax.experimental import pallas as pl
from jax.experimental.pallas import tpu as pltpu

F32 = jnp.float32
BF16 = jnp.bfloat16

D_MODEL = 2048
D_FF = 5632
RMS_EPS = 1e-6
ROPE_THETA = 500000.0
A_HEAD_DIM = 128
A_HEADS = 8
A_WIDTH = A_HEADS * A_HEAD_DIM
ROPE_DIM = A_HEAD_DIM // 4
ROPE_HALF = ROPE_DIM // 2
DILATIONS = (1, 4, 16)
A_SPAN = 128
R4 = 4
B_HEADS = 4
B_VAL_DIM = 256
B_KEY_DIM = 128
B_KEY_WIDTH = B_HEADS * B_KEY_DIM
B_VAL_WIDTH = B_HEADS * B_VAL_DIM
GATE_RANK = 16
GATE_NORMALIZER = 16.0
GLA_CHUNK = 64
GLA_SUB = 8
LANES = 128
MXU_N = 256

VMEM_LIMIT = 56 * 1024 * 1024


def _params(sem):
    return pltpu.CompilerParams(dimension_semantics=sem, vmem_limit_bytes=VMEM_LIMIT)


def _rms(x, g):
    return x * lax.rsqrt(jnp.mean(x * x, axis=-1, keepdims=True) + RMS_EPS) * g


def _dot(a, b):
    return jnp.dot(a, b, preferred_element_type=F32)


def _dot_nt(a, b):
    return lax.dot_general(a, b, (((1,), (1,)), ((), ())), preferred_element_type=F32)


def _dot_tn(a, b):
    return lax.dot_general(a, b, (((0,), (0,)), ((), ())), preferred_element_type=F32)


def _ffn_kernel(x_ref, g_ref, wg_ref, wu_ref, wd_ref, *rest, emit_next):
    if emit_next:
        gn_ref, o_ref, hn_ref, h_scr = rest
    else:
        o_ref, h_scr = rest
    f = pl.program_id(1)

    @pl.when(f == 0)
    def _():
        x = x_ref[...]
        h_scr[...] = _rms(x, g_ref[...]).astype(BF16)
        o_ref[...] = x

    h = h_scr[...]
    gate = _dot(h, wg_ref[...])
    up = _dot(h, wu_ref[...])
    act = (0.5 * (gate * jax.nn.sigmoid(gate)) * up).astype(BF16)
    o_ref[...] += _dot(act, wd_ref[...])

    if emit_next:
        @pl.when(f == pl.num_programs(1) - 1)
        def _():
            hn_ref[...] = _rms(o_ref[...], gn_ref[...]).astype(BF16)


def _ffn(x, gain, wg, wu, wd, next_gain=None, *, tm=1024, tf=256):
    S, D = x.shape
    F = wg.shape[1]
    emit_next = next_gain is not None
    row = lambda i, f: (i, 0)
    fixed = lambda i, f: (0, 0)
    in_specs = [
        pl.BlockSpec((tm, D), row),
        pl.BlockSpec((1, D), fixed),
        pl.BlockSpec((D, tf), lambda i, f: (0, f)),
        pl.BlockSpec((D, tf), lambda i, f: (0, f)),
        pl.BlockSpec((tf, D), lambda i, f: (f, 0)),
    ]
    args = [x, gain, wg, wu, wd]
    out_shape = [jax.ShapeDtypeStruct((S, D), F32)]
    out_specs = [pl.BlockSpec((tm, D), row)]
    if emit_next:
        in_specs.append(pl.BlockSpec((1, D), fixed))
        args.append(next_gain)
        out_shape.append(jax.ShapeDtypeStruct((S, D), BF16))
        out_specs.append(pl.BlockSpec((tm, D), row))
    res = pl.pallas_call(
        functools.partial(_ffn_kernel, emit_next=emit_next),
        grid=(S // tm, F // tf),
        in_specs=in_specs,
        out_specs=out_specs,
        out_shape=out_shape,
        scratch_shapes=[pltpu.VMEM((tm, D), BF16)],
        compiler_params=_params(("parallel", "arbitrary")),
        name="ffn_next" if emit_next else "ffn",
    )(*args)
    return res if emit_next else res[0]


def _proj_kernel(a_ref, w_ref, o_ref):
    o_ref[...] = _dot(a_ref[...], w_ref[...]).astype(o_ref.dtype)


def _proj(a, w, out_dtype, *, tm=1024, tn=512):
    S, K = a.shape
    N = w.shape[1]
    return pl.pallas_call(
        _proj_kernel,
        grid=(S // tm, N // tn),
        in_specs=[pl.BlockSpec((tm, K), lambda i, j: (i, 0)),
                  pl.BlockSpec((K, tn), lambda i, j: (0, j))],
        out_specs=pl.BlockSpec((tm, tn), lambda i, j: (i, j)),
        out_shape=jax.ShapeDtypeStruct((S, N), out_dtype),
        compiler_params=_params(("parallel", "arbitrary")),
        name="proj_" + jnp.dtype(out_dtype).name,
    )(a, w)


ROPE_GAP = A_HEAD_DIM // 2
HEAD_PERM = np.concatenate([np.arange(0, ROPE_HALF), np.arange(ROPE_DIM, ROPE_DIM + ROPE_GAP - ROPE_HALF),
                            np.arange(ROPE_HALF, ROPE_DIM), np.arange(ROPE_DIM + ROPE_GAP - ROPE_HALF, A_HEAD_DIM)])


def _proj_qk_kernel(a_ref, w_ref, g_ref, pos_ref, inv_ref, o_ref, raw, rot):
    i = pl.program_id(0)
    nh = o_ref.shape[0]
    heads_per_dot = MXU_N // A_HEAD_DIM

    @pl.when(i == 0)
    def _():
        raw[1] = jnp.zeros(raw.shape[1:], raw.dtype)

    cur = i % 2
    a = a_ref[...]
    for g in range(nh // heads_per_dot):
        acc = _dot(a, w_ref[:, g * MXU_N:(g + 1) * MXU_N])
        for hh in range(heads_per_dot):
            raw[cur, g * heads_per_dot + hh] = acc[:, hh * A_HEAD_DIM:(hh + 1) * A_HEAD_DIM]

    ang = pos_ref[...].astype(F32) * inv_ref[...]
    lane = lax.broadcasted_iota(jnp.int32, ang.shape, 1)
    rot[0] = jnp.cos(ang)
    rot[1] = jnp.where(lane < ROPE_GAP, -jnp.sin(ang), jnp.sin(ang))
    prev = raw.at[1 - cur]
    tq = raw.shape[2] // R4
    gain = jnp.stack([g_ref[:, h * A_HEAD_DIM:(h + 1) * A_HEAD_DIM] for h in range(nh)], axis=0)
    for r in range(R4):
        rows = pl.ds(r, tq, stride=R4)
        y = _rms(jnp.stack([prev[h, rows, :] for h in range(nh)], axis=0), gain)
        o_ref[:, r] = (y * rot[0, rows, :][None] + pltpu.roll(y, ROPE_GAP, 2) * rot[1, rows, :][None]).astype(o_ref.dtype)


def _proj_qk(a, w, gains, pos, inv, *, tm=512):
    S, K = a.shape
    N = w.shape[1]
    nt = S // tm
    lag = lambda i: jnp.maximum(i - 1, 0)
    return pl.pallas_call(
        _proj_qk_kernel,
        grid=(nt + 1,),
        in_specs=[pl.BlockSpec((tm, K), lambda i: (jnp.minimum(i, nt - 1), 0)),
                  pl.BlockSpec((K, N), lambda i: (0, 0)),
                  pl.BlockSpec((1, N), lambda i: (0, 0)),
                  pl.BlockSpec((tm, 1), lambda i: (lag(i), 0)),
                  pl.BlockSpec((1, A_HEAD_DIM), lambda i: (0, 0))],
        out_specs=pl.BlockSpec((N // A_HEAD_DIM, R4, tm // R4, A_HEAD_DIM), lambda i: (0, 0, lag(i), 0)),
        out_shape=jax.ShapeDtypeStruct((N // A_HEAD_DIM, R4, S // R4, A_HEAD_DIM), BF16),
        scratch_shapes=[pltpu.VMEM((2, N // A_HEAD_DIM, tm, A_HEAD_DIM), F32),
                        pltpu.VMEM((2, tm, A_HEAD_DIM), F32)],
        compiler_params=_params(("arbitrary",)),
        name="proj_qk",
    )(a, w, gains, pos, inv)


def _proj_v_kernel(a_ref, w_ref, oa_ref, ob_ref, raw):
    a = a_ref[...]
    heads_per_dot = MXU_N // A_HEAD_DIM
    for g in range(A_WIDTH // MXU_N):
        acc = _dot(a, w_ref[:, g * MXU_N:(g + 1) * MXU_N])
        for hh in range(heads_per_dot):
            raw[g * heads_per_dot + hh] = acc[:, hh * A_HEAD_DIM:(hh + 1) * A_HEAD_DIM]
    for g in range(B_VAL_WIDTH // MXU_N):
        cols = slice(g * MXU_N, (g + 1) * MXU_N)
        ob_ref[:, cols] = _dot(a, w_ref[:, A_WIDTH + g * MXU_N:A_WIDTH + (g + 1) * MXU_N]).astype(ob_ref.dtype)
    tq = raw.shape[1] // R4
    for h in range(A_HEADS):
        for r in range(R4):
            oa_ref[h, r] = raw[h, pl.ds(r, tq, stride=R4), :].astype(oa_ref.dtype)


def _proj_v(a, w, *, tm=1024):
    S, K = a.shape
    return pl.pallas_call(
        _proj_v_kernel,
        grid=(S // tm,),
        in_specs=[pl.BlockSpec((tm, K), lambda i: (i, 0)),
                  pl.BlockSpec(w.shape, lambda i: (0, 0))],
        out_specs=[pl.BlockSpec((A_HEADS, R4, tm // R4, A_HEAD_DIM), lambda i: (0, 0, i, 0)),
                   pl.BlockSpec((tm, B_VAL_WIDTH), lambda i: (i, 0))],
        out_shape=[jax.ShapeDtypeStruct((A_HEADS, R4, S // R4, A_HEAD_DIM), BF16),
                   jax.ShapeDtypeStruct((S, B_VAL_WIDTH), BF16)],
        scratch_shapes=[pltpu.VMEM((A_HEADS, tm, A_HEAD_DIM), F32)],
        compiler_params=_params(("parallel",)),
        name="proj_v",
    )(a, w)


def _gate_kernel(a_ref, wz_ref, w2_ref, bias_ref, o_ref):
    z = _dot(a_ref[...], wz_ref[...])
    pre = _dot(z.astype(BF16), w2_ref[...]) + bias_ref[...]
    g = (jnp.minimum(pre, 0.0) - jnp.log1p(jnp.exp(-jnp.abs(pre)))) / GATE_NORMALIZER
    r = lax.broadcasted_iota(jnp.int32, g.shape, 0) % GLA_CHUNK
    shift = 1
    while shift < GLA_CHUNK:
        g = g + jnp.where(r >= shift, pltpu.roll(g, shift, 0), 0.0)
        shift *= 2
    o_ref[...] = g


def _gate(a, wz, w2, bias, *, tm=1024):
    S, K = a.shape
    return pl.pallas_call(
        _gate_kernel,
        grid=(S // tm,),
        in_specs=[pl.BlockSpec((tm, K), lambda i: (i, 0)),
                  pl.BlockSpec(wz.shape, lambda i: (0, 0)),
                  pl.BlockSpec(w2.shape, lambda i: (0, 0)),
                  pl.BlockSpec(bias.shape, lambda i: (0, 0))],
        out_specs=pl.BlockSpec((tm, B_KEY_WIDTH), lambda i: (i, 0)),
        out_shape=jax.ShapeDtypeStruct((S, B_KEY_WIDTH), F32),
        compiler_params=_params(("parallel",)),
        name="gate",
    )(a, wz, w2, bias)


A_BLK = 128
A_CHUNK = A_BLK * max(DILATIONS)
A_PLANE = A_CHUNK // R4


def _attn_kernel(q_ref, kp_ref, kc_ref, vp_ref, vc_ref, o_ref, qf, kf, vf, acc_s, m_s, l_s, o_nat, *, group):
    c = pl.program_id(1)
    first_chunk = (c == 0).astype(jnp.int32)
    scale = A_HEAD_DIM ** -0.5
    row = lax.broadcasted_iota(jnp.int32, (1, A_BLK, 2 * A_BLK), 1)
    col = lax.broadcasted_iota(jnp.int32, (1, A_BLK, 2 * A_BLK), 2)
    dist_strided = row + A_SPAN - col
    prev_strided = col < A_BLK
    qn, kn = A_BLK // R4, 2 * A_BLK // R4
    dist_mixed = (R4 * (row % qn) + row // qn) - (R4 * (col % kn) + col // kn) + A_SPAN
    prev_mixed = (col % kn) < kn // 2

    def rows_of(prev_ref, cur_ref, plane, lo, n_rows):
        if lo >= 0:
            return cur_ref[plane, lo:lo + n_rows, :]
        return jnp.concatenate([prev_ref[plane, A_PLANE + lo:A_PLANE, :], cur_ref[plane, 0:lo + n_rows, :]], axis=0)

    def run_group(mixed, tiles):
        dist = dist_mixed if mixed else dist_strided
        band = jnp.logical_and(dist >= 0, dist <= A_SPAN)
        s = jnp.concatenate([_dot_nt(t[0], t[1]) for t in tiles], axis=0).reshape(len(tiles), A_BLK, 2 * A_BLK)
        s = jnp.where(band, s * scale, -jnp.inf)
        firsts = [gi for gi, t in enumerate(tiles) if t[3]]
        if firsts:
            gidx = lax.broadcasted_iota(jnp.int32, (len(tiles), 1, 1), 0)
            is_first = functools.reduce(jnp.logical_or, [gidx == gi for gi in firsts]).astype(jnp.int32) * first_chunk
            s = jnp.where(jnp.logical_and(prev_mixed if mixed else prev_strided, is_first > 0), -jnp.inf, s)
        m = jnp.max(s, axis=-1, keepdims=True)
        p = jnp.exp(s - m)
        l = jnp.sum(p, axis=-1, keepdims=True)
        pb = p.astype(BF16)
        mb = jnp.broadcast_to(m, (len(tiles), A_BLK, A_HEAD_DIM))
        lb = jnp.broadcast_to(l, (len(tiles), A_BLK, A_HEAD_DIM))
        for gi, t in enumerate(tiles):
            t[4](_dot(pb[gi], t[2]), mb[gi], lb[gi])

    def grouped(mixed, tiles):
        for g0 in range(0, len(tiles), group):
            run_group(mixed, tiles[g0:g0 + group])

    tiles = []
    for n in range(A_CHUNK // A_BLK):
        q = jnp.concatenate([q_ref[r, qn * n:qn * (n + 1), :] for r in range(R4)], axis=0)
        k = jnp.concatenate([rows_of(kp_ref, kc_ref, r, qn * (n - 1), kn) for r in range(R4)], axis=0)
        v = jnp.concatenate([rows_of(vp_ref, vc_ref, r, qn * (n - 1), kn) for r in range(R4)], axis=0)

        def store(acc, mb, lb, n=n):
            for r in range(R4):
                dst = slice(qn * n, qn * (n + 1))
                src = slice(qn * r, qn * (r + 1))
                acc_s[0, r, dst, :] = acc[src]
                m_s[0, r, dst, :] = mb[src]
                l_s[0, r, dst, :] = lb[src]
        tiles.append((q, k, v, n == 0, store))
    grouped(True, tiles)

    tiles = []
    for r in range(R4):
        for n in range(A_PLANE // A_BLK):
            q = q_ref[r, A_BLK * n:A_BLK * (n + 1), :]
            k = rows_of(kp_ref, kc_ref, r, A_BLK * (n - 1), 2 * A_BLK)
            v = rows_of(vp_ref, vc_ref, r, A_BLK * (n - 1), 2 * A_BLK)

            def store(acc, mb, lb, r=r, n=n):
                dst = slice(A_BLK * n, A_BLK * (n + 1))
                acc_s[1, r, dst, :] = acc
                m_s[1, r, dst, :] = mb
                l_s[1, r, dst, :] = lb
            tiles.append((q, k, v, n == 0, store))
    grouped(False, tiles)

    qf[...] = q_ref[...].astype(F32)
    kf[:, 0:A_PLANE, :] = kp_ref[...].astype(F32)
    kf[:, A_PLANE:2 * A_PLANE, :] = kc_ref[...].astype(F32)
    vf[:, 0:A_PLANE, :] = vp_ref[...].astype(F32)
    vf[:, A_PLANE:2 * A_PLANE, :] = vc_ref[...].astype(F32)
    tiles = []
    for r in range(R4):
        for g in range(R4):
            own = pl.ds(g, A_BLK, stride=R4)
            both = pl.ds(g, 2 * A_BLK, stride=R4)

            def store(acc, mb, lb, r=r, own=own):
                acc_s[2, r, own, :] = acc
                m_s[2, r, own, :] = mb
                l_s[2, r, own, :] = lb
            tiles.append((qf[r, own, :].astype(BF16), kf[r, both, :].astype(BF16), vf[r, both, :].astype(BF16),
                          True, store))
    grouped(False, tiles)

    ms = [m_s[pi] for pi in range(len(DILATIONS))]
    m = functools.reduce(jnp.maximum, ms)
    ws = [jnp.exp(mi - m) for mi in ms]
    num = sum(w * acc_s[pi] for pi, w in enumerate(ws))
    den = sum(w * l_s[pi] for pi, w in enumerate(ws))
    o = num / den
    for r in range(R4):
        o_nat[pl.ds(r, A_PLANE, stride=R4), :] = o[r]
    o_ref[...] = o_nat[...].astype(o_ref.dtype)


def _attn(qk, v, *, group=8):
    S = v.shape[2] * R4
    npat = len(DILATIONS)
    blk = (None, R4, A_PLANE, A_HEAD_DIM)
    prev = lambda c: jnp.maximum(c - 1, 0)
    return pl.pallas_call(
        functools.partial(_attn_kernel, group=group),
        grid=(A_HEADS, S // A_CHUNK),
        in_specs=[pl.BlockSpec(blk, lambda h, c: (h, 0, c, 0)),
                  pl.BlockSpec(blk, lambda h, c: (A_HEADS + h, 0, prev(c), 0)),
                  pl.BlockSpec(blk, lambda h, c: (A_HEADS + h, 0, c, 0)),
                  pl.BlockSpec(blk, lambda h, c: (h, 0, prev(c), 0)),
                  pl.BlockSpec(blk, lambda h, c: (h, 0, c, 0))],
        out_specs=pl.BlockSpec((A_CHUNK, A_HEAD_DIM), lambda h, c: (c, h)),
        out_shape=jax.ShapeDtypeStruct((S, A_WIDTH), BF16),
        scratch_shapes=[pltpu.VMEM((R4, A_PLANE, A_HEAD_DIM), F32),
                        pltpu.VMEM((R4, 2 * A_PLANE, A_HEAD_DIM), F32),
                        pltpu.VMEM((R4, 2 * A_PLANE, A_HEAD_DIM), F32),
                        pltpu.VMEM((npat, R4, A_PLANE, A_HEAD_DIM), F32),
                        pltpu.VMEM((npat, R4, A_PLANE, A_HEAD_DIM), F32),
                        pltpu.VMEM((npat, R4, A_PLANE, A_HEAD_DIM), F32),
                        pltpu.VMEM((A_CHUNK, A_HEAD_DIM), F32)],
        compiler_params=_params(("parallel", "arbitrary")),
        name="attn",
    )(qk, qk, qk, v, v)


def _gla_kernel(q_ref, k_ref, v_ref, b_ref, r_ref, gn_ref, o_ref, st_ref, *, nchunk):
    C, Cs = GLA_CHUNK, GLA_SUB
    NS = C // Cs

    @pl.when(pl.program_id(0) == 0)
    def _():
        st_ref[...] = jnp.zeros_like(st_ref)

    row = lax.broadcasted_iota(jnp.int32, (C, C), 0)
    col = lax.broadcasted_iota(jnp.int32, (C, C), 1)
    sub_start = (row // Cs) * Cs
    sub_row = lax.broadcasted_iota(jnp.int32, (NS, Cs, B_KEY_DIM), 1)
    levels = []
    z = C // 2
    while z >= Cs:
        levels.append((z, jnp.logical_and((row // z) % 2 == 1, col // z == row // z - 1)))
        z //= 2
    diag_masks = [col == sub_start + j for j in range(Cs)]
    for c in range(nchunk):
        rows = slice(c * C, (c + 1) * C)
        for h in range(B_HEADS):
            kcols = slice(h * B_KEY_DIM, (h + 1) * B_KEY_DIM)
            vcols = slice(h * B_VAL_DIM, (h + 1) * B_VAL_DIM)
            q = q_ref[rows, kcols] * (B_KEY_DIM ** -0.5)
            k = k_ref[rows, kcols]
            b = b_ref[rows, kcols]
            v = v_ref[rows, vcols]
            b_last = b[C - 1:C, :]
            st = st_ref[h]
            o = _dot_nt((q * jnp.exp(b)).astype(BF16), st.astype(BF16))
            k_end = (k * jnp.exp(b_last - b)).astype(BF16)
            st_ref[h] = st * jnp.exp(b_last) + _dot_tn(v, k_end)

            a = jnp.zeros((C, C), F32)
            for z, mask in levels:
                bz = b.reshape(C // z, z, B_KEY_DIM)
                ends = bz[:, z - 1:z, :]
                starts = jnp.concatenate([jnp.zeros_like(ends[:1]), ends[:-1]], axis=0)
                q_z = (q * jnp.exp(bz - starts).reshape(C, B_KEY_DIM)).astype(BF16)
                k_z = (k * jnp.exp(ends - bz).reshape(C, B_KEY_DIM)).astype(BF16)
                a = jnp.where(mask, _dot_nt(q_z, k_z), a)

            q3 = q.reshape(NS, Cs, B_KEY_DIM)
            k3 = k.reshape(NS, Cs, B_KEY_DIM)
            b3 = b.reshape(NS, Cs, B_KEY_DIM)
            for j in range(Cs):
                kb = jnp.broadcast_to(k3[:, j:j + 1, :], q3.shape)
                bb = jnp.broadcast_to(b3[:, j:j + 1, :], q3.shape)
                w = jnp.exp(jnp.where(sub_row >= j, b3 - bb, -jnp.inf))
                dj = jnp.sum((q3 * kb * w).reshape(C, B_KEY_DIM), axis=-1, keepdims=True)
                a = jnp.where(diag_masks[j], dj, a)
            o = o + _dot(a.astype(BF16), v)
            r = r_ref[rows, vcols]
            o_ref[rows, vcols] = (_rms(o, gn_ref[...]) * (r * jax.nn.sigmoid(r))).astype(o_ref.dtype)


def _gla(pf, bv, b, gain, *, rb=256):
    S = pf.shape[0]
    return pl.pallas_call(
        functools.partial(_gla_kernel, nchunk=rb // GLA_CHUNK),
        grid=(S // rb,),
        in_specs=[pl.BlockSpec((rb, B_KEY_WIDTH), lambda i: (i, 0)),
                  pl.BlockSpec((rb, B_KEY_WIDTH), lambda i: (i, 1)),
                  pl.BlockSpec((rb, B_VAL_WIDTH), lambda i: (i, 0)),
                  pl.BlockSpec((rb, B_KEY_WIDTH), lambda i: (i, 0)),
                  pl.BlockSpec((rb, B_VAL_WIDTH), lambda i: (i, 1)),
                  pl.BlockSpec((1, B_VAL_DIM), lambda i: (0, 0))],
        out_specs=pl.BlockSpec((rb, B_VAL_WIDTH), lambda i: (i, 0)),
        out_shape=jax.ShapeDtypeStruct((S, B_VAL_WIDTH), BF16),
        scratch_shapes=[pltpu.VMEM((B_HEADS, B_VAL_DIM, B_KEY_DIM), F32)],
        compiler_params=_params(("arbitrary",)),
        name="gla",
    )(pf, pf, bv, b, pf, gain)


def _merge_out_kernel(x_ref, oa_ref, ob_ref, ga_ref, gb_ref, wa_ref, wb_ref, wo_ref, o_ref):
    j = pl.program_id(1)
    tn = ga_ref.shape[1]
    cols = pl.ds(pl.multiple_of(j * tn, tn), tn)

    @pl.when(j == 0)
    def _():
        o_ref[...] = x_ref[...]

    ya = _dot(oa_ref[...], wa_ref[:, cols])
    yb = _dot(ob_ref[...], wb_ref[:, cols])
    y = jax.nn.sigmoid(ga_ref[...]) * ya + jax.nn.sigmoid(gb_ref[...]) * yb
    o_ref[...] += _dot(y.astype(BF16), wo_ref[cols, :])


def _merge_out(x, oa, ob, pf, wa, wb, wo, *, tm=512, tn=512):
    S, D = x.shape
    ga_blk = 2048 // tn
    gb_blk = 4096 // tn
    row = lambda i, j: (i, 0)
    resident = lambda w: pl.BlockSpec(w.shape, lambda i, j: (0, 0), pipeline_mode=pl.Buffered(1))
    return pl.pallas_call(
        _merge_out_kernel,
        grid=(S // tm, D // tn),
        in_specs=[pl.BlockSpec((tm, D), row),
                  pl.BlockSpec((tm, A_WIDTH), row),
                  pl.BlockSpec((tm, B_VAL_WIDTH), row),
                  pl.BlockSpec((tm, tn), lambda i, j: (i, ga_blk + j)),
                  pl.BlockSpec((tm, tn), lambda i, j: (i, gb_blk + j)),
                  resident(wa), resident(wb), resident(wo)],
        out_specs=pl.BlockSpec((tm, D), row),
        out_shape=jax.ShapeDtypeStruct((S, D), F32),
        compiler_params=_params(("parallel", "arbitrary")),
        name="merge_out",
    )(x, oa, ob, pf, pf, wa, wb, wo)


IN_COLS = np.cumsum([0, A_WIDTH, A_WIDTH, A_WIDTH, B_KEY_WIDTH, B_KEY_WIDTH, B_VAL_WIDTH, B_VAL_WIDTH,
                     GATE_RANK, D_MODEL, D_MODEL]).tolist()


def _w_in_kernel(w_ref, tail_ref, qk_ref, v_ref, f_ref, z_ref):
    c = IN_COLS
    lane = lax.broadcasted_iota(jnp.int32, (w_ref.shape[0], LANES), 1)

    def tile(col):
        return w_ref[:, col:col + LANES]

    for h in range(2 * A_HEADS):
        t = tile(h * A_HEAD_DIM)
        up = pltpu.roll(t, A_HEAD_DIM - ROPE_HALF, 1)
        down = pltpu.roll(t, ROPE_GAP - ROPE_HALF, 1)
        t = jnp.where(lane < ROPE_HALF, t, jnp.where(lane < ROPE_GAP, up, jnp.where(lane < ROPE_GAP + ROPE_HALF, down, t)))
        qk_ref[:, h * A_HEAD_DIM:(h + 1) * A_HEAD_DIM] = t.astype(qk_ref.dtype)
    v_ref[:, 0:A_WIDTH] = w_ref[:, c[2]:c[3]].astype(v_ref.dtype)
    v_ref[:, A_WIDTH:] = w_ref[:, c[5]:c[6]].astype(v_ref.dtype)
    f_ref[:, 0:2 * B_KEY_WIDTH] = w_ref[:, c[3]:c[5]].astype(f_ref.dtype)
    f_ref[:, 2 * B_KEY_WIDTH:2 * B_KEY_WIDTH + B_VAL_WIDTH] = w_ref[:, c[6]:c[7]].astype(f_ref.dtype)
    z_ref[...] = jnp.where(lane < GATE_RANK, tile(c[7]), 0.0).astype(z_ref.dtype)
    base = 2 * B_KEY_WIDTH + B_VAL_WIDTH
    ntile = 2 * D_MODEL // LANES
    cur = pltpu.roll(tile(c[7]), LANES - GATE_RANK, 1)
    for n in range(ntile):
        nxt_src = tile(c[7] + (n + 1) * LANES) if n + 1 < ntile else tail_ref[...]
        nxt = pltpu.roll(nxt_src, LANES - GATE_RANK, 1)
        f_ref[:, base + n * LANES:base + (n + 1) * LANES] = jnp.where(lane < LANES - GATE_RANK, cur, nxt).astype(f_ref.dtype)
        cur = nxt


def _w_in_layout(w_in, l, *, tk=256):
    _, K, N = w_in.shape
    tail_cols = N - (N // LANES) * LANES
    tail = jnp.pad(w_in[l, :, N - tail_cols:], ((0, 0), (0, LANES - tail_cols)))
    widths = (2 * A_WIDTH, A_WIDTH + B_VAL_WIDTH, 2 * B_KEY_WIDTH + B_VAL_WIDTH + 2 * D_MODEL, LANES)
    return pl.pallas_call(
        _w_in_kernel,
        grid=(K // tk,),
        in_specs=[pl.BlockSpec((None, tk, N), lambda i: (l, i, 0)),
                  pl.BlockSpec((tk, LANES), lambda i: (i, 0))],
        out_specs=[pl.BlockSpec((tk, n), lambda i: (i, 0)) for n in widths],
        out_shape=[jax.ShapeDtypeStruct((K, n), BF16) for n in widths],
        compiler_params=_params(("parallel",)),
        name="w_in_layout",
    )(w_in, tail)


def _layer(x, pos, p):
    x1, h = _ffn(x, p["ffn1_norm"], p["ffn1_wg"], p["ffn1_wu"], p["ffn1_wd"], p["mix_norm"], tf=256)
    qk = _proj_qk(h, p["w_qk"], p["qk_gain"], pos, p["rope_inv"])
    av, bv = _proj_v(h, p["w_v"])
    pf = _proj(h, p["w_f"], F32)
    b = _gate(h, p["w_z"], p["w_2"], p["gate_bias"])
    o_a = _attn(qk, av)
    o_b = _gla(pf, bv, b, p["b_out_norm"])
    x2 = _merge_out(x1, o_a, o_b, pf, p["w_a_up"], p["w_b_up"], p["w_out"])
    return _ffn(x2, p["ffn2_norm"], p["ffn2_wg"], p["ffn2_wu"], p["ffn2_wd"], tf=512)


def kernel(x, positions, ffn1_norm, ffn1_w_gate, ffn1_w_up, ffn1_w_down, mix_norm, w_in, a_q_norm, a_k_norm, b_gate_w2, b_gate_bias, b_out_norm, w_a_up, w_b_up, w_out, ffn2_norm, ffn2_w_gate, ffn2_w_up, ffn2_w_down):
    B, S, D = x.shape
    assert D == D_MODEL and S % A_CHUNK == 0
    depth = w_in.shape[0]
    inv = jnp.power(ROPE_THETA, -(jnp.arange(ROPE_HALF, dtype=F32) * 2.0 / ROPE_DIM))
    rope_inv = jnp.concatenate([inv, inv, jnp.zeros((A_HEAD_DIM - ROPE_DIM,), F32)])[HEAD_PERM][None, :]
    outs = []
    for bi in range(B):
        xb = x.reshape(S, D) if B == 1 else x[bi]
        pos = positions.reshape(S, 1) if B == 1 else positions[bi][:, None]
        for l in range(depth):
            w_qk, w_v, w_f, w_z = _w_in_layout(w_in, l)
            p = {
                "ffn1_norm": ffn1_norm[l][None, :], "mix_norm": mix_norm[l][None, :],
                "ffn2_norm": ffn2_norm[l][None, :],
                "ffn1_wg": ffn1_w_gate[l].astype(BF16), "ffn1_wu": ffn1_w_up[l].astype(BF16),
                "ffn1_wd": ffn1_w_down[l].astype(BF16),
                "ffn2_wg": ffn2_w_gate[l].astype(BF16), "ffn2_wu": ffn2_w_up[l].astype(BF16),
                "ffn2_wd": ffn2_w_down[l].astype(BF16),
                "w_qk": w_qk, "w_v": w_v, "w_f": w_f, "w_z": w_z,
                "w_2": jnp.pad(b_gate_w2[l], ((0, LANES - GATE_RANK), (0, 0))).astype(BF16),
                "gate_bias": b_gate_bias[l][None, :],
                "qk_gain": jnp.concatenate([jnp.tile(a_q_norm[l][HEAD_PERM], A_HEADS),
                                            jnp.tile(a_k_norm[l][HEAD_PERM], A_HEADS)])[None, :],
                "rope_inv": rope_inv,
                "b_out_norm": b_out_norm[l][None, :],
                "w_a_up": w_a_up[l].astype(BF16), "w_b_up": w_b_up[l].astype(BF16), "w_out": w_out[l].astype(BF16),
            }
            xb = _layer(xb, pos, p)
        outs.append(xb)
    return outs[0].reshape(B, S, D) if B == 1 else jnp.stack(outs, axis=0)
```

```python
import functools

import jax
import jax.numpy as jnp
import numpy as np
from jax import lax
from jax.experimental import pallas as pl
from jax.experimental.pallas import tpu as pltpu

F32 = jnp.float32
BF16 = jnp.bfloat16

D_MODEL = 2048
D_FF = 5632
RMS_EPS = 1e-6
ROPE_THETA = 500000.0
A_HEAD_DIM = 128
A_HEADS = 8
A_WIDTH = A_HEADS * A_HEAD_DIM
ROPE_DIM = A_HEAD_DIM // 4
ROPE_HALF = ROPE_DIM // 2
DILATIONS = (1, 4, 16)
A_SPAN = 128
R4 = 4
B_HEADS = 4
B_VAL_DIM = 256
B_KEY_DIM = 128
B_KEY_WIDTH = B_HEADS * B_KEY_DIM
B_VAL_WIDTH = B_HEADS * B_VAL_DIM
GATE_RANK = 16
GATE_NORMALIZER = 16.0
GLA_CHUNK = 64
GLA_SUB = 8
LANES = 128
MXU_N = 256

VMEM_LIMIT = 56 * 1024 * 1024


def _params(sem):
    return pltpu.CompilerParams(dimension_semantics=sem, vmem_limit_bytes=VMEM_LIMIT)


def _rms(x, g):
    return x * lax.rsqrt(jnp.mean(x * x, axis=-1, keepdims=True) + RMS_EPS) * g


def _dot(a, b):
    return jnp.dot(a, b, preferred_element_type=F32)


def _dot_nt(a, b):
    return lax.dot_general(a, b, (((1,), (1,)), ((), ())), preferred_element_type=F32)


def _dot_tn(a, b):
    return lax.dot_general(a, b, (((0,), (0,)), ((), ())), preferred_element_type=F32)


def _ffn_kernel(x_ref, g_ref, wg_ref, wu_ref, wd_ref, *rest, emit_next):
    if emit_next:
        gn_ref, o_ref, hn_ref, h_scr = rest
    else:
        o_ref, h_scr = rest
    f = pl.program_id(1)

    @pl.when(f == 0)
    def _():
        x = x_ref[...]
        h_scr[...] = _rms(x, g_ref[...]).astype(BF16)
        o_ref[...] = x

    h = h_scr[...]
    gate = _dot(h, wg_ref[...])
    up = _dot(h, wu_ref[...])
    act = (0.5 * (gate * jax.nn.sigmoid(gate)) * up).astype(BF16)
    o_ref[...] += _dot(act, wd_ref[...])

    if emit_next:
        @pl.when(f == pl.num_programs(1) - 1)
        def _():
            hn_ref[...] = _rms(o_ref[...], gn_ref[...]).astype(BF16)


def _ffn(x, gain, wg, wu, wd, next_gain=None, *, tm=1024, tf=256):
    S, D = x.shape
    F = wg.shape[1]
    emit_next = next_gain is not None
    row = lambda i, f: (i, 0)
    fixed = lambda i, f: (0, 0)
    in_specs = [
        pl.BlockSpec((tm, D), row),
        pl.BlockSpec((1, D), fixed),
        pl.BlockSpec((D, tf), lambda i, f: (0, f)),
        pl.BlockSpec((D, tf), lambda i, f: (0, f)),
        pl.BlockSpec((tf, D), lambda i, f: (f, 0)),
    ]
    args = [x, gain, wg, wu, wd]
    out_shape = [jax.ShapeDtypeStruct((S, D), F32)]
    out_specs = [pl.BlockSpec((tm, D), row)]
    if emit_next:
        in_specs.append(pl.BlockSpec((1, D), fixed))
        args.append(next_gain)
        out_shape.append(jax.ShapeDtypeStruct((S, D), BF16))
        out_specs.append(pl.BlockSpec((tm, D), row))
    res = pl.pallas_call(
        functools.partial(_ffn_kernel, emit_next=emit_next),
        grid=(S // tm, F // tf),
        in_specs=in_specs,
        out_specs=out_specs,
        out_shape=out_shape,
        scratch_shapes=[pltpu.VMEM((tm, D), BF16)],
        compiler_params=_params(("parallel", "arbitrary")),
        name="ffn_next" if emit_next else "ffn",
    )(*args)
    return res if emit_next else res[0]


def _proj_kernel(a_ref, w_ref, o_ref):
    o_ref[...] = _dot_nt(a_ref[...], w_ref[...]).astype(o_ref.dtype)


def _proj(a, w, out_dtype, *, tm=1024, tn=1024):
    S, K = a.shape
    N = w.shape[0]
    return pl.pallas_call(
        _proj_kernel,
        grid=(S // tm, N // tn),
        in_specs=[pl.BlockSpec((tm, K), lambda i, j: (i, 0)),
                  pl.BlockSpec((tn, K), lambda i, j: (j, 0))],
        out_specs=pl.BlockSpec((tm, tn), lambda i, j: (i, j)),
        out_shape=jax.ShapeDtypeStruct((S, N), out_dtype),
        compiler_params=_params(("parallel", "arbitrary")),
        name="proj_" + jnp.dtype(out_dtype).name,
    )(a, w)


ROPE_GAP = A_HEAD_DIM // 2
HEAD_PERM = np.concatenate([np.arange(0, ROPE_HALF), np.arange(ROPE_DIM, ROPE_DIM + ROPE_GAP - ROPE_HALF),
                            np.arange(ROPE_HALF, ROPE_DIM), np.arange(ROPE_DIM + ROPE_GAP - ROPE_HALF, A_HEAD_DIM)])


def _proj_qk_kernel(a_ref, w_ref, g_ref, pos_ref, inv_ref, o_ref, raw, rot):
    i = pl.program_id(0)
    nh = o_ref.shape[0]
    heads_per_dot = MXU_N // A_HEAD_DIM

    @pl.when(i == 0)
    def _():
        raw[1] = jnp.zeros(raw.shape[1:], raw.dtype)

    cur = i % 2
    a = a_ref[...]
    for g in range(nh // heads_per_dot):
        acc = _dot_nt(a, w_ref[g * MXU_N:(g + 1) * MXU_N, :])
        for hh in range(heads_per_dot):
            raw[cur, g * heads_per_dot + hh] = acc[:, hh * A_HEAD_DIM:(hh + 1) * A_HEAD_DIM]

    ang = pos_ref[...].astype(F32) * inv_ref[...]
    lane = lax.broadcasted_iota(jnp.int32, ang.shape, 1)
    rot[0] = jnp.cos(ang)
    rot[1] = jnp.where(lane < ROPE_GAP, -jnp.sin(ang), jnp.sin(ang))
    prev = raw.at[1 - cur]
    tq = raw.shape[2] // R4
    gain = jnp.stack([g_ref[:, h * A_HEAD_DIM:(h + 1) * A_HEAD_DIM] for h in range(nh)], axis=0)
    for r in range(R4):
        rows = pl.ds(r, tq, stride=R4)
        y = _rms(jnp.stack([prev[h, rows, :] for h in range(nh)], axis=0), gain)
        o_ref[:, r] = (y * rot[0, rows, :][None] + pltpu.roll(y, ROPE_GAP, 2) * rot[1, rows, :][None]).astype(o_ref.dtype)


def _proj_qk(a, w, gains, pos, inv, *, tm=512):
    S, K = a.shape
    N = w.shape[0]
    nt = S // tm
    lag = lambda i: jnp.maximum(i - 1, 0)
    return pl.pallas_call(
        _proj_qk_kernel,
        grid=(nt + 1,),
        in_specs=[pl.BlockSpec((tm, K), lambda i: (jnp.minimum(i, nt - 1), 0)),
                  pl.BlockSpec((N, K), lambda i: (0, 0)),
                  pl.BlockSpec((1, N), lambda i: (0, 0)),
                  pl.BlockSpec((tm, 1), lambda i: (lag(i), 0)),
                  pl.BlockSpec((1, A_HEAD_DIM), lambda i: (0, 0))],
        out_specs=pl.BlockSpec((N // A_HEAD_DIM, R4, tm // R4, A_HEAD_DIM), lambda i: (0, 0, lag(i), 0)),
        out_shape=jax.ShapeDtypeStruct((N // A_HEAD_DIM, R4, S // R4, A_HEAD_DIM), BF16),
        scratch_shapes=[pltpu.VMEM((2, N // A_HEAD_DIM, tm, A_HEAD_DIM), F32),
                        pltpu.VMEM((2, tm, A_HEAD_DIM), F32)],
        compiler_params=_params(("arbitrary",)),
        name="proj_qk",
    )(a, w, gains, pos, inv)


def _proj_v_kernel(a_ref, w_ref, oa_ref, ob_ref, raw):
    a = a_ref[...]
    heads_per_dot = MXU_N // A_HEAD_DIM
    for g in range(A_WIDTH // MXU_N):
        acc = _dot_nt(a, w_ref[g * MXU_N:(g + 1) * MXU_N, :])
        for hh in range(heads_per_dot):
            raw[g * heads_per_dot + hh] = acc[:, hh * A_HEAD_DIM:(hh + 1) * A_HEAD_DIM]
    for g in range(B_VAL_WIDTH // MXU_N):
        cols = slice(g * MXU_N, (g + 1) * MXU_N)
        ob_ref[:, cols] = _dot_nt(a, w_ref[A_WIDTH + g * MXU_N:A_WIDTH + (g + 1) * MXU_N, :]).astype(ob_ref.dtype)
    tq = raw.shape[1] // R4
    for h in range(A_HEADS):
        for r in range(R4):
            oa_ref[h, r] = raw[h, pl.ds(r, tq, stride=R4), :].astype(oa_ref.dtype)


def _proj_v(a, w, *, tm=1024):
    S, K = a.shape
    return pl.pallas_call(
        _proj_v_kernel,
        grid=(S // tm,),
        in_specs=[pl.BlockSpec((tm, K), lambda i: (i, 0)),
                  pl.BlockSpec(w.shape, lambda i: (0, 0))],
        out_specs=[pl.BlockSpec((A_HEADS, R4, tm // R4, A_HEAD_DIM), lambda i: (0, 0, i, 0)),
                   pl.BlockSpec((tm, B_VAL_WIDTH), lambda i: (i, 0))],
        out_shape=[jax.ShapeDtypeStruct((A_HEADS, R4, S // R4, A_HEAD_DIM), BF16),
                   jax.ShapeDtypeStruct((S, B_VAL_WIDTH), BF16)],
        scratch_shapes=[pltpu.VMEM((A_HEADS, tm, A_HEAD_DIM), F32)],
        compiler_params=_params(("parallel",)),
        name="proj_v",
    )(a, w)


def _gate_kernel(a_ref, wz_ref, w2_ref, bias_ref, o_ref):
    z = _dot_nt(a_ref[...], wz_ref[...])
    pre = _dot(z.astype(BF16), w2_ref[...]) + bias_ref[...]
    g = (jnp.minimum(pre, 0.0) - jnp.log1p(jnp.exp(-jnp.abs(pre)))) / GATE_NORMALIZER
    r = lax.broadcasted_iota(jnp.int32, g.shape, 0) % GLA_CHUNK
    shift = 1
    while shift < GLA_CHUNK:
        g = g + jnp.where(r >= shift, pltpu.roll(g, shift, 0), 0.0)
        shift *= 2
    o_ref[...] = g


def _gate(a, wz, w2, bias, *, tm=1024):
    S, K = a.shape
    return pl.pallas_call(
        _gate_kernel,
        grid=(S // tm,),
        in_specs=[pl.BlockSpec((tm, K), lambda i: (i, 0)),
                  pl.BlockSpec(wz.shape, lambda i: (0, 0)),
                  pl.BlockSpec(w2.shape, lambda i: (0, 0)),
                  pl.BlockSpec(bias.shape, lambda i: (0, 0))],
        out_specs=pl.BlockSpec((tm, B_KEY_WIDTH), lambda i: (i, 0)),
        out_shape=jax.ShapeDtypeStruct((S, B_KEY_WIDTH), F32),
        compiler_params=_params(("parallel",)),
        name="gate",
    )(a, wz, w2, bias)


A_BLK = 128
A_CHUNK = A_BLK * max(DILATIONS)
A_PLANE = A_CHUNK // R4


def _attn_kernel(q_ref, kp_ref, kc_ref, vp_ref, vc_ref, o_ref, qf, kf, vf, acc_s, m_s, l_s, o_nat, *, group):
    c = pl.program_id(1)
    first_chunk = (c == 0).astype(jnp.int32)
    scale = A_HEAD_DIM ** -0.5
    row = lax.broadcasted_iota(jnp.int32, (1, A_BLK, 2 * A_BLK), 1)
    col = lax.broadcasted_iota(jnp.int32, (1, A_BLK, 2 * A_BLK), 2)
    dist_strided = row + A_SPAN - col
    prev_strided = col < A_BLK
    qn, kn = A_BLK // R4, 2 * A_BLK // R4
    dist_mixed = (R4 * (row % qn) + row // qn) - (R4 * (col % kn) + col // kn) + A_SPAN
    prev_mixed = (col % kn) < kn // 2

    def rows_of(prev_ref, cur_ref, plane, lo, n_rows):
        if lo >= 0:
            return cur_ref[plane, lo:lo + n_rows, :]
        return jnp.concatenate([prev_ref[plane, A_PLANE + lo:A_PLANE, :], cur_ref[plane, 0:lo + n_rows, :]], axis=0)

    def run_group(mixed, tiles):
        dist = dist_mixed if mixed else dist_strided
        band = jnp.logical_and(dist >= 0, dist <= A_SPAN)
        s = jnp.concatenate([_dot_nt(t[0], t[1]) for t in tiles], axis=0).reshape(len(tiles), A_BLK, 2 * A_BLK)
        s = jnp.where(band, s * scale, -jnp.inf)
        firsts = [gi for gi, t in enumerate(tiles) if t[3]]
        if firsts:
            gidx = lax.broadcasted_iota(jnp.int32, (len(tiles), 1, 1), 0)
            is_first = functools.reduce(jnp.logical_or, [gidx == gi for gi in firsts]).astype(jnp.int32) * first_chunk
            s = jnp.where(jnp.logical_and(prev_mixed if mixed else prev_strided, is_first > 0), -jnp.inf, s)
        m = jnp.max(s, axis=-1, keepdims=True)
        p = jnp.exp(s - m)
        l = jnp.sum(p, axis=-1, keepdims=True)
        pb = p.astype(BF16)
        mb = jnp.broadcast_to(m, (len(tiles), A_BLK, A_HEAD_DIM))
        lb = jnp.broadcast_to(l, (len(tiles), A_BLK, A_HEAD_DIM))
        for gi, t in enumerate(tiles):
            t[4](_dot(pb[gi], t[2]), mb[gi], lb[gi])

    def grouped(mixed, tiles):
        for g0 in range(0, len(tiles), group):
            run_group(mixed, tiles[g0:g0 + group])

    tiles = []
    for n in range(A_CHUNK // A_BLK):
        q = jnp.concatenate([q_ref[r, qn * n:qn * (n + 1), :] for r in range(R4)], axis=0)
        k = jnp.concatenate([rows_of(kp_ref, kc_ref, r, qn * (n - 1), kn) for r in range(R4)], axis=0)
        v = jnp.concatenate([rows_of(vp_ref, vc_ref, r, qn * (n - 1), kn) for r in range(R4)], axis=0)

        def store(acc, mb, lb, n=n):
            for r in range(R4):
                dst = slice(qn * n, qn * (n + 1))
                src = slice(qn * r, qn * (r + 1))
                acc_s[0, r, dst, :] = acc[src]
                m_s[0, r, dst, :] = mb[src]
                l_s[0, r, dst, :] = lb[src]
        tiles.append((q, k, v, n == 0, store))
    grouped(True, tiles)

    tiles = []
    for r in range(R4):
        for n in range(A_PLANE // A_BLK):
            q = q_ref[r, A_BLK * n:A_BLK * (n + 1), :]
            k = rows_of(kp_ref, kc_ref, r, A_BLK * (n - 1), 2 * A_BLK)
            v = rows_of(vp_ref, vc_ref, r, A_BLK * (n - 1), 2 * A_BLK)

            def store(acc, mb, lb, r=r, n=n):
                dst = slice(A_BLK * n, A_BLK * (n + 1))
                acc_s[1, r, dst, :] = acc
                m_s[1, r, dst, :] = mb
                l_s[1, r, dst, :] = lb
            tiles.append((q, k, v, n == 0, store))
    grouped(False, tiles)

    qf[...] = q_ref[...].astype(F32)
    kf[:, 0:A_PLANE, :] = kp_ref[...].astype(F32)
    kf[:, A_PLANE:2 * A_PLANE, :] = kc_ref[...].astype(F32)
    vf[:, 0:A_PLANE, :] = vp_ref[...].astype(F32)
    vf[:, A_PLANE:2 * A_PLANE, :] = vc_ref[...].astype(F32)
    tiles = []
    for r in range(R4):
        for g in range(R4):
            own = pl.ds(g, A_BLK, stride=R4)
            both = pl.ds(g, 2 * A_BLK, stride=R4)

            def store(acc, mb, lb, r=r, own=own):
                acc_s[2, r, own, :] = acc
                m_s[2, r, own, :] = mb
                l_s[2, r, own, :] = lb
            tiles.append((qf[r, own, :].astype(BF16), kf[r, both, :].astype(BF16), vf[r, both, :].astype(BF16),
                          True, store))
    grouped(False, tiles)

    ms = [m_s[pi] for pi in range(len(DILATIONS))]
    m = functools.reduce(jnp.maximum, ms)
    ws = [jnp.exp(mi - m) for mi in ms]
    num = sum(w * acc_s[pi] for pi, w in enumerate(ws))
    den = sum(w * l_s[pi] for pi, w in enumerate(ws))
    o = num / den
    for r in range(R4):
        o_nat[pl.ds(r, A_PLANE, stride=R4), :] = o[r]
    o_ref[...] = o_nat[...].astype(o_ref.dtype)


def _attn(qk, v, *, group=8):
    S = v.shape[2] * R4
    npat = len(DILATIONS)
    blk = (None, R4, A_PLANE, A_HEAD_DIM)
    prev = lambda c: jnp.maximum(c - 1, 0)
    return pl.pallas_call(
        functools.partial(_attn_kernel, group=group),
        grid=(A_HEADS, S // A_CHUNK),
        in_specs=[pl.BlockSpec(blk, lambda h, c: (h, 0, c, 0)),
                  pl.BlockSpec(blk, lambda h, c: (A_HEADS + h, 0, prev(c), 0)),
                  pl.BlockSpec(blk, lambda h, c: (A_HEADS + h, 0, c, 0)),
                  pl.BlockSpec(blk, lambda h, c: (h, 0, prev(c), 0)),
                  pl.BlockSpec(blk, lambda h, c: (h, 0, c, 0))],
        out_specs=pl.BlockSpec((A_CHUNK, A_HEAD_DIM), lambda h, c: (c, h)),
        out_shape=jax.ShapeDtypeStruct((S, A_WIDTH), BF16),
        scratch_shapes=[pltpu.VMEM((R4, A_PLANE, A_HEAD_DIM), F32),
                        pltpu.VMEM((R4, 2 * A_PLANE, A_HEAD_DIM), F32),
                        pltpu.VMEM((R4, 2 * A_PLANE, A_HEAD_DIM), F32),
                        pltpu.VMEM((npat, R4, A_PLANE, A_HEAD_DIM), F32),
                        pltpu.VMEM((npat, R4, A_PLANE, A_HEAD_DIM), F32),
                        pltpu.VMEM((npat, R4, A_PLANE, A_HEAD_DIM), F32),
                        pltpu.VMEM((A_CHUNK, A_HEAD_DIM), F32)],
        compiler_params=_params(("parallel", "arbitrary")),
        name="attn",
    )(qk, qk, qk, v, v)


def _gla_kernel(q_ref, k_ref, v_ref, b_ref, r_ref, gn_ref, o_ref, st_ref, *, nchunk):
    C, Cs = GLA_CHUNK, GLA_SUB
    NS = C // Cs

    @pl.when(pl.program_id(0) == 0)
    def _():
        st_ref[...] = jnp.zeros_like(st_ref)

    row = lax.broadcasted_iota(jnp.int32, (C, C), 0)
    col = lax.broadcasted_iota(jnp.int32, (C, C), 1)
    sub_start = (row // Cs) * Cs
    sub_row = lax.broadcasted_iota(jnp.int32, (NS, Cs, B_KEY_DIM), 1)
    levels = []
    z = C // 2
    while z >= Cs:
        levels.append((z, jnp.logical_and((row // z) % 2 == 1, col // z == row // z - 1)))
        z //= 2
    diag_masks = [col == sub_start + j for j in range(Cs)]
    for c in range(nchunk):
        rows = slice(c * C, (c + 1) * C)
        for h in range(B_HEADS):
            kcols = slice(h * B_KEY_DIM, (h + 1) * B_KEY_DIM)
            vcols = slice(h * B_VAL_DIM, (h + 1) * B_VAL_DIM)
            q = q_ref[rows, kcols] * (B_KEY_DIM ** -0.5)
            k = k_ref[rows, kcols]
            b = b_ref[rows, kcols]
            v = v_ref[rows, vcols]
            b_last = b[C - 1:C, :]
            st = st_ref[h]
            o = _dot_nt((q * jnp.exp(b)).astype(BF16), st.astype(BF16))
            k_end = (k * jnp.exp(b_last - b)).astype(BF16)
            st_ref[h] = st * jnp.exp(b_last) + _dot_tn(v, k_end)

            a = jnp.zeros((C, C), F32)
            for z, mask in levels:
                bz = b.reshape(C // z, z, B_KEY_DIM)
                ends = bz[:, z - 1:z, :]
                starts = jnp.concatenate([jnp.zeros_like(ends[:1]), ends[:-1]], axis=0)
                q_z = (q * jnp.exp(bz - starts).reshape(C, B_KEY_DIM)).astype(BF16)
                k_z = (k * jnp.exp(ends - bz).reshape(C, B_KEY_DIM)).astype(BF16)
                a = jnp.where(mask, _dot_nt(q_z, k_z), a)

            q3 = q.reshape(NS, Cs, B_KEY_DIM)
            k3 = k.reshape(NS, Cs, B_KEY_DIM)
            b3 = b.reshape(NS, Cs, B_KEY_DIM)
            for j in range(Cs):
                kb = jnp.broadcast_to(k3[:, j:j + 1, :], q3.shape)
                bb = jnp.broadcast_to(b3[:, j:j + 1, :], q3.shape)
                w = jnp.exp(jnp.where(sub_row >= j, b3 - bb, -jnp.inf))
                dj = jnp.sum((q3 * kb * w).reshape(C, B_KEY_DIM), axis=-1, keepdims=True)
                a = jnp.where(diag_masks[j], dj, a)
            o = o + _dot(a.astype(BF16), v)
            r = r_ref[rows, vcols]
            o_ref[rows, vcols] = (_rms(o, gn_ref[...]) * (r * jax.nn.sigmoid(r))).astype(o_ref.dtype)


def _gla(pf, bv, b, gain, *, rb=256):
    S = pf.shape[0]
    return pl.pallas_call(
        functools.partial(_gla_kernel, nchunk=rb // GLA_CHUNK),
        grid=(S // rb,),
        in_specs=[pl.BlockSpec((rb, B_KEY_WIDTH), lambda i: (i, 0)),
                  pl.BlockSpec((rb, B_KEY_WIDTH), lambda i: (i, 1)),
                  pl.BlockSpec((rb, B_VAL_WIDTH), lambda i: (i, 0)),
                  pl.BlockSpec((rb, B_KEY_WIDTH), lambda i: (i, 0)),
                  pl.BlockSpec((rb, B_VAL_WIDTH), lambda i: (i, 1)),
                  pl.BlockSpec((1, B_VAL_DIM), lambda i: (0, 0))],
        out_specs=pl.BlockSpec((rb, B_VAL_WIDTH), lambda i: (i, 0)),
        out_shape=jax.ShapeDtypeStruct((S, B_VAL_WIDTH), BF16),
        scratch_shapes=[pltpu.VMEM((B_HEADS, B_VAL_DIM, B_KEY_DIM), F32)],
        compiler_params=_params(("arbitrary",)),
        name="gla",
    )(pf, pf, bv, b, pf, gain)


def _merge_out_kernel(x_ref, oa_ref, ob_ref, ga_ref, gb_ref, wa_ref, wb_ref, wo_ref, o_ref):
    j = pl.program_id(1)
    tn = ga_ref.shape[1]
    cols = pl.ds(pl.multiple_of(j * tn, tn), tn)

    @pl.when(j == 0)
    def _():
        o_ref[...] = x_ref[...]

    ya = _dot(oa_ref[...], wa_ref[:, cols])
    yb = _dot(ob_ref[...], wb_ref[:, cols])
    y = jax.nn.sigmoid(ga_ref[...]) * ya + jax.nn.sigmoid(gb_ref[...]) * yb
    o_ref[...] += _dot(y.astype(BF16), wo_ref[cols, :])


def _merge_out(x, oa, ob, pf, wa, wb, wo, *, tm=512, tn=1024):
    S, D = x.shape
    ga_blk = 2048 // tn
    gb_blk = 4096 // tn
    row = lambda i, j: (i, 0)
    resident = lambda w: pl.BlockSpec(w.shape, lambda i, j: (0, 0), pipeline_mode=pl.Buffered(1))
    return pl.pallas_call(
        _merge_out_kernel,
        grid=(S // tm, D // tn),
        in_specs=[pl.BlockSpec((tm, D), row),
                  pl.BlockSpec((tm, A_WIDTH), row),
                  pl.BlockSpec((tm, B_VAL_WIDTH), row),
                  pl.BlockSpec((tm, tn), lambda i, j: (i, ga_blk + j)),
                  pl.BlockSpec((tm, tn), lambda i, j: (i, gb_blk + j)),
                  resident(wa), resident(wb), resident(wo)],
        out_specs=pl.BlockSpec((tm, D), row),
        out_shape=jax.ShapeDtypeStruct((S, D), F32),
        compiler_params=_params(("parallel", "arbitrary")),
        name="merge_out",
    )(x, oa, ob, pf, pf, wa, wb, wo)


IN_COLS = np.cumsum([0, A_WIDTH, A_WIDTH, A_WIDTH, B_KEY_WIDTH, B_KEY_WIDTH, B_VAL_WIDTH, B_VAL_WIDTH,
                     GATE_RANK, D_MODEL, D_MODEL]).tolist()


def _w_in_kernel(w_ref, qk_ref, v_ref, f_ref, z_ref):
    c = IN_COLS

    def rows(lo, hi):
        return w_ref[lo:hi, :].astype(BF16)

    perm_runs = [(0, ROPE_HALF), (ROPE_DIM, ROPE_DIM + ROPE_GAP - ROPE_HALF), (ROPE_HALF, ROPE_DIM),
                 (ROPE_DIM + ROPE_GAP - ROPE_HALF, A_HEAD_DIM)]
    for h in range(2 * A_HEADS):
        dst = h * A_HEAD_DIM
        for lo, hi in perm_runs:
            qk_ref[dst:dst + hi - lo, :] = rows(h * A_HEAD_DIM + lo, h * A_HEAD_DIM + hi)
            dst += hi - lo
    v_ref[0:A_WIDTH, :] = rows(c[2], c[3])
    v_ref[A_WIDTH:, :] = rows(c[5], c[6])
    f_ref[0:2 * B_KEY_WIDTH, :] = rows(c[3], c[5])
    f_ref[2 * B_KEY_WIDTH:2 * B_KEY_WIDTH + B_VAL_WIDTH, :] = rows(c[6], c[7])
    f_ref[2 * B_KEY_WIDTH + B_VAL_WIDTH:, :] = rows(c[8], c[10])
    z_ref[0:GATE_RANK, :] = rows(c[7], c[8])
    z_ref[GATE_RANK:, :] = jnp.zeros((z_ref.shape[0] - GATE_RANK, z_ref.shape[1]), z_ref.dtype)


def _w_in_layout(w_t, l, *, tk=256):
    _, N, K = w_t.shape
    heights = (2 * A_WIDTH, A_WIDTH + B_VAL_WIDTH, 2 * B_KEY_WIDTH + B_VAL_WIDTH + 2 * D_MODEL, LANES)
    return pl.pallas_call(
        _w_in_kernel,
        grid=(K // tk,),
        in_specs=[pl.BlockSpec((None, N, tk), lambda i: (l, 0, i))],
        out_specs=[pl.BlockSpec((n, tk), lambda i: (0, i)) for n in heights],
        out_shape=[jax.ShapeDtypeStruct((n, K), BF16) for n in heights],
        compiler_params=_params(("parallel",)),
        name="w_in_layout",
    )(w_t)


def _layer(x, pos, p):
    x1, h = _ffn(x, p["ffn1_norm"], p["ffn1_wg"], p["ffn1_wu"], p["ffn1_wd"], p["mix_norm"], tf=256)
    qk = _proj_qk(h, p["w_qk"], p["qk_gain"], pos, p["rope_inv"])
    av, bv = _proj_v(h, p["w_v"])
    pf = _proj(h, p["w_f"], F32)
    b = _gate(h, p["w_z"], p["w_2"], p["gate_bias"])
    o_a = _attn(qk, av)
    o_b = _gla(pf, bv, b, p["b_out_norm"])
    x2 = _merge_out(x1, o_a, o_b, pf, p["w_a_up"], p["w_b_up"], p["w_out"])
    return _ffn(x2, p["ffn2_norm"], p["ffn2_wg"], p["ffn2_wu"], p["ffn2_wd"], tf=512)


def kernel(x, positions, ffn1_norm, ffn1_w_gate, ffn1_w_up, ffn1_w_down, mix_norm, w_in, a_q_norm, a_k_norm, b_gate_w2, b_gate_bias, b_out_norm, w_a_up, w_b_up, w_out, ffn2_norm, ffn2_w_gate, ffn2_w_up, ffn2_w_down):
    B, S, D = x.shape
    assert D == D_MODEL and S % A_CHUNK == 0
    depth = w_in.shape[0]
    w_in_t = jnp.swapaxes(w_in, 1, 2)
    inv = jnp.power(ROPE_THETA, -(jnp.arange(ROPE_HALF, dtype=F32) * 2.0 / ROPE_DIM))
    rope_inv = jnp.concatenate([inv, inv, jnp.zeros((A_HEAD_DIM - ROPE_DIM,), F32)])[HEAD_PERM][None, :]
    outs = []
    for bi in range(B):
        xb = x.reshape(S, D) if B == 1 else x[bi]
        pos = positions.reshape(S, 1) if B == 1 else positions[bi][:, None]
        for l in range(depth):
            w_qk, w_v, w_f, w_z = _w_in_layout(w_in_t, l)
            p = {
                "ffn1_norm": ffn1_norm[l][None, :], "mix_norm": mix_norm[l][None, :],
                "ffn2_norm": ffn2_norm[l][None, :],
                "ffn1_wg": ffn1_w_gate[l].astype(BF16), "ffn1_wu": ffn1_w_up[l].astype(BF16),
                "ffn1_wd": ffn1_w_down[l].astype(BF16),
                "ffn2_wg": ffn2_w_gate[l].astype(BF16), "ffn2_wu": ffn2_w_up[l].astype(BF16),
                "ffn2_wd": ffn2_w_down[l].astype(BF16),
                "w_qk": w_qk, "w_v": w_v, "w_f": w_f, "w_z": w_z,
                "w_2": jnp.pad(b_gate_w2[l], ((0, LANES - GATE_RANK), (0, 0))).astype(BF16),
                "gate_bias": b_gate_bias[l][None, :],
                "qk_gain": jnp.concatenate([jnp.tile(a_q_norm[l][HEAD_PERM], A_HEADS),
                                            jnp.tile(a_k_norm[l][HEAD_PERM], A_HEADS)])[None, :],
                "rope_inv": rope_inv,
                "b_out_norm": b_out_norm[l][None, :],
                "w_a_up": w_a_up[l].astype(BF16), "w_b_up": w_b_up[l].astype(BF16), "w_out": w_out[l].astype(BF16),
            }
            xb = _layer(xb, pos, p)
        outs.append(xb)
    return outs[0].reshape(B, S, D) if B == 1 else jnp.stack(outs, axis=0)
```

```python
import functools

import jax
import jax.numpy as jnp
import numpy as np
from jax import lax
from jax.experimental import pallas as pl
from jax.experimental.pallas import tpu as pltpu

F32 = jnp.float32
BF16 = jnp.bfloat16

D_MODEL = 2048
D_FF = 5632
RMS_EPS = 1e-6
ROPE_THETA = 500000.0
A_HEAD_DIM = 128
A_HEADS = 8
A_WIDTH = A_HEADS * A_HEAD_DIM
ROPE_DIM = A_HEAD_DIM // 4
ROPE_HALF = ROPE_DIM // 2
DILATIONS = (1, 4, 16)
A_SPAN = 128
R4 = 4
B_HEADS = 4
B_VAL_DIM = 256
B_KEY_DIM = 128
B_KEY_WIDTH = B_HEADS * B_KEY_DIM
B_VAL_WIDTH = B_HEADS * B_VAL_DIM
GATE_RANK = 16
GATE_NORMALIZER = 16.0
GLA_CHUNK = 64
GLA_SUB = 8
LANES = 128
MXU_N = 256

VMEM_LIMIT = 56 * 1024 * 1024


def _params(sem):
    return pltpu.CompilerParams(dimension_semantics=sem, vmem_limit_bytes=VMEM_LIMIT)


def _rms(x, g):
    return x * lax.rsqrt(jnp.mean(x * x, axis=-1, keepdims=True) + RMS_EPS) * g


def _dot(a, b):
    return jnp.dot(a, b, preferred_element_type=F32)


def _dot_nt(a, b):
    return lax.dot_general(a, b, (((1,), (1,)), ((), ())), preferred_element_type=F32)


def _dot_tn(a, b):
    return lax.dot_general(a, b, (((0,), (0,)), ((), ())), preferred_element_type=F32)


def _ffn_kernel(x_ref, g_ref, wg_ref, wu_ref, wd_ref, *rest, emit_next):
    if emit_next:
        gn_ref, o_ref, hn_ref, h_scr = rest
    else:
        o_ref, h_scr = rest
    f = pl.program_id(1)

    @pl.when(f == 0)
    def _():
        x = x_ref[...]
        h_scr[...] = _rms(x, g_ref[...]).astype(BF16)
        o_ref[...] = x

    h = h_scr[...]
    gate = _dot(h, wg_ref[...])
    up = _dot(h, wu_ref[...])
    act = (0.5 * (gate * jax.nn.sigmoid(gate)) * up).astype(BF16)
    o_ref[...] += _dot(act, wd_ref[...])

    if emit_next:
        @pl.when(f == pl.num_programs(1) - 1)
        def _():
            hn_ref[...] = _rms(o_ref[...], gn_ref[...]).astype(BF16)


def _ffn(x, gain, wg, wu, wd, next_gain=None, *, tm=1024, tf=256):
    S, D = x.shape
    F = wg.shape[1]
    emit_next = next_gain is not None
    row = lambda i, f: (i, 0)
    fixed = lambda i, f: (0, 0)
    in_specs = [
        pl.BlockSpec((tm, D), row),
        pl.BlockSpec((1, D), fixed),
        pl.BlockSpec((D, tf), lambda i, f: (0, f)),
        pl.BlockSpec((D, tf), lambda i, f: (0, f)),
        pl.BlockSpec((tf, D), lambda i, f: (f, 0)),
    ]
    args = [x, gain, wg, wu, wd]
    out_shape = [jax.ShapeDtypeStruct((S, D), F32)]
    out_specs = [pl.BlockSpec((tm, D), row)]
    if emit_next:
        in_specs.append(pl.BlockSpec((1, D), fixed))
        args.append(next_gain)
        out_shape.append(jax.ShapeDtypeStruct((S, D), BF16))
        out_specs.append(pl.BlockSpec((tm, D), row))
    res = pl.pallas_call(
        functools.partial(_ffn_kernel, emit_next=emit_next),
        grid=(S // tm, F // tf),
        in_specs=in_specs,
        out_specs=out_specs,
        out_shape=out_shape,
        scratch_shapes=[pltpu.VMEM((tm, D), BF16)],
        compiler_params=_params(("parallel", "arbitrary")),
        name="ffn_next" if emit_next else "ffn",
    )(*args)
    return res if emit_next else res[0]


def _proj_kernel(a_ref, w_ref, o_ref):
    o_ref[...] = _dot_nt(a_ref[...], w_ref[...]).astype(o_ref.dtype)


def _proj(a, w, out_dtype, *, tm=1024, tn=1024):
    S, K = a.shape
    N = w.shape[0]
    return pl.pallas_call(
        _proj_kernel,
        grid=(S // tm, N // tn),
        in_specs=[pl.BlockSpec((tm, K), lambda i, j: (i, 0)),
                  pl.BlockSpec((tn, K), lambda i, j: (j, 0))],
        out_specs=pl.BlockSpec((tm, tn), lambda i, j: (i, j)),
        out_shape=jax.ShapeDtypeStruct((S, N), out_dtype),
        compiler_params=_params(("parallel", "arbitrary")),
        name="proj_" + jnp.dtype(out_dtype).name,
    )(a, w)


ROPE_GAP = A_HEAD_DIM // 2
HEAD_PERM = np.concatenate([np.arange(0, ROPE_HALF), np.arange(ROPE_DIM, ROPE_DIM + ROPE_GAP - ROPE_HALF),
                            np.arange(ROPE_HALF, ROPE_DIM), np.arange(ROPE_DIM + ROPE_GAP - ROPE_HALF, A_HEAD_DIM)])


def _proj_qk_kernel(a_ref, w_ref, g_ref, pos_ref, inv_ref, o_ref, raw, rot):
    i = pl.program_id(0)
    nh = o_ref.shape[0]
    heads_per_dot = MXU_N // A_HEAD_DIM

    @pl.when(i == 0)
    def _():
        raw[1] = jnp.zeros(raw.shape[1:], raw.dtype)

    cur = i % 2
    a = a_ref[...]
    for g in range(nh // heads_per_dot):
        acc = _dot_nt(a, w_ref[g * MXU_N:(g + 1) * MXU_N, :])
        for hh in range(heads_per_dot):
            raw[cur, g * heads_per_dot + hh] = acc[:, hh * A_HEAD_DIM:(hh + 1) * A_HEAD_DIM]

    ang = pos_ref[...].astype(F32) * inv_ref[...]
    lane = lax.broadcasted_iota(jnp.int32, ang.shape, 1)
    rot[0] = jnp.cos(ang)
    rot[1] = jnp.where(lane < ROPE_GAP, -jnp.sin(ang), jnp.sin(ang))
    prev = raw.at[1 - cur]
    tq = raw.shape[2] // R4
    gain = jnp.stack([g_ref[:, h * A_HEAD_DIM:(h + 1) * A_HEAD_DIM] for h in range(nh)], axis=0)
    for r in range(R4):
        rows = pl.ds(r, tq, stride=R4)
        y = _rms(jnp.stack([prev[h, rows, :] for h in range(nh)], axis=0), gain)
        o_ref[:, r] = (y * rot[0, rows, :][None] + pltpu.roll(y, ROPE_GAP, 2) * rot[1, rows, :][None]).astype(o_ref.dtype)


def _proj_qk(a, w, gains, pos, inv, *, tm=1024):
    S, K = a.shape
    N = w.shape[0]
    nt = S // tm
    lag = lambda i: jnp.maximum(i - 1, 0)
    return pl.pallas_call(
        _proj_qk_kernel,
        grid=(nt + 1,),
        in_specs=[pl.BlockSpec((tm, K), lambda i: (jnp.minimum(i, nt - 1), 0)),
                  pl.BlockSpec((N, K), lambda i: (0, 0), pipeline_mode=pl.Buffered(1)),
                  pl.BlockSpec((1, N), lambda i: (0, 0)),
                  pl.BlockSpec((tm, 1), lambda i: (lag(i), 0)),
                  pl.BlockSpec((1, A_HEAD_DIM), lambda i: (0, 0))],
        out_specs=pl.BlockSpec((N // A_HEAD_DIM, R4, tm // R4, A_HEAD_DIM), lambda i: (0, 0, lag(i), 0)),
        out_shape=jax.ShapeDtypeStruct((N // A_HEAD_DIM, R4, S // R4, A_HEAD_DIM), BF16),
        scratch_shapes=[pltpu.VMEM((2, N // A_HEAD_DIM, tm, A_HEAD_DIM), F32),
                        pltpu.VMEM((2, tm, A_HEAD_DIM), F32)],
        compiler_params=_params(("arbitrary",)),
        name="proj_qk",
    )(a, w, gains, pos, inv)


def _chunk_log_decay(z, w2, bias):
    pre = _dot(z.astype(BF16), w2) + bias
    g = (jnp.minimum(pre, 0.0) - jnp.log1p(jnp.exp(-jnp.abs(pre)))) / GATE_NORMALIZER
    r = lax.broadcasted_iota(jnp.int32, g.shape, 0) % GLA_CHUNK
    shift = 1
    while shift < GLA_CHUNK:
        g = g + jnp.where(r >= shift, pltpu.roll(g, shift, 0), 0.0)
        shift *= 2
    return g


def _proj_v_kernel(a_ref, w_ref, wz_ref, w2_ref, bias_ref, oa_ref, ob_ref, b_ref, raw):
    a = a_ref[...]
    b_ref[...] = _chunk_log_decay(_dot_nt(a, wz_ref[...]), w2_ref[...], bias_ref[...])
    heads_per_dot = MXU_N // A_HEAD_DIM
    for g in range(A_WIDTH // MXU_N):
        acc = _dot_nt(a, w_ref[g * MXU_N:(g + 1) * MXU_N, :])
        for hh in range(heads_per_dot):
            raw[g * heads_per_dot + hh] = acc[:, hh * A_HEAD_DIM:(hh + 1) * A_HEAD_DIM]
    for g in range(B_VAL_WIDTH // MXU_N):
        cols = slice(g * MXU_N, (g + 1) * MXU_N)
        ob_ref[:, cols] = _dot_nt(a, w_ref[A_WIDTH + g * MXU_N:A_WIDTH + (g + 1) * MXU_N, :]).astype(ob_ref.dtype)
    tq = raw.shape[1] // R4
    for h in range(A_HEADS):
        for r in range(R4):
            oa_ref[h, r] = raw[h, pl.ds(r, tq, stride=R4), :].astype(oa_ref.dtype)


def _proj_v(a, w, wz, w2, bias, *, tm=1024):
    S, K = a.shape
    fixed = lambda t: pl.BlockSpec(t.shape, lambda i: (0, 0))
    return pl.pallas_call(
        _proj_v_kernel,
        grid=(S // tm,),
        in_specs=[pl.BlockSpec((tm, K), lambda i: (i, 0)), fixed(w), fixed(wz), fixed(w2), fixed(bias)],
        out_specs=[pl.BlockSpec((A_HEADS, R4, tm // R4, A_HEAD_DIM), lambda i: (0, 0, i, 0)),
                   pl.BlockSpec((tm, B_VAL_WIDTH), lambda i: (i, 0)),
                   pl.BlockSpec((tm, B_KEY_WIDTH), lambda i: (i, 0))],
        out_shape=[jax.ShapeDtypeStruct((A_HEADS, R4, S // R4, A_HEAD_DIM), BF16),
                   jax.ShapeDtypeStruct((S, B_VAL_WIDTH), BF16),
                   jax.ShapeDtypeStruct((S, B_KEY_WIDTH), F32)],
        scratch_shapes=[pltpu.VMEM((A_HEADS, tm, A_HEAD_DIM), F32)],
        compiler_params=_params(("parallel",)),
        name="proj_v",
    )(a, w, wz, w2, bias)


A_BLK = 128
A_CHUNK = A_BLK * max(DILATIONS)
A_PLANE = A_CHUNK // R4


def _attn_kernel(q_ref, kp_ref, kc_ref, vp_ref, vc_ref, o_ref, qf, kf, vf, acc_s, m_s, l_s, o_nat, *, group):
    c = pl.program_id(1)
    first_chunk = (c == 0).astype(jnp.int32)
    scale = A_HEAD_DIM ** -0.5
    row = lax.broadcasted_iota(jnp.int32, (1, A_BLK, 2 * A_BLK), 1)
    col = lax.broadcasted_iota(jnp.int32, (1, A_BLK, 2 * A_BLK), 2)
    dist_strided = row + A_SPAN - col
    prev_strided = col < A_BLK
    qn, kn = A_BLK // R4, 2 * A_BLK // R4
    dist_mixed = (R4 * (row % qn) + row // qn) - (R4 * (col % kn) + col // kn) + A_SPAN
    prev_mixed = (col % kn) < kn // 2

    def rows_of(prev_ref, cur_ref, plane, lo, n_rows):
        if lo >= 0:
            return cur_ref[plane, lo:lo + n_rows, :]
        return jnp.concatenate([prev_ref[plane, A_PLANE + lo:A_PLANE, :], cur_ref[plane, 0:lo + n_rows, :]], axis=0)

    def run_group(mixed, tiles):
        dist = dist_mixed if mixed else dist_strided
        band = jnp.logical_and(dist >= 0, dist <= A_SPAN)
        s = jnp.concatenate([_dot_nt(t[0], t[1]) for t in tiles], axis=0).reshape(len(tiles), A_BLK, 2 * A_BLK)
        s = jnp.where(band, s * scale, -jnp.inf)
        firsts = [gi for gi, t in enumerate(tiles) if t[3]]
        if firsts:
            gidx = lax.broadcasted_iota(jnp.int32, (len(tiles), 1, 1), 0)
            is_first = functools.reduce(jnp.logical_or, [gidx == gi for gi in firsts]).astype(jnp.int32) * first_chunk
            s = jnp.where(jnp.logical_and(prev_mixed if mixed else prev_strided, is_first > 0), -jnp.inf, s)
        m = jnp.max(s, axis=-1, keepdims=True)
        p = jnp.exp(s - m)
        l = jnp.sum(p, axis=-1, keepdims=True)
        pb = p.astype(BF16)
        mb = jnp.broadcast_to(m, (len(tiles), A_BLK, A_HEAD_DIM))
        lb = jnp.broadcast_to(l, (len(tiles), A_BLK, A_HEAD_DIM))
        for gi, t in enumerate(tiles):
            t[4](_dot(pb[gi], t[2]), mb[gi], lb[gi])

    def grouped(mixed, tiles):
        for g0 in range(0, len(tiles), group):
            run_group(mixed, tiles[g0:g0 + group])

    tiles = []
    for n in range(A_CHUNK // A_BLK):
        q = jnp.concatenate([q_ref[r, qn * n:qn * (n + 1), :] for r in range(R4)], axis=0)
        k = jnp.concatenate([rows_of(kp_ref, kc_ref, r, qn * (n - 1), kn) for r in range(R4)], axis=0)
        v = jnp.concatenate([rows_of(vp_ref, vc_ref, r, qn * (n - 1), kn) for r in range(R4)], axis=0)

        def store(acc, mb, lb, n=n):
            for r in range(R4):
                dst = slice(qn * n, qn * (n + 1))
                src = slice(qn * r, qn * (r + 1))
                acc_s[0, r, dst, :] = acc[src]
                m_s[0, r, dst, :] = mb[src]
                l_s[0, r, dst, :] = lb[src]
        tiles.append((q, k, v, n == 0, store))
    grouped(True, tiles)

    tiles = []
    for r in range(R4):
        for n in range(A_PLANE // A_BLK):
            q = q_ref[r, A_BLK * n:A_BLK * (n + 1), :]
            k = rows_of(kp_ref, kc_ref, r, A_BLK * (n - 1), 2 * A_BLK)
            v = rows_of(vp_ref, vc_ref, r, A_BLK * (n - 1), 2 * A_BLK)

            def store(acc, mb, lb, r=r, n=n):
                dst = slice(A_BLK * n, A_BLK * (n + 1))
                acc_s[1, r, dst, :] = acc
                m_s[1, r, dst, :] = mb
                l_s[1, r, dst, :] = lb
            tiles.append((q, k, v, n == 0, store))
    grouped(False, tiles)

    qf[...] = q_ref[...].astype(F32)
    kf[:, 0:A_PLANE, :] = kp_ref[...].astype(F32)
    kf[:, A_PLANE:2 * A_PLANE, :] = kc_ref[...].astype(F32)
    vf[:, 0:A_PLANE, :] = vp_ref[...].astype(F32)
    vf[:, A_PLANE:2 * A_PLANE, :] = vc_ref[...].astype(F32)
    tiles = []
    for r in range(R4):
        for g in range(R4):
            own = pl.ds(g, A_BLK, stride=R4)
            both = pl.ds(g, 2 * A_BLK, stride=R4)

            def store(acc, mb, lb, r=r, own=own):
                acc_s[2, r, own, :] = acc
                m_s[2, r, own, :] = mb
                l_s[2, r, own, :] = lb
            tiles.append((qf[r, own, :].astype(BF16), kf[r, both, :].astype(BF16), vf[r, both, :].astype(BF16),
                          True, store))
    grouped(False, tiles)

    ms = [m_s[pi] for pi in range(len(DILATIONS))]
    m = functools.reduce(jnp.maximum, ms)
    ws = [jnp.exp(mi - m) for mi in ms]
    num = sum(w * acc_s[pi] for pi, w in enumerate(ws))
    den = sum(w * l_s[pi] for pi, w in enumerate(ws))
    o = num / den
    for r in range(R4):
        o_nat[pl.ds(r, A_PLANE, stride=R4), :] = o[r]
    o_ref[...] = o_nat[...].astype(o_ref.dtype)


def _attn(qk, v, *, group=8):
    S = v.shape[2] * R4
    npat = len(DILATIONS)
    blk = (None, R4, A_PLANE, A_HEAD_DIM)
    prev = lambda c: jnp.maximum(c - 1, 0)
    return pl.pallas_call(
        functools.partial(_attn_kernel, group=group),
        grid=(A_HEADS, S // A_CHUNK),
        in_specs=[pl.BlockSpec(blk, lambda h, c: (h, 0, c, 0)),
                  pl.BlockSpec(blk, lambda h, c: (A_HEADS + h, 0, prev(c), 0)),
                  pl.BlockSpec(blk, lambda h, c: (A_HEADS + h, 0, c, 0)),
                  pl.BlockSpec(blk, lambda h, c: (h, 0, prev(c), 0)),
                  pl.BlockSpec(blk, lambda h, c: (h, 0, c, 0))],
        out_specs=pl.BlockSpec((A_CHUNK, A_HEAD_DIM), lambda h, c: (c, h)),
        out_shape=jax.ShapeDtypeStruct((S, A_WIDTH), BF16),
        scratch_shapes=[pltpu.VMEM((R4, A_PLANE, A_HEAD_DIM), F32),
                        pltpu.VMEM((R4, 2 * A_PLANE, A_HEAD_DIM), F32),
                        pltpu.VMEM((R4, 2 * A_PLANE, A_HEAD_DIM), F32),
                        pltpu.VMEM((npat, R4, A_PLANE, A_HEAD_DIM), F32),
                        pltpu.VMEM((npat, R4, A_PLANE, A_HEAD_DIM), F32),
                        pltpu.VMEM((npat, R4, A_PLANE, A_HEAD_DIM), F32),
                        pltpu.VMEM((A_CHUNK, A_HEAD_DIM), F32)],
        compiler_params=_params(("parallel", "arbitrary")),
        name="attn",
    )(qk, qk, qk, v, v)


def _gla_kernel(q_ref, k_ref, v_ref, b_ref, r_ref, gn_ref, o_ref, st_ref, *, nchunk):
    C, Cs = GLA_CHUNK, GLA_SUB
    NS = C // Cs

    @pl.when(pl.program_id(0) == 0)
    def _():
        st_ref[...] = jnp.zeros_like(st_ref)

    row = lax.broadcasted_iota(jnp.int32, (C, C), 0)
    col = lax.broadcasted_iota(jnp.int32, (C, C), 1)
    sub_start = (row // Cs) * Cs
    sub_row = lax.broadcasted_iota(jnp.int32, (NS, Cs, B_KEY_DIM), 1)
    levels = []
    z = C // 2
    while z >= Cs:
        levels.append((z, jnp.logical_and((row // z) % 2 == 1, col // z == row // z - 1)))
        z //= 2
    diag_masks = [col == sub_start + j for j in range(Cs)]
    for c in range(nchunk):
        rows = slice(c * C, (c + 1) * C)
        for h in range(B_HEADS):
            kcols = slice(h * B_KEY_DIM, (h + 1) * B_KEY_DIM)
            vcols = slice(h * B_VAL_DIM, (h + 1) * B_VAL_DIM)
            q = q_ref[rows, kcols] * (B_KEY_DIM ** -0.5)
            k = k_ref[rows, kcols]
            b = b_ref[rows, kcols]
            v = v_ref[rows, vcols]
            b_last = b[C - 1:C, :]
            st = st_ref[h]
            o = _dot_nt((q * jnp.exp(b)).astype(BF16), st.astype(BF16))
            k_end = (k * jnp.exp(b_last - b)).astype(BF16)
            st_ref[h] = st * jnp.exp(b_last) + _dot_tn(v, k_end)

            a = jnp.zeros((C, C), F32)
            for z, mask in levels:
                bz = b.reshape(C // z, z, B_KEY_DIM)
                ends = bz[:, z - 1:z, :]
                starts = jnp.concatenate([jnp.zeros_like(ends[:1]), ends[:-1]], axis=0)
                q_z = (q * jnp.exp(bz - starts).reshape(C, B_KEY_DIM)).astype(BF16)
                k_z = (k * jnp.exp(ends - bz).reshape(C, B_KEY_DIM)).astype(BF16)
                a = jnp.where(mask, _dot_nt(q_z, k_z), a)

            q3 = q.reshape(NS, Cs, B_KEY_DIM)
            k3 = k.reshape(NS, Cs, B_KEY_DIM)
            b3 = b.reshape(NS, Cs, B_KEY_DIM)
            for j in range(Cs):
                kb = jnp.broadcast_to(k3[:, j:j + 1, :], q3.shape)
                bb = jnp.broadcast_to(b3[:, j:j + 1, :], q3.shape)
                w = jnp.exp(jnp.where(sub_row >= j, b3 - bb, -jnp.inf))
                dj = jnp.sum((q3 * kb * w).reshape(C, B_KEY_DIM), axis=-1, keepdims=True)
                a = jnp.where(diag_masks[j], dj, a)
            o = o + _dot(a.astype(BF16), v)
            r = r_ref[rows, vcols]
            o_ref[rows, vcols] = (_rms(o, gn_ref[...]) * (r * jax.nn.sigmoid(r))).astype(o_ref.dtype)


def _gla(pf, bv, b, gain, *, rb=256):
    S = pf.shape[0]
    return pl.pallas_call(
        functools.partial(_gla_kernel, nchunk=rb // GLA_CHUNK),
        grid=(S // rb,),
        in_specs=[pl.BlockSpec((rb, B_KEY_WIDTH), lambda i: (i, 0)),
                  pl.BlockSpec((rb, B_KEY_WIDTH), lambda i: (i, 1)),
                  pl.BlockSpec((rb, B_VAL_WIDTH), lambda i: (i, 0)),
                  pl.BlockSpec((rb, B_KEY_WIDTH), lambda i: (i, 0)),
                  pl.BlockSpec((rb, B_VAL_WIDTH), lambda i: (i, 1)),
                  pl.BlockSpec((1, B_VAL_DIM), lambda i: (0, 0))],
        out_specs=pl.BlockSpec((rb, B_VAL_WIDTH), lambda i: (i, 0)),
        out_shape=jax.ShapeDtypeStruct((S, B_VAL_WIDTH), BF16),
        scratch_shapes=[pltpu.VMEM((B_HEADS, B_VAL_DIM, B_KEY_DIM), F32)],
        compiler_params=_params(("arbitrary",)),
        name="gla",
    )(pf, pf, bv, b, pf, gain)


def _merge_out_kernel(x_ref, oa_ref, ob_ref, ga_ref, gb_ref, wa_ref, wb_ref, wo_ref, o_ref):
    j = pl.program_id(1)
    tn = ga_ref.shape[1]
    cols = pl.ds(pl.multiple_of(j * tn, tn), tn)

    @pl.when(j == 0)
    def _():
        o_ref[...] = x_ref[...]

    ya = _dot(oa_ref[...], wa_ref[:, cols])
    yb = _dot(ob_ref[...], wb_ref[:, cols])
    y = jax.nn.sigmoid(ga_ref[...]) * ya + jax.nn.sigmoid(gb_ref[...]) * yb
    o_ref[...] += _dot(y.astype(BF16), wo_ref[cols, :])


def _merge_out(x, oa, ob, pf, wa, wb, wo, *, tm=512, tn=1024):
    S, D = x.shape
    ga_blk = 2048 // tn
    gb_blk = 4096 // tn
    row = lambda i, j: (i, 0)
    resident = lambda w: pl.BlockSpec(w.shape, lambda i, j: (0, 0), pipeline_mode=pl.Buffered(1))
    return pl.pallas_call(
        _merge_out_kernel,
        grid=(S // tm, D // tn),
        in_specs=[pl.BlockSpec((tm, D), row),
                  pl.BlockSpec((tm, A_WIDTH), row),
                  pl.BlockSpec((tm, B_VAL_WIDTH), row),
                  pl.BlockSpec((tm, tn), lambda i, j: (i, ga_blk + j)),
                  pl.BlockSpec((tm, tn), lambda i, j: (i, gb_blk + j)),
                  resident(wa), resident(wb), resident(wo)],
        out_specs=pl.BlockSpec((tm, D), row),
        out_shape=jax.ShapeDtypeStruct((S, D), F32),
        compiler_params=_params(("parallel", "arbitrary")),
        name="merge_out",
    )(x, oa, ob, pf, pf, wa, wb, wo)


IN_COLS = np.cumsum([0, A_WIDTH, A_WIDTH, A_WIDTH, B_KEY_WIDTH, B_KEY_WIDTH, B_VAL_WIDTH, B_VAL_WIDTH,
                     GATE_RANK, D_MODEL, D_MODEL]).tolist()


def _w_in_kernel(w_ref, qk_ref, v_ref, f_ref, z_ref):
    c = IN_COLS

    def rows(lo, hi):
        return w_ref[lo:hi, :].astype(BF16)

    perm_runs = [(0, ROPE_HALF), (ROPE_DIM, ROPE_DIM + ROPE_GAP - ROPE_HALF), (ROPE_HALF, ROPE_DIM),
                 (ROPE_DIM + ROPE_GAP - ROPE_HALF, A_HEAD_DIM)]
    for h in range(2 * A_HEADS):
        dst = h * A_HEAD_DIM
        for lo, hi in perm_runs:
            qk_ref[dst:dst + hi - lo, :] = rows(h * A_HEAD_DIM + lo, h * A_HEAD_DIM + hi)
            dst += hi - lo
    v_ref[0:A_WIDTH, :] = rows(c[2], c[3])
    v_ref[A_WIDTH:, :] = rows(c[5], c[6])
    f_ref[0:2 * B_KEY_WIDTH, :] = rows(c[3], c[5])
    f_ref[2 * B_KEY_WIDTH:2 * B_KEY_WIDTH + B_VAL_WIDTH, :] = rows(c[6], c[7])
    f_ref[2 * B_KEY_WIDTH + B_VAL_WIDTH:, :] = rows(c[8], c[10])
    z_ref[0:GATE_RANK, :] = rows(c[7], c[8])
    z_ref[GATE_RANK:, :] = jnp.zeros((z_ref.shape[0] - GATE_RANK, z_ref.shape[1]), z_ref.dtype)


def _w_in_layout(w_t, l, *, tk=256):
    _, N, K = w_t.shape
    heights = (2 * A_WIDTH, A_WIDTH + B_VAL_WIDTH, 2 * B_KEY_WIDTH + B_VAL_WIDTH + 2 * D_MODEL, LANES)
    return pl.pallas_call(
        _w_in_kernel,
        grid=(K // tk,),
        in_specs=[pl.BlockSpec((None, N, tk), lambda i: (l, 0, i))],
        out_specs=[pl.BlockSpec((n, tk), lambda i: (0, i)) for n in heights],
        out_shape=[jax.ShapeDtypeStruct((n, K), BF16) for n in heights],
        compiler_params=_params(("parallel",)),
        name="w_in_layout",
    )(w_t)


def _layer(x, pos, p):
    x1, h = _ffn(x, p["ffn1_norm"], p["ffn1_wg"], p["ffn1_wu"], p["ffn1_wd"], p["mix_norm"], tf=256)
    qk = _proj_qk(h, p["w_qk"], p["qk_gain"], pos, p["rope_inv"])
    av, bv, b = _proj_v(h, p["w_v"], p["w_z"], p["w_2"], p["gate_bias"])
    pf = _proj(h, p["w_f"], F32)
    o_a = _attn(qk, av)
    o_b = _gla(pf, bv, b, p["b_out_norm"])
    x2 = _merge_out(x1, o_a, o_b, pf, p["w_a_up"], p["w_b_up"], p["w_out"])
    return _ffn(x2, p["ffn2_norm"], p["ffn2_wg"], p["ffn2_wu"], p["ffn2_wd"], tf=512)


def kernel(x, positions, ffn1_norm, ffn1_w_gate, ffn1_w_up, ffn1_w_down, mix_norm, w_in, a_q_norm, a_k_norm, b_gate_w2, b_gate_bias, b_out_norm, w_a_up, w_b_up, w_out, ffn2_norm, ffn2_w_gate, ffn2_w_up, ffn2_w_down):
    B, S, D = x.shape
    assert D == D_MODEL and S % A_CHUNK == 0
    depth = w_in.shape[0]
    w_in_t = jnp.swapaxes(w_in, 1, 2)
    inv = jnp.power(ROPE_THETA, -(jnp.arange(ROPE_HALF, dtype=F32) * 2.0 / ROPE_DIM))
    rope_inv = jnp.concatenate([inv, inv, jnp.zeros((A_HEAD_DIM - ROPE_DIM,), F32)])[HEAD_PERM][None, :]
    outs = []
    for bi in range(B):
        xb = x.reshape(S, D) if B == 1 else x[bi]
        pos = positions.reshape(S, 1) if B == 1 else positions[bi][:, None]
        for l in range(depth):
            w_qk, w_v, w_f, w_z = _w_in_layout(w_in_t, l)
            p = {
                "ffn1_norm": ffn1_norm[l][None, :], "mix_norm": mix_norm[l][None, :],
                "ffn2_norm": ffn2_norm[l][None, :],
                "ffn1_wg": ffn1_w_gate[l].astype(BF16), "ffn1_wu": ffn1_w_up[l].astype(BF16),
                "ffn1_wd": ffn1_w_down[l].astype(BF16),
                "ffn2_wg": ffn2_w_gate[l].astype(BF16), "ffn2_wu": ffn2_w_up[l].astype(BF16),
                "ffn2_wd": ffn2_w_down[l].astype(BF16),
                "w_qk": w_qk, "w_v": w_v, "w_f": w_f, "w_z": w_z,
                "w_2": jnp.pad(b_gate_w2[l], ((0, LANES - GATE_RANK), (0, 0))).astype(BF16),
                "gate_bias": b_gate_bias[l][None, :],
                "qk_gain": jnp.concatenate([jnp.tile(a_q_norm[l][HEAD_PERM], A_HEADS),
                                            jnp.tile(a_k_norm[l][HEAD_PERM], A_HEADS)])[None, :],
                "rope_inv": rope_inv,
                "b_out_norm": b_out_norm[l][None, :],
                "w_a_up": w_a_up[l].astype(BF16), "w_b_up": w_b_up[l].astype(BF16), "w_out": w_out[l].astype(BF16),
            }
            xb = _layer(xb, pos, p)
        outs.append(xb)
    return outs[0].reshape(B, S, D) if B == 1 else jnp.stack(outs, axis=0)
```

```python
import functools

import jax
import jax.numpy as jnp
import numpy as np
from jax import lax
from jax.experimental import pallas as pl
from jax.experimental.pallas import tpu as pltpu

F32 = jnp.float32
BF16 = jnp.bfloat16

D_MODEL = 2048
D_FF = 5632
RMS_EPS = 1e-6
ROPE_THETA = 500000.0
A_HEAD_DIM = 128
A_HEADS = 8
A_WIDTH = A_HEADS * A_HEAD_DIM
ROPE_DIM = A_HEAD_DIM // 4
ROPE_HALF = ROPE_DIM // 2
DILATIONS = (1, 4, 16)
A_SPAN = 128
R4 = 4
B_HEADS = 4
B_VAL_DIM = 256
B_KEY_DIM = 128
B_KEY_WIDTH = B_HEADS * B_KEY_DIM
B_VAL_WIDTH = B_HEADS * B_VAL_DIM
GATE_RANK = 16
GATE_NORMALIZER = 16.0
GLA_CHUNK = 64
GLA_SUB = 8
LANES = 128
MXU_N = 256

VMEM_LIMIT = 56 * 1024 * 1024


def _params(sem):
    return pltpu.CompilerParams(dimension_semantics=sem, vmem_limit_bytes=VMEM_LIMIT)


def _rms(x, g):
    return x * lax.rsqrt(jnp.mean(x * x, axis=-1, keepdims=True) + RMS_EPS) * g


def _dot(a, b):
    return jnp.dot(a, b, preferred_element_type=F32)


def _dot_nt(a, b):
    return lax.dot_general(a, b, (((1,), (1,)), ((), ())), preferred_element_type=F32)


def _dot_tn(a, b):
    return lax.dot_general(a, b, (((0,), (0,)), ((), ())), preferred_element_type=F32)


def _ffn_kernel(x_ref, g_ref, wg_ref, wu_ref, wd_ref, *rest, emit_next):
    if emit_next:
        gn_ref, o_ref, hn_ref, h_scr = rest
    else:
        o_ref, h_scr = rest
    f = pl.program_id(1)

    @pl.when(f == 0)
    def _():
        x = x_ref[...]
        h_scr[...] = _rms(x, g_ref[...]).astype(BF16)
        o_ref[...] = x

    h = h_scr[...]
    gate = _dot(h, wg_ref[...])
    up = _dot(h, wu_ref[...])
    act = (0.5 * (gate * jax.nn.sigmoid(gate)) * up).astype(BF16)
    o_ref[...] += _dot(act, wd_ref[...])

    if emit_next:
        @pl.when(f == pl.num_programs(1) - 1)
        def _():
            hn_ref[...] = _rms(o_ref[...], gn_ref[...]).astype(BF16)


def _ffn(x, gain, wg, wu, wd, next_gain=None, *, tm=1024, tf=256):
    S, D = x.shape
    F = wg.shape[1]
    emit_next = next_gain is not None
    row = lambda i, f: (i, 0)
    fixed = lambda i, f: (0, 0)
    in_specs = [
        pl.BlockSpec((tm, D), row),
        pl.BlockSpec((1, D), fixed),
        pl.BlockSpec((D, tf), lambda i, f: (0, f)),
        pl.BlockSpec((D, tf), lambda i, f: (0, f)),
        pl.BlockSpec((tf, D), lambda i, f: (f, 0)),
    ]
    args = [x, gain, wg, wu, wd]
    out_shape = [jax.ShapeDtypeStruct((S, D), F32)]
    out_specs = [pl.BlockSpec((tm, D), row)]
    if emit_next:
        in_specs.append(pl.BlockSpec((1, D), fixed))
        args.append(next_gain)
        out_shape.append(jax.ShapeDtypeStruct((S, D), BF16))
        out_specs.append(pl.BlockSpec((tm, D), row))
    res = pl.pallas_call(
        functools.partial(_ffn_kernel, emit_next=emit_next),
        grid=(S // tm, F // tf),
        in_specs=in_specs,
        out_specs=out_specs,
        out_shape=out_shape,
        scratch_shapes=[pltpu.VMEM((tm, D), BF16)],
        compiler_params=_params(("parallel", "arbitrary")),
        name="ffn_next" if emit_next else "ffn",
    )(*args)
    return res if emit_next else res[0]


def _proj_kernel(a_ref, w_ref, o_ref):
    o_ref[...] = _dot_nt(a_ref[...], w_ref[...]).astype(o_ref.dtype)


def _proj(a, w, out_dtype, *, tm=1024, tn=1024):
    S, K = a.shape
    N = w.shape[0]
    return pl.pallas_call(
        _proj_kernel,
        grid=(S // tm, N // tn),
        in_specs=[pl.BlockSpec((tm, K), lambda i, j: (i, 0)),
                  pl.BlockSpec((tn, K), lambda i, j: (j, 0))],
        out_specs=pl.BlockSpec((tm, tn), lambda i, j: (i, j)),
        out_shape=jax.ShapeDtypeStruct((S, N), out_dtype),
        compiler_params=_params(("parallel", "arbitrary")),
        name="proj_" + jnp.dtype(out_dtype).name,
    )(a, w)


ROPE_GAP = A_HEAD_DIM // 2
HEAD_PERM = np.concatenate([np.arange(0, ROPE_HALF), np.arange(ROPE_DIM, ROPE_DIM + ROPE_GAP - ROPE_HALF),
                            np.arange(ROPE_HALF, ROPE_DIM), np.arange(ROPE_DIM + ROPE_GAP - ROPE_HALF, A_HEAD_DIM)])


def _proj_qk_kernel(a_ref, w_ref, g_ref, pos_ref, inv_ref, o_ref, raw):
    i = pl.program_id(0)
    nh = o_ref.shape[0]
    heads_per_dot = MXU_N // A_HEAD_DIM

    @pl.when(i == 0)
    def _():
        raw[1] = jnp.zeros(raw.shape[1:], raw.dtype)

    cur = i % 2
    acc = _dot_nt(a_ref[...], w_ref[...])
    for h in range(nh):
        raw[cur, h] = acc[:, h * A_HEAD_DIM:(h + 1) * A_HEAD_DIM]

    group = LANES // R4
    p4 = pos_ref[...].astype(F32)
    tq = p4.shape[0]
    lane = lax.broadcasted_iota(jnp.int32, (tq, LANES), 1)
    pos_lanes = p4[:, R4 - 1:R4]
    for r in reversed(range(R4 - 1)):
        pos_lanes = jnp.where(lane < (r + 1) * group, p4[:, r:r + 1], pos_lanes)
    ang = pos_lanes * inv_ref[...]
    cos_all = jnp.cos(ang)
    sin_all = jnp.sin(ang)
    lo = lane < ROPE_HALF
    hi = jnp.logical_and(lane >= ROPE_GAP, lane < ROPE_GAP + ROPE_HALF)

    def to_head_lanes(t, r):
        shifts = ((LANES - r * group) % LANES, (ROPE_GAP - r * group) % LANES)
        return [pltpu.roll(t, s, 1) if s else t for s in shifts]

    prev = raw.at[1 - cur]
    gain = jnp.stack([g_ref[:, h * A_HEAD_DIM:(h + 1) * A_HEAD_DIM] for h in range(nh)], axis=0)
    for r in range(R4):
        rows = pl.ds(r, tq, stride=R4)
        c1, c2 = to_head_lanes(cos_all, r)
        s1, s2 = to_head_lanes(sin_all, r)
        cosf = jnp.where(lo, c1, jnp.where(hi, c2, 1.0))
        sinf = jnp.where(lo, -s1, jnp.where(hi, s2, 0.0))
        y = _rms(jnp.stack([prev[h, rows, :] for h in range(nh)], axis=0), gain)
        o_ref[:, r] = (y * cosf[None] + pltpu.roll(y, ROPE_GAP, 2) * sinf[None]).astype(o_ref.dtype)


def _proj_qk(a, w, gains, pos4, inv, *, tm=1024):
    S, K = a.shape
    N = w.shape[0]
    nt = S // tm
    lag = lambda i: jnp.maximum(i - 1, 0)
    return pl.pallas_call(
        _proj_qk_kernel,
        grid=(nt + 1,),
        in_specs=[pl.BlockSpec((tm, K), lambda i: (jnp.minimum(i, nt - 1), 0)),
                  pl.BlockSpec((N, K), lambda i: (0, 0), pipeline_mode=pl.Buffered(1)),
                  pl.BlockSpec((1, N), lambda i: (0, 0)),
                  pl.BlockSpec((tm // R4, R4), lambda i: (lag(i), 0)),
                  pl.BlockSpec((1, LANES), lambda i: (0, 0))],
        out_specs=pl.BlockSpec((N // A_HEAD_DIM, R4, tm // R4, A_HEAD_DIM), lambda i: (0, 0, lag(i), 0)),
        out_shape=jax.ShapeDtypeStruct((N // A_HEAD_DIM, R4, S // R4, A_HEAD_DIM), BF16),
        scratch_shapes=[pltpu.VMEM((2, N // A_HEAD_DIM, tm, A_HEAD_DIM), F32)],
        compiler_params=_params(("arbitrary",)),
        name="proj_qk",
    )(a, w, gains, pos4, inv)


def _chunk_log_decay(z, w2, bias):
    pre = _dot(z.astype(BF16), w2) + bias
    g = (jnp.minimum(pre, 0.0) - jnp.log(1.0 + jnp.exp(-jnp.abs(pre)))) / GATE_NORMALIZER
    r = lax.broadcasted_iota(jnp.int32, g.shape, 0) % GLA_CHUNK
    shift = 1
    while shift < GLA_CHUNK:
        g = g + jnp.where(r >= shift, pltpu.roll(g, shift, 0), 0.0)
        shift *= 2
    return g


def _proj_v_kernel(a_ref, w_ref, wz_ref, w2_ref, bias_ref, oa_ref, ob_ref, b_ref, raw):
    a = a_ref[...]
    b_ref[...] = _chunk_log_decay(_dot_nt(a, wz_ref[...]), w2_ref[...], bias_ref[...])
    heads_per_dot = MXU_N // A_HEAD_DIM
    for g in range(A_WIDTH // MXU_N):
        acc = _dot_nt(a, w_ref[g * MXU_N:(g + 1) * MXU_N, :])
        for hh in range(heads_per_dot):
            raw[g * heads_per_dot + hh] = acc[:, hh * A_HEAD_DIM:(hh + 1) * A_HEAD_DIM]
    for g in range(B_VAL_WIDTH // MXU_N):
        cols = slice(g * MXU_N, (g + 1) * MXU_N)
        ob_ref[:, cols] = _dot_nt(a, w_ref[A_WIDTH + g * MXU_N:A_WIDTH + (g + 1) * MXU_N, :]).astype(ob_ref.dtype)
    tq = raw.shape[1] // R4
    for h in range(A_HEADS):
        for r in range(R4):
            oa_ref[h, r] = raw[h, pl.ds(r, tq, stride=R4), :].astype(oa_ref.dtype)


def _proj_v(a, w, wz, w2, bias, *, tm=1024):
    S, K = a.shape
    fixed = lambda t: pl.BlockSpec(t.shape, lambda i: (0, 0))
    return pl.pallas_call(
        _proj_v_kernel,
        grid=(S // tm,),
        in_specs=[pl.BlockSpec((tm, K), lambda i: (i, 0)), fixed(w), fixed(wz), fixed(w2), fixed(bias)],
        out_specs=[pl.BlockSpec((A_HEADS, R4, tm // R4, A_HEAD_DIM), lambda i: (0, 0, i, 0)),
                   pl.BlockSpec((tm, B_VAL_WIDTH), lambda i: (i, 0)),
                   pl.BlockSpec((tm, B_KEY_WIDTH), lambda i: (i, 0))],
        out_shape=[jax.ShapeDtypeStruct((A_HEADS, R4, S // R4, A_HEAD_DIM), BF16),
                   jax.ShapeDtypeStruct((S, B_VAL_WIDTH), BF16),
                   jax.ShapeDtypeStruct((S, B_KEY_WIDTH), F32)],
        scratch_shapes=[pltpu.VMEM((A_HEADS, tm, A_HEAD_DIM), F32)],
        compiler_params=_params(("parallel",)),
        name="proj_v",
    )(a, w, wz, w2, bias)


A_BLK = 128
A_CHUNK = A_BLK * max(DILATIONS)
A_PLANE = A_CHUNK // R4


def _attn_kernel(q_ref, kp_ref, kc_ref, vp_ref, vc_ref, o_ref, qf, kf, vf, acc_s, m_s, l_s, o_nat, *, group):
    c = pl.program_id(1)
    first_chunk = (c == 0).astype(jnp.int32)
    scale = A_HEAD_DIM ** -0.5
    row = lax.broadcasted_iota(jnp.int32, (1, A_BLK, 2 * A_BLK), 1)
    col = lax.broadcasted_iota(jnp.int32, (1, A_BLK, 2 * A_BLK), 2)
    dist_strided = row + A_SPAN - col
    prev_strided = col < A_BLK
    qn, kn = A_BLK // R4, 2 * A_BLK // R4
    dist_mixed = (R4 * (row % qn) + row // qn) - (R4 * (col % kn) + col // kn) + A_SPAN
    prev_mixed = (col % kn) < kn // 2

    def rows_of(prev_ref, cur_ref, plane, lo, n_rows):
        if lo >= 0:
            return cur_ref[plane, lo:lo + n_rows, :]
        return jnp.concatenate([prev_ref[plane, A_PLANE + lo:A_PLANE, :], cur_ref[plane, 0:lo + n_rows, :]], axis=0)

    def run_group(mixed, tiles):
        dist = dist_mixed if mixed else dist_strided
        band = jnp.logical_and(dist >= 0, dist <= A_SPAN)
        s = jnp.concatenate([_dot_nt(t[0], t[1]) for t in tiles], axis=0).reshape(len(tiles), A_BLK, 2 * A_BLK)
        s = jnp.where(band, s * scale, -jnp.inf)
        firsts = [gi for gi, t in enumerate(tiles) if t[3]]
        if firsts:
            gidx = lax.broadcasted_iota(jnp.int32, (len(tiles), 1, 1), 0)
            is_first = functools.reduce(jnp.logical_or, [gidx == gi for gi in firsts]).astype(jnp.int32) * first_chunk
            s = jnp.where(jnp.logical_and(prev_mixed if mixed else prev_strided, is_first > 0), -jnp.inf, s)
        m = jnp.max(s, axis=-1, keepdims=True)
        p = jnp.exp(s - m)
        l = jnp.sum(p, axis=-1, keepdims=True)
        pb = p.astype(BF16)
        mb = jnp.broadcast_to(m, (len(tiles), A_BLK, A_HEAD_DIM))
        lb = jnp.broadcast_to(l, (len(tiles), A_BLK, A_HEAD_DIM))
        for gi, t in enumerate(tiles):
            t[4](_dot(pb[gi], t[2]), mb[gi], lb[gi])

    def grouped(mixed, tiles):
        for g0 in range(0, len(tiles), group):
            run_group(mixed, tiles[g0:g0 + group])

    tiles = []
    for n in range(A_CHUNK // A_BLK):
        q = jnp.concatenate([q_ref[r, qn * n:qn * (n + 1), :] for r in range(R4)], axis=0)
        k = jnp.concatenate([rows_of(kp_ref, kc_ref, r, qn * (n - 1), kn) for r in range(R4)], axis=0)
        v = jnp.concatenate([rows_of(vp_ref, vc_ref, r, qn * (n - 1), kn) for r in range(R4)], axis=0)

        def store(acc, mb, lb, n=n):
            for r in range(R4):
                dst = slice(qn * n, qn * (n + 1))
                src = slice(qn * r, qn * (r + 1))
                acc_s[0, r, dst, :] = acc[src]
                m_s[0, r, dst, :] = mb[src]
                l_s[0, r, dst, :] = lb[src]
        tiles.append((q, k, v, n == 0, store))
    grouped(True, tiles)

    tiles = []
    for r in range(R4):
        for n in range(A_PLANE // A_BLK):
            q = q_ref[r, A_BLK * n:A_BLK * (n + 1), :]
            k = rows_of(kp_ref, kc_ref, r, A_BLK * (n - 1), 2 * A_BLK)
            v = rows_of(vp_ref, vc_ref, r, A_BLK * (n - 1), 2 * A_BLK)

            def store(acc, mb, lb, r=r, n=n):
                dst = slice(A_BLK * n, A_BLK * (n + 1))
                acc_s[1, r, dst, :] = acc
                m_s[1, r, dst, :] = mb
                l_s[1, r, dst, :] = lb
            tiles.append((q, k, v, n == 0, store))
    grouped(False, tiles)

    qf[...] = q_ref[...].astype(F32)
    kf[:, 0:A_PLANE, :] = kp_ref[...].astype(F32)
    kf[:, A_PLANE:2 * A_PLANE, :] = kc_ref[...].astype(F32)
    vf[:, 0:A_PLANE, :] = vp_ref[...].astype(F32)
    vf[:, A_PLANE:2 * A_PLANE, :] = vc_ref[...].astype(F32)
    tiles = []
    for r in range(R4):
        for g in range(R4):
            own = pl.ds(g, A_BLK, stride=R4)
            both = pl.ds(g, 2 * A_BLK, stride=R4)

            def store(acc, mb, lb, r=r, own=own):
                acc_s[2, r, own, :] = acc
                m_s[2, r, own, :] = mb
                l_s[2, r, own, :] = lb
            tiles.append((qf[r, own, :].astype(BF16), kf[r, both, :].astype(BF16), vf[r, both, :].astype(BF16),
                          True, store))
    grouped(False, tiles)

    ms = [m_s[pi] for pi in range(len(DILATIONS))]
    m = functools.reduce(jnp.maximum, ms)
    ws = [jnp.exp(mi - m) for mi in ms]
    num = sum(w * acc_s[pi] for pi, w in enumerate(ws))
    den = sum(w * l_s[pi] for pi, w in enumerate(ws))
    o = num / den
    for r in range(R4):
        o_nat[pl.ds(r, A_PLANE, stride=R4), :] = o[r]
    o_ref[...] = o_nat[...].astype(o_ref.dtype)


def _attn(qk, v, *, group=8):
    S = v.shape[2] * R4
    npat = len(DILATIONS)
    blk = (None, R4, A_PLANE, A_HEAD_DIM)
    prev = lambda c: jnp.maximum(c - 1, 0)
    return pl.pallas_call(
        functools.partial(_attn_kernel, group=group),
        grid=(A_HEADS, S // A_CHUNK),
        in_specs=[pl.BlockSpec(blk, lambda h, c: (h, 0, c, 0)),
                  pl.BlockSpec(blk, lambda h, c: (A_HEADS + h, 0, prev(c), 0)),
                  pl.BlockSpec(blk, lambda h, c: (A_HEADS + h, 0, c, 0)),
                  pl.BlockSpec(blk, lambda h, c: (h, 0, prev(c), 0)),
                  pl.BlockSpec(blk, lambda h, c: (h, 0, c, 0))],
        out_specs=pl.BlockSpec((A_CHUNK, A_HEAD_DIM), lambda h, c: (c, h)),
        out_shape=jax.ShapeDtypeStruct((S, A_WIDTH), BF16),
        scratch_shapes=[pltpu.VMEM((R4, A_PLANE, A_HEAD_DIM), F32),
                        pltpu.VMEM((R4, 2 * A_PLANE, A_HEAD_DIM), F32),
                        pltpu.VMEM((R4, 2 * A_PLANE, A_HEAD_DIM), F32),
                        pltpu.VMEM((npat, R4, A_PLANE, A_HEAD_DIM), F32),
                        pltpu.VMEM((npat, R4, A_PLANE, A_HEAD_DIM), F32),
                        pltpu.VMEM((npat, R4, A_PLANE, A_HEAD_DIM), F32),
                        pltpu.VMEM((A_CHUNK, A_HEAD_DIM), F32)],
        compiler_params=_params(("parallel", "arbitrary")),
        name="attn",
    )(qk, qk, qk, v, v)


def _gla_kernel(q_ref, k_ref, v_ref, b_ref, r_ref, gn_ref, o_ref, st_ref, *, nchunk):
    C, Cs = GLA_CHUNK, GLA_SUB
    NS = C // Cs

    @pl.when(pl.program_id(0) == 0)
    def _():
        st_ref[...] = jnp.zeros_like(st_ref)

    row = lax.broadcasted_iota(jnp.int32, (C, C), 0)
    col = lax.broadcasted_iota(jnp.int32, (C, C), 1)
    sub_start = (row // Cs) * Cs
    sub_row = lax.broadcasted_iota(jnp.int32, (NS, Cs, B_KEY_DIM), 1)
    levels = []
    z = C // 2
    while z >= Cs:
        levels.append((z, jnp.logical_and((row // z) % 2 == 1, col // z == row // z - 1)))
        z //= 2
    diag_masks = [col == sub_start + j for j in range(Cs)]
    for c in range(nchunk):
        rows = slice(c * C, (c + 1) * C)
        for h in range(B_HEADS):
            kcols = slice(h * B_KEY_DIM, (h + 1) * B_KEY_DIM)
            vcols = slice(h * B_VAL_DIM, (h + 1) * B_VAL_DIM)
            q = q_ref[rows, kcols] * (B_KEY_DIM ** -0.5)
            k = k_ref[rows, kcols]
            b = b_ref[rows, kcols]
            v = v_ref[rows, vcols]
            b_last = b[C - 1:C, :]
            st = st_ref[h]
            o = _dot_nt((q * jnp.exp(b)).astype(BF16), st.astype(BF16))
            k_end = (k * jnp.exp(b_last - b)).astype(BF16)
            st_ref[h] = st * jnp.exp(b_last) + _dot_tn(v, k_end)

            a = jnp.zeros((C, C), F32)
            for z, mask in levels:
                bz = b.reshape(C // z, z, B_KEY_DIM)
                ends = bz[:, z - 1:z, :]
                starts = jnp.concatenate([jnp.zeros_like(ends[:1]), ends[:-1]], axis=0)
                q_z = (q * jnp.exp(bz - starts).reshape(C, B_KEY_DIM)).astype(BF16)
                k_z = (k * jnp.exp(ends - bz).reshape(C, B_KEY_DIM)).astype(BF16)
                a = jnp.where(mask, _dot_nt(q_z, k_z), a)

            q3 = q.reshape(NS, Cs, B_KEY_DIM)
            k3 = k.reshape(NS, Cs, B_KEY_DIM)
            b3 = b.reshape(NS, Cs, B_KEY_DIM)
            for j in range(Cs):
                kb = jnp.broadcast_to(k3[:, j:j + 1, :], q3.shape)
                bb = jnp.broadcast_to(b3[:, j:j + 1, :], q3.shape)
                w = jnp.exp(jnp.where(sub_row >= j, b3 - bb, -jnp.inf))
                dj = jnp.sum((q3 * kb * w).reshape(C, B_KEY_DIM), axis=-1, keepdims=True)
                a = jnp.where(diag_masks[j], dj, a)
            o = o + _dot(a.astype(BF16), v)
            r = r_ref[rows, vcols]
            o_ref[rows, vcols] = (_rms(o, gn_ref[...]) * (r * jax.nn.sigmoid(r))).astype(o_ref.dtype)


def _gla(pf, bv, b, gain, *, rb=512):
    S = pf.shape[0]
    return pl.pallas_call(
        functools.partial(_gla_kernel, nchunk=rb // GLA_CHUNK),
        grid=(S // rb,),
        in_specs=[pl.BlockSpec((rb, B_KEY_WIDTH), lambda i: (i, 0)),
                  pl.BlockSpec((rb, B_KEY_WIDTH), lambda i: (i, 1)),
                  pl.BlockSpec((rb, B_VAL_WIDTH), lambda i: (i, 0)),
                  pl.BlockSpec((rb, B_KEY_WIDTH), lambda i: (i, 0)),
                  pl.BlockSpec((rb, B_VAL_WIDTH), lambda i: (i, 1)),
                  pl.BlockSpec((1, B_VAL_DIM), lambda i: (0, 0))],
        out_specs=pl.BlockSpec((rb, B_VAL_WIDTH), lambda i: (i, 0)),
        out_shape=jax.ShapeDtypeStruct((S, B_VAL_WIDTH), BF16),
        scratch_shapes=[pltpu.VMEM((B_HEADS, B_VAL_DIM, B_KEY_DIM), F32)],
        compiler_params=_params(("arbitrary",)),
        name="gla",
    )(pf, pf, bv, b, pf, gain)


def _merge_out_kernel(x_ref, oa_ref, ob_ref, ga_ref, gb_ref, wa_ref, wb_ref, wo_ref, o_ref):
    j = pl.program_id(1)
    tn = ga_ref.shape[1]
    cols = pl.ds(pl.multiple_of(j * tn, tn), tn)

    @pl.when(j == 0)
    def _():
        o_ref[...] = x_ref[...]

    ya = _dot(oa_ref[...], wa_ref[:, cols])
    yb = _dot(ob_ref[...], wb_ref[:, cols])
    y = jax.nn.sigmoid(ga_ref[...]) * ya + jax.nn.sigmoid(gb_ref[...]) * yb
    o_ref[...] += _dot(y.astype(BF16), wo_ref[cols, :])


def _merge_out(x, oa, ob, pf, wa, wb, wo, *, tm=512, tn=1024):
    S, D = x.shape
    ga_blk = 2048 // tn
    gb_blk = 4096 // tn
    row = lambda i, j: (i, 0)
    resident = lambda w: pl.BlockSpec(w.shape, lambda i, j: (0, 0), pipeline_mode=pl.Buffered(1))
    return pl.pallas_call(
        _merge_out_kernel,
        grid=(S // tm, D // tn),
        in_specs=[pl.BlockSpec((tm, D), row),
                  pl.BlockSpec((tm, A_WIDTH), row),
                  pl.BlockSpec((tm, B_VAL_WIDTH), row),
                  pl.BlockSpec((tm, tn), lambda i, j: (i, ga_blk + j)),
                  pl.BlockSpec((tm, tn), lambda i, j: (i, gb_blk + j)),
                  resident(wa), resident(wb), resident(wo)],
        out_specs=pl.BlockSpec((tm, D), row),
        out_shape=jax.ShapeDtypeStruct((S, D), F32),
        compiler_params=_params(("parallel", "arbitrary")),
        name="merge_out",
    )(x, oa, ob, pf, pf, wa, wb, wo)


IN_COLS = np.cumsum([0, A_WIDTH, A_WIDTH, A_WIDTH, B_KEY_WIDTH, B_KEY_WIDTH, B_VAL_WIDTH, B_VAL_WIDTH,
                     GATE_RANK, D_MODEL, D_MODEL]).tolist()


def _w_in_kernel(w_ref, qk_ref, v_ref, f_ref, z_ref):
    c = IN_COLS

    def rows(lo, hi):
        return w_ref[lo:hi, :].astype(BF16)

    perm_runs = [(0, ROPE_HALF), (ROPE_DIM, ROPE_DIM + ROPE_GAP - ROPE_HALF), (ROPE_HALF, ROPE_DIM),
                 (ROPE_DIM + ROPE_GAP - ROPE_HALF, A_HEAD_DIM)]
    for h in range(2 * A_HEADS):
        dst = h * A_HEAD_DIM
        for lo, hi in perm_runs:
            qk_ref[dst:dst + hi - lo, :] = rows(h * A_HEAD_DIM + lo, h * A_HEAD_DIM + hi)
            dst += hi - lo
    v_ref[0:A_WIDTH, :] = rows(c[2], c[3])
    v_ref[A_WIDTH:, :] = rows(c[5], c[6])
    f_ref[0:2 * B_KEY_WIDTH, :] = rows(c[3], c[5])
    f_ref[2 * B_KEY_WIDTH:2 * B_KEY_WIDTH + B_VAL_WIDTH, :] = rows(c[6], c[7])
    f_ref[2 * B_KEY_WIDTH + B_VAL_WIDTH:, :] = rows(c[8], c[10])
    z_ref[0:GATE_RANK, :] = rows(c[7], c[8])
    z_ref[GATE_RANK:, :] = jnp.zeros((z_ref.shape[0] - GATE_RANK, z_ref.shape[1]), z_ref.dtype)


def _w_in_layout(w_t, l, *, tk=256):
    _, N, K = w_t.shape
    heights = (2 * A_WIDTH, A_WIDTH + B_VAL_WIDTH, 2 * B_KEY_WIDTH + B_VAL_WIDTH + 2 * D_MODEL, LANES)
    return pl.pallas_call(
        _w_in_kernel,
        grid=(K // tk,),
        in_specs=[pl.BlockSpec((None, N, tk), lambda i: (l, 0, i))],
        out_specs=[pl.BlockSpec((n, tk), lambda i: (0, i)) for n in heights],
        out_shape=[jax.ShapeDtypeStruct((n, K), BF16) for n in heights],
        compiler_params=_params(("parallel",)),
        name="w_in_layout",
    )(w_t)


def _layer(x, pos, p):
    x1, h = _ffn(x, p["ffn1_norm"], p["ffn1_wg"], p["ffn1_wu"], p["ffn1_wd"], p["mix_norm"], tf=256)
    qk = _proj_qk(h, p["w_qk"], p["qk_gain"], pos, p["rope_inv"])
    av, bv, b = _proj_v(h, p["w_v"], p["w_z"], p["w_2"], p["gate_bias"])
    pf = _proj(h, p["w_f"], F32)
    o_a = _attn(qk, av)
    o_b = _gla(pf, bv, b, p["b_out_norm"])
    x2 = _merge_out(x1, o_a, o_b, pf, p["w_a_up"], p["w_b_up"], p["w_out"])
    return _ffn(x2, p["ffn2_norm"], p["ffn2_wg"], p["ffn2_wu"], p["ffn2_wd"], tf=512)


def kernel(x, positions, ffn1_norm, ffn1_w_gate, ffn1_w_up, ffn1_w_down, mix_norm, w_in, a_q_norm, a_k_norm, b_gate_w2, b_gate_bias, b_out_norm, w_a_up, w_b_up, w_out, ffn2_norm, ffn2_w_gate, ffn2_w_up, ffn2_w_down):
    B, S, D = x.shape
    assert D == D_MODEL and S % A_CHUNK == 0
    depth = w_in.shape[0]
    w_in_t = jnp.swapaxes(w_in, 1, 2)
    inv = jnp.power(ROPE_THETA, -(jnp.arange(ROPE_HALF, dtype=F32) * 2.0 / ROPE_DIM))
    rope_inv = jnp.tile(inv, LANES // ROPE_HALF)[None, :]
    outs = []
    for bi in range(B):
        xb = x.reshape(S, D) if B == 1 else x[bi]
        pos = (positions if B == 1 else positions[bi]).reshape(S // R4, R4)
        for l in range(depth):
            w_qk, w_v, w_f, w_z = _w_in_layout(w_in_t, l)
            p = {
                "ffn1_norm": ffn1_norm[l][None, :], "mix_norm": mix_norm[l][None, :],
                "ffn2_norm": ffn2_norm[l][None, :],
                "ffn1_wg": ffn1_w_gate[l].astype(BF16), "ffn1_wu": ffn1_w_up[l].astype(BF16),
                "ffn1_wd": ffn1_w_down[l].astype(BF16),
                "ffn2_wg": ffn2_w_gate[l].astype(BF16), "ffn2_wu": ffn2_w_up[l].astype(BF16),
                "ffn2_wd": ffn2_w_down[l].astype(BF16),
                "w_qk": w_qk, "w_v": w_v, "w_f": w_f, "w_z": w_z,
                "w_2": jnp.pad(b_gate_w2[l], ((0, LANES - GATE_RANK), (0, 0))).astype(BF16),
                "gate_bias": b_gate_bias[l][None, :],
                "qk_gain": jnp.concatenate([jnp.tile(a_q_norm[l][HEAD_PERM], A_HEADS),
                                            jnp.tile(a_k_norm[l][HEAD_PERM], A_HEADS)])[None, :],
                "rope_inv": rope_inv,
                "b_out_norm": b_out_norm[l][None, :],
                "w_a_up": w_a_up[l].astype(BF16), "w_b_up": w_b_up[l].astype(BF16), "w_out": w_out[l].astype(BF16),
            }
            xb = _layer(xb, pos, p)
        outs.append(xb)
    return outs[0].reshape(B, S, D) if B == 1 else jnp.stack(outs, axis=0)
```

```python
import functools

import jax
import jax.numpy as jnp
import numpy as np
from jax import lax
from jax.experimental import pallas as pl
from jax.experimental.pallas import tpu as pltpu

F32 = jnp.float32
BF16 = jnp.bfloat16

D_MODEL = 2048
D_FF = 5632
RMS_EPS = 1e-6
ROPE_THETA = 500000.0
A_HEAD_DIM = 128
A_HEADS = 8
A_WIDTH = A_HEADS * A_HEAD_DIM
ROPE_DIM = A_HEAD_DIM // 4
ROPE_HALF = ROPE_DIM // 2
DILATIONS = (1, 4, 16)
A_SPAN = 128
R4 = 4
B_HEADS = 4
B_VAL_DIM = 256
B_KEY_DIM = 128
B_KEY_WIDTH = B_HEADS * B_KEY_DIM
B_VAL_WIDTH = B_HEADS * B_VAL_DIM
GATE_RANK = 16
GATE_NORMALIZER = 16.0
GLA_CHUNK = 64
GLA_SUB = 8
LANES = 128
MXU_N = 256

VMEM_LIMIT = 56 * 1024 * 1024


def _params(sem):
    return pltpu.CompilerParams(dimension_semantics=sem, vmem_limit_bytes=VMEM_LIMIT)


def _rms(x, g):
    return x * lax.rsqrt(jnp.mean(x * x, axis=-1, keepdims=True) + RMS_EPS) * g


def _dot(a, b):
    return jnp.dot(a, b, preferred_element_type=F32)


def _dot_nt(a, b):
    return lax.dot_general(a, b, (((1,), (1,)), ((), ())), preferred_element_type=F32)


def _dot_tn(a, b):
    return lax.dot_general(a, b, (((0,), (0,)), ((), ())), preferred_element_type=F32)


def _ffn_kernel(x_ref, g_ref, wg_ref, wu_ref, wd_ref, *rest, emit_next):
    if emit_next:
        gn_ref, o_ref, hn_ref, h_scr = rest
    else:
        o_ref, h_scr = rest
    f = pl.program_id(1)

    @pl.when(f == 0)
    def _():
        x = x_ref[...]
        h_scr[...] = _rms(x, g_ref[...]).astype(BF16)
        o_ref[...] = x

    h = h_scr[...]
    gate = _dot(h, wg_ref[...])
    up = _dot(h, wu_ref[...])
    act = (0.5 * (gate * jax.nn.sigmoid(gate)) * up).astype(BF16)
    o_ref[...] += _dot(act, wd_ref[...])

    if emit_next:
        @pl.when(f == pl.num_programs(1) - 1)
        def _():
            hn_ref[...] = _rms(o_ref[...], gn_ref[...]).astype(BF16)


def _ffn(x, gain, wg, wu, wd, next_gain=None, *, tm=1024, tf=256):
    S, D = x.shape
    F = wg.shape[1]
    emit_next = next_gain is not None
    row = lambda i, f: (i, 0)
    fixed = lambda i, f: (0, 0)
    in_specs = [
        pl.BlockSpec((tm, D), row),
        pl.BlockSpec((1, D), fixed),
        pl.BlockSpec((D, tf), lambda i, f: (0, f)),
        pl.BlockSpec((D, tf), lambda i, f: (0, f)),
        pl.BlockSpec((tf, D), lambda i, f: (f, 0)),
    ]
    args = [x, gain, wg, wu, wd]
    out_shape = [jax.ShapeDtypeStruct((S, D), F32)]
    out_specs = [pl.BlockSpec((tm, D), row)]
    if emit_next:
        in_specs.append(pl.BlockSpec((1, D), fixed))
        args.append(next_gain)
        out_shape.append(jax.ShapeDtypeStruct((S, D), BF16))
        out_specs.append(pl.BlockSpec((tm, D), row))
    res = pl.pallas_call(
        functools.partial(_ffn_kernel, emit_next=emit_next),
        grid=(S // tm, F // tf),
        in_specs=in_specs,
        out_specs=out_specs,
        out_shape=out_shape,
        scratch_shapes=[pltpu.VMEM((tm, D), BF16)],
        compiler_params=_params(("parallel", "arbitrary")),
        name="ffn_next" if emit_next else "ffn",
    )(*args)
    return res if emit_next else res[0]


def _proj_kernel(a_ref, w_ref, o_ref):
    o_ref[...] = _dot_nt(a_ref[...], w_ref[...]).astype(o_ref.dtype)


def _proj(a, w, out_dtype, *, tm=1024, tn=1024):
    S, K = a.shape
    N = w.shape[0]
    return pl.pallas_call(
        _proj_kernel,
        grid=(S // tm, N // tn),
        in_specs=[pl.BlockSpec((tm, K), lambda i, j: (i, 0)),
                  pl.BlockSpec((tn, K), lambda i, j: (j, 0))],
        out_specs=pl.BlockSpec((tm, tn), lambda i, j: (i, j)),
        out_shape=jax.ShapeDtypeStruct((S, N), out_dtype),
        compiler_params=_params(("parallel", "arbitrary")),
        name="proj_" + jnp.dtype(out_dtype).name,
    )(a, w)


ROPE_GAP = A_HEAD_DIM // 2
HEAD_PERM = np.concatenate([np.arange(0, ROPE_HALF), np.arange(ROPE_DIM, ROPE_DIM + ROPE_GAP - ROPE_HALF),
                            np.arange(ROPE_HALF, ROPE_DIM), np.arange(ROPE_DIM + ROPE_GAP - ROPE_HALF, A_HEAD_DIM)])


def _proj_qk_kernel(a_ref, w_ref, g_ref, pos_ref, inv_ref, o_ref, raw):
    i = pl.program_id(0)
    nh = o_ref.shape[0]

    @pl.when(i == 0)
    def _():
        raw[1] = jnp.zeros(raw.shape[1:], raw.dtype)

    cur = i % 2
    acc = _dot_nt(a_ref[...], w_ref[...])
    for h in range(nh):
        raw[cur, h] = acc[:, h * A_HEAD_DIM:(h + 1) * A_HEAD_DIM]

    group = LANES // R4
    p4 = pos_ref[...].astype(F32)
    tq = p4.shape[0]
    lane = lax.broadcasted_iota(jnp.int32, (tq, LANES), 1)
    pos_lanes = p4[:, R4 - 1:R4]
    for r in reversed(range(R4 - 1)):
        pos_lanes = jnp.where(lane < (r + 1) * group, p4[:, r:r + 1], pos_lanes)
    ang = pos_lanes * inv_ref[...]
    cos_all = jnp.cos(ang)
    sin_all = jnp.sin(ang)
    lo = lane < ROPE_HALF
    hi = jnp.logical_and(lane >= ROPE_GAP, lane < ROPE_GAP + ROPE_HALF)

    def to_head_lanes(t, r):
        shifts = ((LANES - r * group) % LANES, (ROPE_GAP - r * group) % LANES)
        return [pltpu.roll(t, s, 1) if s else t for s in shifts]

    prev = raw.at[1 - cur]
    gain = jnp.stack([g_ref[:, h * A_HEAD_DIM:(h + 1) * A_HEAD_DIM] for h in range(nh)], axis=0)
    for r in range(R4):
        rows = pl.ds(r, tq, stride=R4)
        c1, c2 = to_head_lanes(cos_all, r)
        s1, s2 = to_head_lanes(sin_all, r)
        cosf = jnp.where(lo, c1, jnp.where(hi, c2, 1.0))
        sinf = jnp.where(lo, -s1, jnp.where(hi, s2, 0.0))
        y = _rms(jnp.stack([prev[h, rows, :] for h in range(nh)], axis=0), gain)
        o_ref[:, r] = (y * cosf[None] + pltpu.roll(y, ROPE_GAP, 2) * sinf[None]).astype(o_ref.dtype)


def _proj_qk(a, w, gains, pos4, inv, *, tm=1024):
    S, K = a.shape
    N = w.shape[0]
    nt = S // tm
    lag = lambda i: jnp.maximum(i - 1, 0)
    return pl.pallas_call(
        _proj_qk_kernel,
        grid=(nt + 1,),
        in_specs=[pl.BlockSpec((tm, K), lambda i: (jnp.minimum(i, nt - 1), 0)),
                  pl.BlockSpec((N, K), lambda i: (0, 0), pipeline_mode=pl.Buffered(1)),
                  pl.BlockSpec((1, N), lambda i: (0, 0)),
                  pl.BlockSpec((tm // R4, R4), lambda i: (lag(i), 0)),
                  pl.BlockSpec((1, LANES), lambda i: (0, 0))],
        out_specs=pl.BlockSpec((N // A_HEAD_DIM, R4, tm // R4, A_HEAD_DIM), lambda i: (0, 0, lag(i), 0)),
        out_shape=jax.ShapeDtypeStruct((N // A_HEAD_DIM, R4, S // R4, A_HEAD_DIM), BF16),
        scratch_shapes=[pltpu.VMEM((2, N // A_HEAD_DIM, tm, A_HEAD_DIM), F32)],
        compiler_params=_params(("arbitrary",)),
        name="proj_qk",
    )(a, w, gains, pos4, inv)


def _chunk_log_decay(z, w2, bias):
    pre = _dot(z.astype(BF16), w2) + bias
    g = (jnp.minimum(pre, 0.0) - jnp.log(1.0 + jnp.exp(-jnp.abs(pre)))) / GATE_NORMALIZER
    r = lax.broadcasted_iota(jnp.int32, g.shape, 0) % GLA_CHUNK
    shift = 1
    while shift < GLA_CHUNK:
        g = g + jnp.where(r >= shift, pltpu.roll(g, shift, 0), 0.0)
        shift *= 2
    return g


def _proj_v_kernel(a_ref, w_ref, wz_ref, w2_ref, bias_ref, oa_ref, ob_ref, b_ref, raw):
    a = a_ref[...]
    b_ref[...] = _chunk_log_decay(_dot_nt(a, wz_ref[...]), w2_ref[...], bias_ref[...])
    heads_per_dot = MXU_N // A_HEAD_DIM
    for g in range(A_WIDTH // MXU_N):
        acc = _dot_nt(a, w_ref[g * MXU_N:(g + 1) * MXU_N, :])
        for hh in range(heads_per_dot):
            raw[g * heads_per_dot + hh] = acc[:, hh * A_HEAD_DIM:(hh + 1) * A_HEAD_DIM]
    for g in range(B_VAL_WIDTH // MXU_N):
        cols = slice(g * MXU_N, (g + 1) * MXU_N)
        ob_ref[:, cols] = _dot_nt(a, w_ref[A_WIDTH + g * MXU_N:A_WIDTH + (g + 1) * MXU_N, :]).astype(ob_ref.dtype)
    tq = raw.shape[1] // R4
    for h in range(A_HEADS):
        for r in range(R4):
            oa_ref[h, r] = raw[h, pl.ds(r, tq, stride=R4), :].astype(oa_ref.dtype)


def _proj_v(a, w, wz, w2, bias, *, tm=1024):
    S, K = a.shape
    fixed = lambda t: pl.BlockSpec(t.shape, lambda i: (0, 0))
    return pl.pallas_call(
        _proj_v_kernel,
        grid=(S // tm,),
        in_specs=[pl.BlockSpec((tm, K), lambda i: (i, 0)), fixed(w), fixed(wz), fixed(w2), fixed(bias)],
        out_specs=[pl.BlockSpec((A_HEADS, R4, tm // R4, A_HEAD_DIM), lambda i: (0, 0, i, 0)),
                   pl.BlockSpec((tm, B_VAL_WIDTH), lambda i: (i, 0)),
                   pl.BlockSpec((tm, B_KEY_WIDTH), lambda i: (i, 0))],
        out_shape=[jax.ShapeDtypeStruct((A_HEADS, R4, S // R4, A_HEAD_DIM), BF16),
                   jax.ShapeDtypeStruct((S, B_VAL_WIDTH), BF16),
                   jax.ShapeDtypeStruct((S, B_KEY_WIDTH), F32)],
        scratch_shapes=[pltpu.VMEM((A_HEADS, tm, A_HEAD_DIM), F32)],
        compiler_params=_params(("parallel",)),
        name="proj_v",
    )(a, w, wz, w2, bias)


A_BLK = 128
A_CHUNK = A_BLK * max(DILATIONS)
A_PLANE = A_CHUNK // R4


def _attn_kernel(q_ref, kp_ref, kc_ref, vp_ref, vc_ref, o_ref, qf, kf, vf, acc_s, m_s, l_s, o_nat, *, group):
    c = pl.program_id(1)
    first_chunk = (c == 0).astype(jnp.int32)
    scale = A_HEAD_DIM ** -0.5
    row = lax.broadcasted_iota(jnp.int32, (1, A_BLK, 2 * A_BLK), 1)
    col = lax.broadcasted_iota(jnp.int32, (1, A_BLK, 2 * A_BLK), 2)
    dist_strided = row + A_SPAN - col
    prev_strided = col < A_BLK
    qn, kn = A_BLK // R4, 2 * A_BLK // R4
    dist_mixed = (R4 * (row % qn) + row // qn) - (R4 * (col % kn) + col // kn) + A_SPAN
    prev_mixed = (col % kn) < kn // 2

    def rows_of(prev_ref, cur_ref, plane, lo, n_rows):
        if lo >= 0:
            return cur_ref[plane, lo:lo + n_rows, :]
        return jnp.concatenate([prev_ref[plane, A_PLANE + lo:A_PLANE, :], cur_ref[plane, 0:lo + n_rows, :]], axis=0)

    def run_group(mixed, tiles):
        dist = dist_mixed if mixed else dist_strided
        band = jnp.logical_and(dist >= 0, dist <= A_SPAN)
        s = jnp.concatenate([_dot_nt(t[0], t[1]) for t in tiles], axis=0).reshape(len(tiles), A_BLK, 2 * A_BLK)
        s = jnp.where(band, s * scale, -jnp.inf)
        firsts = [gi for gi, t in enumerate(tiles) if t[3]]
        if firsts:
            gidx = lax.broadcasted_iota(jnp.int32, (len(tiles), 1, 1), 0)
            is_first = functools.reduce(jnp.logical_or, [gidx == gi for gi in firsts]).astype(jnp.int32) * first_chunk
            s = jnp.where(jnp.logical_and(prev_mixed if mixed else prev_strided, is_first > 0), -jnp.inf, s)
        m = jnp.max(s, axis=-1, keepdims=True)
        p = jnp.exp(s - m)
        l = jnp.sum(p, axis=-1, keepdims=True)
        pb = p.astype(BF16)
        mb = jnp.broadcast_to(m, (len(tiles), A_BLK, A_HEAD_DIM))
        lb = jnp.broadcast_to(l, (len(tiles), A_BLK, A_HEAD_DIM))
        for gi, t in enumerate(tiles):
            t[4](_dot(pb[gi], t[2]), mb[gi], lb[gi])

    def grouped(mixed, tiles):
        for g0 in range(0, len(tiles), group):
            run_group(mixed, tiles[g0:g0 + group])

    tiles = []
    for n in range(A_CHUNK // A_BLK):
        q = jnp.concatenate([q_ref[r, qn * n:qn * (n + 1), :] for r in range(R4)], axis=0)
        k = jnp.concatenate([rows_of(kp_ref, kc_ref, r, qn * (n - 1), kn) for r in range(R4)], axis=0)
        v = jnp.concatenate([rows_of(vp_ref, vc_ref, r, qn * (n - 1), kn) for r in range(R4)], axis=0)

        def store(acc, mb, lb, n=n):
            for r in range(R4):
                dst = slice(qn * n, qn * (n + 1))
                src = slice(qn * r, qn * (r + 1))
                acc_s[0, r, dst, :] = acc[src]
                m_s[0, r, dst, :] = mb[src]
                l_s[0, r, dst, :] = lb[src]
        tiles.append((q, k, v, n == 0, store))
    grouped(True, tiles)

    tiles = []
    for r in range(R4):
        for n in range(A_PLANE // A_BLK):
            q = q_ref[r, A_BLK * n:A_BLK * (n + 1), :]
            k = rows_of(kp_ref, kc_ref, r, A_BLK * (n - 1), 2 * A_BLK)
            v = rows_of(vp_ref, vc_ref, r, A_BLK * (n - 1), 2 * A_BLK)

            def store(acc, mb, lb, r=r, n=n):
                dst = slice(A_BLK * n, A_BLK * (n + 1))
                acc_s[1, r, dst, :] = acc
                m_s[1, r, dst, :] = mb
                l_s[1, r, dst, :] = lb
            tiles.append((q, k, v, n == 0, store))
    grouped(False, tiles)

    qf[...] = q_ref[...].astype(F32)
    kf[:, 0:A_PLANE, :] = kp_ref[...].astype(F32)
    kf[:, A_PLANE:2 * A_PLANE, :] = kc_ref[...].astype(F32)
    vf[:, 0:A_PLANE, :] = vp_ref[...].astype(F32)
    vf[:, A_PLANE:2 * A_PLANE, :] = vc_ref[...].astype(F32)
    tiles = []
    for r in range(R4):
        for g in range(R4):
            own = pl.ds(g, A_BLK, stride=R4)
            both = pl.ds(g, 2 * A_BLK, stride=R4)

            def store(acc, mb, lb, r=r, own=own):
                acc_s[2, r, own, :] = acc
                m_s[2, r, own, :] = mb
                l_s[2, r, own, :] = lb
            tiles.append((qf[r, own, :].astype(BF16), kf[r, both, :].astype(BF16), vf[r, both, :].astype(BF16),
                          True, store))
    grouped(False, tiles)

    ms = [m_s[pi] for pi in range(len(DILATIONS))]
    m = functools.reduce(jnp.maximum, ms)
    ws = [jnp.exp(mi - m) for mi in ms]
    num = sum(w * acc_s[pi] for pi, w in enumerate(ws))
    den = sum(w * l_s[pi] for pi, w in enumerate(ws))
    o = num / den
    for r in range(R4):
        o_nat[pl.ds(r, A_PLANE, stride=R4), :] = o[r]
    o_ref[...] = o_nat[...].astype(o_ref.dtype)


def _attn(qk, v, *, group=8):
    S = v.shape[2] * R4
    npat = len(DILATIONS)
    blk = (None, R4, A_PLANE, A_HEAD_DIM)
    prev = lambda c: jnp.maximum(c - 1, 0)
    return pl.pallas_call(
        functools.partial(_attn_kernel, group=group),
        grid=(A_HEADS, S // A_CHUNK),
        in_specs=[pl.BlockSpec(blk, lambda h, c: (h, 0, c, 0)),
                  pl.BlockSpec(blk, lambda h, c: (A_HEADS + h, 0, prev(c), 0)),
                  pl.BlockSpec(blk, lambda h, c: (A_HEADS + h, 0, c, 0)),
                  pl.BlockSpec(blk, lambda h, c: (h, 0, prev(c), 0)),
                  pl.BlockSpec(blk, lambda h, c: (h, 0, c, 0))],
        out_specs=pl.BlockSpec((A_CHUNK, A_HEAD_DIM), lambda h, c: (c, h)),
        out_shape=jax.ShapeDtypeStruct((S, A_WIDTH), BF16),
        scratch_shapes=[pltpu.VMEM((R4, A_PLANE, A_HEAD_DIM), F32),
                        pltpu.VMEM((R4, 2 * A_PLANE, A_HEAD_DIM), F32),
                        pltpu.VMEM((R4, 2 * A_PLANE, A_HEAD_DIM), F32),
                        pltpu.VMEM((npat, R4, A_PLANE, A_HEAD_DIM), F32),
                        pltpu.VMEM((npat, R4, A_PLANE, A_HEAD_DIM), F32),
                        pltpu.VMEM((npat, R4, A_PLANE, A_HEAD_DIM), F32),
                        pltpu.VMEM((A_CHUNK, A_HEAD_DIM), F32)],
        compiler_params=_params(("parallel", "arbitrary")),
        name="attn",
    )(qk, qk, qk, v, v)


def _gla_kernel(q_ref, k_ref, v_ref, b_ref, r_ref, gn_ref, o_ref, st_ref, *, nchunk):
    C, Cs = GLA_CHUNK, GLA_SUB
    NS = C // Cs

    @pl.when(pl.program_id(0) == 0)
    def _():
        st_ref[...] = jnp.zeros_like(st_ref)

    row = lax.broadcasted_iota(jnp.int32, (C, C), 0)
    col = lax.broadcasted_iota(jnp.int32, (C, C), 1)
    sub_start = (row // Cs) * Cs
    sub_row = lax.broadcasted_iota(jnp.int32, (NS, Cs, B_KEY_DIM), 1)
    levels = []
    z = C // 2
    while z >= Cs:
        levels.append((z, jnp.logical_and((row // z) % 2 == 1, col // z == row // z - 1)))
        z //= 2
    diag_masks = [col == sub_start + j for j in range(Cs)]
    for c in range(nchunk):
        rows = slice(c * C, (c + 1) * C)
        for h in range(B_HEADS):
            kcols = slice(h * B_KEY_DIM, (h + 1) * B_KEY_DIM)
            vcols = slice(h * B_VAL_DIM, (h + 1) * B_VAL_DIM)
            q = q_ref[rows, kcols] * (B_KEY_DIM ** -0.5)
            k = k_ref[rows, kcols]
            b = b_ref[rows, kcols]
            v = v_ref[rows, vcols]
            b_last = b[C - 1:C, :]
            st = st_ref[h]
            o = _dot_nt((q * jnp.exp(b)).astype(BF16), st.astype(BF16))
            k_end = (k * jnp.exp(b_last - b)).astype(BF16)
            st_ref[h] = st * jnp.exp(b_last) + _dot_tn(v, k_end)

            a = jnp.zeros((C, C), F32)
            for z, mask in levels:
                bz = b.reshape(C // z, z, B_KEY_DIM)
                ends = bz[:, z - 1:z, :]
                starts = jnp.concatenate([jnp.zeros_like(ends[:1]), ends[:-1]], axis=0)
                q_z = (q * jnp.exp(bz - starts).reshape(C, B_KEY_DIM)).astype(BF16)
                k_z = (k * jnp.exp(ends - bz).reshape(C, B_KEY_DIM)).astype(BF16)
                a = jnp.where(mask, _dot_nt(q_z, k_z), a)

            q3 = q.reshape(NS, Cs, B_KEY_DIM)
            k3 = k.reshape(NS, Cs, B_KEY_DIM)
            b3 = b.reshape(NS, Cs, B_KEY_DIM)
            for j in range(Cs):
                kb = jnp.broadcast_to(k3[:, j:j + 1, :], q3.shape)
                bb = jnp.broadcast_to(b3[:, j:j + 1, :], q3.shape)
                w = jnp.exp(jnp.where(sub_row >= j, b3 - bb, -jnp.inf))
                dj = jnp.sum((q3 * kb * w).reshape(C, B_KEY_DIM), axis=-1, keepdims=True)
                a = jnp.where(diag_masks[j], dj, a)
            o = o + _dot(a.astype(BF16), v)
            r = r_ref[rows, vcols]
            o_ref[rows, vcols] = (_rms(o, gn_ref[...]) * (r * jax.nn.sigmoid(r))).astype(o_ref.dtype)


def _gla(pf, bv, b, gain, *, rb=512):
    S = pf.shape[0]
    return pl.pallas_call(
        functools.partial(_gla_kernel, nchunk=rb // GLA_CHUNK),
        grid=(S // rb,),
        in_specs=[pl.BlockSpec((rb, B_KEY_WIDTH), lambda i: (i, 0)),
                  pl.BlockSpec((rb, B_KEY_WIDTH), lambda i: (i, 1)),
                  pl.BlockSpec((rb, B_VAL_WIDTH), lambda i: (i, 0)),
                  pl.BlockSpec((rb, B_KEY_WIDTH), lambda i: (i, 0)),
                  pl.BlockSpec((rb, B_VAL_WIDTH), lambda i: (i, 1)),
                  pl.BlockSpec((1, B_VAL_DIM), lambda i: (0, 0))],
        out_specs=pl.BlockSpec((rb, B_VAL_WIDTH), lambda i: (i, 0)),
        out_shape=jax.ShapeDtypeStruct((S, B_VAL_WIDTH), BF16),
        scratch_shapes=[pltpu.VMEM((B_HEADS, B_VAL_DIM, B_KEY_DIM), F32)],
        compiler_params=_params(("arbitrary",)),
        name="gla",
    )(pf, pf, bv, b, pf, gain)


def _merge_out_kernel(x_ref, oa_ref, ob_ref, ga_ref, gb_ref, wa_ref, wb_ref, wo_ref, o_ref):
    j = pl.program_id(1)
    tn = ga_ref.shape[1]
    cols = pl.ds(pl.multiple_of(j * tn, tn), tn)

    @pl.when(j == 0)
    def _():
        o_ref[...] = x_ref[...]

    ya = _dot(oa_ref[...], wa_ref[:, cols])
    yb = _dot(ob_ref[...], wb_ref[:, cols])
    y = jax.nn.sigmoid(ga_ref[...]) * ya + jax.nn.sigmoid(gb_ref[...]) * yb
    o_ref[...] += _dot(y.astype(BF16), wo_ref[cols, :])


def _merge_out(x, oa, ob, pf, wa, wb, wo, *, tm=512, tn=1024):
    S, D = x.shape
    ga_blk = 2048 // tn
    gb_blk = 4096 // tn
    row = lambda i, j: (i, 0)
    resident = lambda w: pl.BlockSpec(w.shape, lambda i, j: (0, 0), pipeline_mode=pl.Buffered(1))
    return pl.pallas_call(
        _merge_out_kernel,
        grid=(S // tm, D // tn),
        in_specs=[pl.BlockSpec((tm, D), row),
                  pl.BlockSpec((tm, A_WIDTH), row),
                  pl.BlockSpec((tm, B_VAL_WIDTH), row),
                  pl.BlockSpec((tm, tn), lambda i, j: (i, ga_blk + j)),
                  pl.BlockSpec((tm, tn), lambda i, j: (i, gb_blk + j)),
                  resident(wa), resident(wb), resident(wo)],
        out_specs=pl.BlockSpec((tm, D), row),
        out_shape=jax.ShapeDtypeStruct((S, D), F32),
        compiler_params=_params(("parallel", "arbitrary")),
        name="merge_out",
    )(x, oa, ob, pf, pf, wa, wb, wo)


IN_COLS = np.cumsum([0, A_WIDTH, A_WIDTH, A_WIDTH, B_KEY_WIDTH, B_KEY_WIDTH, B_VAL_WIDTH, B_VAL_WIDTH,
                     GATE_RANK, D_MODEL, D_MODEL]).tolist()


def _w_in_kernel(w_ref, qk_ref, v_ref, f_ref, z_ref):
    c = IN_COLS

    def rows(lo, hi):
        return w_ref[lo:hi, :].astype(BF16)

    perm_runs = [(0, ROPE_HALF), (ROPE_DIM, ROPE_DIM + ROPE_GAP - ROPE_HALF), (ROPE_HALF, ROPE_DIM),
                 (ROPE_DIM + ROPE_GAP - ROPE_HALF, A_HEAD_DIM)]
    for h in range(2 * A_HEADS):
        dst = h * A_HEAD_DIM
        for lo, hi in perm_runs:
            qk_ref[dst:dst + hi - lo, :] = rows(h * A_HEAD_DIM + lo, h * A_HEAD_DIM + hi)
            dst += hi - lo
    v_ref[0:A_WIDTH, :] = rows(c[2], c[3])
    v_ref[A_WIDTH:, :] = rows(c[5], c[6])
    f_ref[0:2 * B_KEY_WIDTH, :] = rows(c[3], c[5])
    f_ref[2 * B_KEY_WIDTH:2 * B_KEY_WIDTH + B_VAL_WIDTH, :] = rows(c[6], c[7])
    f_ref[2 * B_KEY_WIDTH + B_VAL_WIDTH:, :] = rows(c[8], c[10])
    z_ref[0:GATE_RANK, :] = rows(c[7], c[8])
    z_ref[GATE_RANK:, :] = jnp.zeros((z_ref.shape[0] - GATE_RANK, z_ref.shape[1]), z_ref.dtype)


def _w_in_layout(w_t, l, *, tk=256):
    _, N, K = w_t.shape
    heights = (2 * A_WIDTH, A_WIDTH + B_VAL_WIDTH, 2 * B_KEY_WIDTH + B_VAL_WIDTH + 2 * D_MODEL, LANES)
    return pl.pallas_call(
        _w_in_kernel,
        grid=(K // tk,),
        in_specs=[pl.BlockSpec((None, N, tk), lambda i: (l, 0, i))],
        out_specs=[pl.BlockSpec((n, tk), lambda i: (0, i)) for n in heights],
        out_shape=[jax.ShapeDtypeStruct((n, K), BF16) for n in heights],
        compiler_params=_params(("parallel",)),
        name="w_in_layout",
    )(w_t)


def _layer(x, pos, p):
    x1, h = _ffn(x, p["ffn1_norm"], p["ffn1_wg"], p["ffn1_wu"], p["ffn1_wd"], p["mix_norm"], tf=256)
    qk = _proj_qk(h, p["w_qk"], p["qk_gain"], pos, p["rope_inv"])
    av, bv, b = _proj_v(h, p["w_v"], p["w_z"], p["w_2"], p["gate_bias"])
    pf = _proj(h, p["w_f"], F32)
    o_a = _attn(qk, av)
    o_b = _gla(pf, bv, b, p["b_out_norm"])
    x2 = _merge_out(x1, o_a, o_b, pf, p["w_a_up"], p["w_b_up"], p["w_out"])
    return _ffn(x2, p["ffn2_norm"], p["ffn2_wg"], p["ffn2_wu"], p["ffn2_wd"], tf=512)


def kernel(x, positions, ffn1_norm, ffn1_w_gate, ffn1_w_up, ffn1_w_down, mix_norm, w_in, a_q_norm, a_k_norm, b_gate_w2, b_gate_bias, b_out_norm, w_a_up, w_b_up, w_out, ffn2_norm, ffn2_w_gate, ffn2_w_up, ffn2_w_down):
    B, S, D = x.shape
    assert D == D_MODEL and S % A_CHUNK == 0
    depth = w_in.shape[0]
    w_in_t = jnp.swapaxes(w_in, 1, 2)
    inv = jnp.power(ROPE_THETA, -(jnp.arange(ROPE_HALF, dtype=F32) * 2.0 / ROPE_DIM))
    rope_inv = jnp.tile(inv, LANES // ROPE_HALF)[None, :]
    outs = []
    for bi in range(B):
        xb = x.reshape(S, D) if B == 1 else x[bi]
        pos = (positions if B == 1 else positions[bi]).reshape(S // R4, R4)
        for l in range(depth):
            w_qk, w_v, w_f, w_z = _w_in_layout(w_in_t, l)
            p = {
                "ffn1_norm": ffn1_norm[l][None, :], "mix_norm": mix_norm[l][None, :],
                "ffn2_norm": ffn2_norm[l][None, :],
                "ffn1_wg": ffn1_w_gate[l].astype(BF16), "ffn1_wu": ffn1_w_up[l].astype(BF16),
                "ffn1_wd": ffn1_w_down[l].astype(BF16),
                "ffn2_wg": ffn2_w_gate[l].astype(BF16), "ffn2_wu": ffn2_w_up[l].astype(BF16),
                "ffn2_wd": ffn2_w_down[l].astype(BF16),
                "w_qk": w_qk, "w_v": w_v, "w_f": w_f, "w_z": w_z,
                "w_2": jnp.pad(b_gate_w2[l], ((0, LANES - GATE_RANK), (0, 0))).astype(BF16),
                "gate_bias": b_gate_bias[l][None, :],
                "qk_gain": jnp.concatenate([jnp.tile(a_q_norm[l][HEAD_PERM], A_HEADS),
                                            jnp.tile(a_k_norm[l][HEAD_PERM], A_HEADS)])[None, :],
                "rope_inv": rope_inv,
                "b_out_norm": b_out_norm[l][None, :],
                "w_a_up": w_a_up[l].astype(BF16), "w_b_up": w_b_up[l].astype(BF16), "w_out": w_out[l].astype(BF16),
            }
            xb = _layer(xb, pos, p)
        outs.append(xb)
    return outs[0].reshape(B, S, D) if B == 1 else jnp.stack(outs, axis=0)
```

```python
import functools

import jax
import jax.numpy as jnp
import numpy as np
from jax import lax
from jax.experimental import pallas as pl
from jax.experimental.pallas import tpu as pltpu

F32 = jnp.float32
BF16 = jnp.bfloat16

D_MODEL = 2048
D_FF = 5632
RMS_EPS = 1e-6
ROPE_THETA = 500000.0
A_HEAD_DIM = 128
A_HEADS = 8
A_WIDTH = A_HEADS * A_HEAD_DIM
ROPE_DIM = A_HEAD_DIM // 4
ROPE_HALF = ROPE_DIM // 2
DILATIONS = (1, 4, 16)
A_SPAN = 128
R4 = 4
B_HEADS = 4
B_VAL_DIM = 256
B_KEY_DIM = 128
B_KEY_WIDTH = B_HEADS * B_KEY_DIM
B_VAL_WIDTH = B_HEADS * B_VAL_DIM
GATE_RANK = 16
GATE_NORMALIZER = 16.0
GLA_CHUNK = 64
GLA_SUB = 8
LANES = 128
MXU_N = 256

VMEM_LIMIT = 56 * 1024 * 1024


def _params(sem):
    return pltpu.CompilerParams(dimension_semantics=sem, vmem_limit_bytes=VMEM_LIMIT)


def _rms(x, g):
    return x * lax.rsqrt(jnp.mean(x * x, axis=-1, keepdims=True) + RMS_EPS) * g


def _dot(a, b):
    return jnp.dot(a, b, preferred_element_type=F32)


def _dot_nt(a, b):
    return lax.dot_general(a, b, (((1,), (1,)), ((), ())), preferred_element_type=F32)


def _dot_tn(a, b):
    return lax.dot_general(a, b, (((0,), (0,)), ((), ())), preferred_element_type=F32)


def _ffn_kernel(x_ref, g_ref, wg_ref, wu_ref, wd_ref, *rest, emit_next):
    if emit_next:
        gn_ref, o_ref, hn_ref, h_scr = rest
    else:
        o_ref, h_scr = rest
    f = pl.program_id(1)

    @pl.when(f == 0)
    def _():
        x = x_ref[...]
        h_scr[...] = _rms(x, g_ref[...]).astype(BF16)
        o_ref[...] = x

    h = h_scr[...]
    gate = _dot(h, wg_ref[...])
    up = _dot(h, wu_ref[...])
    act = (0.5 * (gate * jax.nn.sigmoid(gate)) * up).astype(BF16)
    o_ref[...] += _dot(act, wd_ref[...].astype(BF16))

    if emit_next:
        @pl.when(f == pl.num_programs(1) - 1)
        def _():
            hn_ref[...] = _rms(o_ref[...], gn_ref[...]).astype(BF16)


def _ffn(x, gain, wg, wu, wd, next_gain=None, *, tm=1024, tf=256):
    S, D = x.shape
    F = wg.shape[1]
    emit_next = next_gain is not None
    row = lambda i, f: (i, 0)
    fixed = lambda i, f: (0, 0)
    in_specs = [
        pl.BlockSpec((tm, D), row),
        pl.BlockSpec((1, D), fixed),
        pl.BlockSpec((D, tf), lambda i, f: (0, f)),
        pl.BlockSpec((D, tf), lambda i, f: (0, f)),
        pl.BlockSpec((tf, D), lambda i, f: (f, 0)),
    ]
    args = [x, gain, wg, wu, wd]
    out_shape = [jax.ShapeDtypeStruct((S, D), F32)]
    out_specs = [pl.BlockSpec((tm, D), row)]
    if emit_next:
        in_specs.append(pl.BlockSpec((1, D), fixed))
        args.append(next_gain)
        out_shape.append(jax.ShapeDtypeStruct((S, D), BF16))
        out_specs.append(pl.BlockSpec((tm, D), row))
    res = pl.pallas_call(
        functools.partial(_ffn_kernel, emit_next=emit_next),
        grid=(S // tm, F // tf),
        in_specs=in_specs,
        out_specs=out_specs,
        out_shape=out_shape,
        scratch_shapes=[pltpu.VMEM((tm, D), BF16)],
        compiler_params=_params(("parallel", "arbitrary")),
        name="ffn_next" if emit_next else "ffn",
    )(*args)
    return res if emit_next else res[0]


def _proj_kernel(a_ref, w_ref, o_ref):
    o_ref[...] = _dot_nt(a_ref[...], w_ref[...]).astype(o_ref.dtype)


def _proj(a, w, out_dtype, *, tm=1024, tn=1024):
    S, K = a.shape
    N = w.shape[0]
    return pl.pallas_call(
        _proj_kernel,
        grid=(S // tm, N // tn),
        in_specs=[pl.BlockSpec((tm, K), lambda i, j: (i, 0)),
                  pl.BlockSpec((tn, K), lambda i, j: (j, 0))],
        out_specs=pl.BlockSpec((tm, tn), lambda i, j: (i, j)),
        out_shape=jax.ShapeDtypeStruct((S, N), out_dtype),
        compiler_params=_params(("parallel", "arbitrary")),
        name="proj_" + jnp.dtype(out_dtype).name,
    )(a, w)


ROPE_GAP = A_HEAD_DIM // 2
HEAD_PERM = np.concatenate([np.arange(0, ROPE_HALF), np.arange(ROPE_DIM, ROPE_DIM + ROPE_GAP - ROPE_HALF),
                            np.arange(ROPE_HALF, ROPE_DIM), np.arange(ROPE_DIM + ROPE_GAP - ROPE_HALF, A_HEAD_DIM)])


def _proj_qk_kernel(a_ref, w_ref, g_ref, pos_ref, inv_ref, o_ref, raw):
    i = pl.program_id(0)
    nh = o_ref.shape[0]

    @pl.when(i == 0)
    def _():
        raw[1] = jnp.zeros(raw.shape[1:], raw.dtype)

    cur = i % 2
    acc = _dot_nt(a_ref[...], w_ref[...])
    for h in range(nh):
        raw[cur, h] = acc[:, h * A_HEAD_DIM:(h + 1) * A_HEAD_DIM]

    group = LANES // R4
    p4 = pos_ref[...].astype(F32)
    tq = p4.shape[0]
    lane = lax.broadcasted_iota(jnp.int32, (tq, LANES), 1)
    pos_lanes = p4[:, R4 - 1:R4]
    for r in reversed(range(R4 - 1)):
        pos_lanes = jnp.where(lane < (r + 1) * group, p4[:, r:r + 1], pos_lanes)
    ang = pos_lanes * inv_ref[...]
    cos_all = jnp.cos(ang)
    sin_all = jnp.sin(ang)
    lo = lane < ROPE_HALF
    hi = jnp.logical_and(lane >= ROPE_GAP, lane < ROPE_GAP + ROPE_HALF)

    def to_head_lanes(t, r):
        shifts = ((LANES - r * group) % LANES, (ROPE_GAP - r * group) % LANES)
        return [pltpu.roll(t, s, 1) if s else t for s in shifts]

    prev = raw.at[1 - cur]
    gain = jnp.stack([g_ref[:, h * A_HEAD_DIM:(h + 1) * A_HEAD_DIM] for h in range(nh)], axis=0)
    for r in range(R4):
        rows = pl.ds(r, tq, stride=R4)
        c1, c2 = to_head_lanes(cos_all, r)
        s1, s2 = to_head_lanes(sin_all, r)
        cosf = jnp.where(lo, c1, jnp.where(hi, c2, 1.0))
        sinf = jnp.where(lo, -s1, jnp.where(hi, s2, 0.0))
        y = _rms(jnp.stack([prev[h, rows, :] for h in range(nh)], axis=0), gain)
        o_ref[:, r] = (y * cosf[None] + pltpu.roll(y, ROPE_GAP, 2) * sinf[None]).astype(o_ref.dtype)


def _proj_qk(a, w, gains, pos4, inv, *, tm=1024):
    S, K = a.shape
    N = w.shape[0]
    nt = S // tm
    lag = lambda i: jnp.maximum(i - 1, 0)
    return pl.pallas_call(
        _proj_qk_kernel,
        grid=(nt + 1,),
        in_specs=[pl.BlockSpec((tm, K), lambda i: (jnp.minimum(i, nt - 1), 0)),
                  pl.BlockSpec((N, K), lambda i: (0, 0), pipeline_mode=pl.Buffered(1)),
                  pl.BlockSpec((1, N), lambda i: (0, 0)),
                  pl.BlockSpec((tm // R4, R4), lambda i: (lag(i), 0)),
                  pl.BlockSpec((1, LANES), lambda i: (0, 0))],
        out_specs=pl.BlockSpec((N // A_HEAD_DIM, R4, tm // R4, A_HEAD_DIM), lambda i: (0, 0, lag(i), 0)),
        out_shape=jax.ShapeDtypeStruct((N // A_HEAD_DIM, R4, S // R4, A_HEAD_DIM), BF16),
        scratch_shapes=[pltpu.VMEM((2, N // A_HEAD_DIM, tm, A_HEAD_DIM), F32)],
        compiler_params=_params(("arbitrary",)),
        name="proj_qk",
    )(a, w, gains, pos4, inv)


def _chunk_log_decay(z, w2, bias):
    pre = _dot(z.astype(BF16), w2) + bias
    g = (jnp.minimum(pre, 0.0) - jnp.log(1.0 + jnp.exp(-jnp.abs(pre)))) / GATE_NORMALIZER
    r = lax.broadcasted_iota(jnp.int32, g.shape, 0) % GLA_CHUNK
    shift = 1
    while shift < GLA_CHUNK:
        g = g + jnp.where(r >= shift, pltpu.roll(g, shift, 0), 0.0)
        shift *= 2
    return g


def _proj_v_kernel(a_ref, w_ref, wz_ref, w2_ref, bias_ref, oa_ref, ob_ref, b_ref, raw):
    a = a_ref[...]
    b_ref[...] = _chunk_log_decay(_dot_nt(a, wz_ref[...]), w2_ref[...], bias_ref[...])
    heads_per_dot = MXU_N // A_HEAD_DIM
    for g in range(A_WIDTH // MXU_N):
        acc = _dot_nt(a, w_ref[g * MXU_N:(g + 1) * MXU_N, :])
        for hh in range(heads_per_dot):
            raw[g * heads_per_dot + hh] = acc[:, hh * A_HEAD_DIM:(hh + 1) * A_HEAD_DIM]
    for g in range(B_VAL_WIDTH // MXU_N):
        cols = slice(g * MXU_N, (g + 1) * MXU_N)
        ob_ref[:, cols] = _dot_nt(a, w_ref[A_WIDTH + g * MXU_N:A_WIDTH + (g + 1) * MXU_N, :]).astype(ob_ref.dtype)
    tq = raw.shape[1] // R4
    for h in range(A_HEADS):
        for r in range(R4):
            oa_ref[h, r] = raw[h, pl.ds(r, tq, stride=R4), :].astype(oa_ref.dtype)


def _proj_v(a, w, wz, w2, bias, *, tm=1024):
    S, K = a.shape
    fixed = lambda t: pl.BlockSpec(t.shape, lambda i: (0, 0))
    return pl.pallas_call(
        _proj_v_kernel,
        grid=(S // tm,),
        in_specs=[pl.BlockSpec((tm, K), lambda i: (i, 0)), fixed(w), fixed(wz), fixed(w2), fixed(bias)],
        out_specs=[pl.BlockSpec((A_HEADS, R4, tm // R4, A_HEAD_DIM), lambda i: (0, 0, i, 0)),
                   pl.BlockSpec((tm, B_VAL_WIDTH), lambda i: (i, 0)),
                   pl.BlockSpec((tm, B_KEY_WIDTH), lambda i: (i, 0))],
        out_shape=[jax.ShapeDtypeStruct((A_HEADS, R4, S // R4, A_HEAD_DIM), BF16),
                   jax.ShapeDtypeStruct((S, B_VAL_WIDTH), BF16),
                   jax.ShapeDtypeStruct((S, B_KEY_WIDTH), F32)],
        scratch_shapes=[pltpu.VMEM((A_HEADS, tm, A_HEAD_DIM), F32)],
        compiler_params=_params(("parallel",)),
        name="proj_v",
    )(a, w, wz, w2, bias)


A_BLK = 128
A_CHUNK = A_BLK * max(DILATIONS)
A_PLANE = A_CHUNK // R4


def _attn_kernel(q_ref, kp_ref, kc_ref, vp_ref, vc_ref, o_ref, qf, kf, vf, acc_s, m_s, l_s, o_nat, *, group):
    c = pl.program_id(1)
    first_chunk = (c == 0).astype(jnp.int32)
    scale = A_HEAD_DIM ** -0.5
    row = lax.broadcasted_iota(jnp.int32, (1, A_BLK, 2 * A_BLK), 1)
    col = lax.broadcasted_iota(jnp.int32, (1, A_BLK, 2 * A_BLK), 2)
    dist_strided = row + A_SPAN - col
    prev_strided = col < A_BLK
    qn, kn = A_BLK // R4, 2 * A_BLK // R4
    dist_mixed = (R4 * (row % qn) + row // qn) - (R4 * (col % kn) + col // kn) + A_SPAN
    prev_mixed = (col % kn) < kn // 2

    def rows_of(prev_ref, cur_ref, plane, lo, n_rows):
        if lo >= 0:
            return cur_ref[plane, lo:lo + n_rows, :]
        return jnp.concatenate([prev_ref[plane, A_PLANE + lo:A_PLANE, :], cur_ref[plane, 0:lo + n_rows, :]], axis=0)

    def run_group(mixed, tiles):
        dist = dist_mixed if mixed else dist_strided
        band = jnp.logical_and(dist >= 0, dist <= A_SPAN)
        s = jnp.concatenate([_dot_nt(t[0], t[1]) for t in tiles], axis=0).reshape(len(tiles), A_BLK, 2 * A_BLK)
        s = jnp.where(band, s * scale, -jnp.inf)
        firsts = [gi for gi, t in enumerate(tiles) if t[3]]
        if firsts:
            gidx = lax.broadcasted_iota(jnp.int32, (len(tiles), 1, 1), 0)
            is_first = functools.reduce(jnp.logical_or, [gidx == gi for gi in firsts]).astype(jnp.int32) * first_chunk
            s = jnp.where(jnp.logical_and(prev_mixed if mixed else prev_strided, is_first > 0), -jnp.inf, s)
        m = jnp.max(s, axis=-1, keepdims=True)
        p = jnp.exp(s - m)
        l = jnp.sum(p, axis=-1, keepdims=True)
        pb = p.astype(BF16)
        mb = jnp.broadcast_to(m, (len(tiles), A_BLK, A_HEAD_DIM))
        lb = jnp.broadcast_to(l, (len(tiles), A_BLK, A_HEAD_DIM))
        for gi, t in enumerate(tiles):
            t[4](_dot(pb[gi], t[2]), mb[gi], lb[gi])

    def grouped(mixed, tiles):
        for g0 in range(0, len(tiles), group):
            run_group(mixed, tiles[g0:g0 + group])

    tiles = []
    for n in range(A_CHUNK // A_BLK):
        q = jnp.concatenate([q_ref[r, qn * n:qn * (n + 1), :] for r in range(R4)], axis=0)
        k = jnp.concatenate([rows_of(kp_ref, kc_ref, r, qn * (n - 1), kn) for r in range(R4)], axis=0)
        v = jnp.concatenate([rows_of(vp_ref, vc_ref, r, qn * (n - 1), kn) for r in range(R4)], axis=0)

        def store(acc, mb, lb, n=n):
            for r in range(R4):
                dst = slice(qn * n, qn * (n + 1))
                src = slice(qn * r, qn * (r + 1))
                acc_s[0, r, dst, :] = acc[src]
                m_s[0, r, dst, :] = mb[src]
                l_s[0, r, dst, :] = lb[src]
        tiles.append((q, k, v, n == 0, store))
    grouped(True, tiles)

    tiles = []
    for r in range(R4):
        for n in range(A_PLANE // A_BLK):
            q = q_ref[r, A_BLK * n:A_BLK * (n + 1), :]
            k = rows_of(kp_ref, kc_ref, r, A_BLK * (n - 1), 2 * A_BLK)
            v = rows_of(vp_ref, vc_ref, r, A_BLK * (n - 1), 2 * A_BLK)

            def store(acc, mb, lb, r=r, n=n):
                dst = slice(A_BLK * n, A_BLK * (n + 1))
                acc_s[1, r, dst, :] = acc
                m_s[1, r, dst, :] = mb
                l_s[1, r, dst, :] = lb
            tiles.append((q, k, v, n == 0, store))
    grouped(False, tiles)

    qf[...] = q_ref[...].astype(F32)
    kf[:, 0:A_PLANE, :] = kp_ref[...].astype(F32)
    kf[:, A_PLANE:2 * A_PLANE, :] = kc_ref[...].astype(F32)
    vf[:, 0:A_PLANE, :] = vp_ref[...].astype(F32)
    vf[:, A_PLANE:2 * A_PLANE, :] = vc_ref[...].astype(F32)
    tiles = []
    for r in range(R4):
        for g in range(R4):
            own = pl.ds(g, A_BLK, stride=R4)
            both = pl.ds(g, 2 * A_BLK, stride=R4)

            def store(acc, mb, lb, r=r, own=own):
                acc_s[2, r, own, :] = acc
                m_s[2, r, own, :] = mb
                l_s[2, r, own, :] = lb
            tiles.append((qf[r, own, :].astype(BF16), kf[r, both, :].astype(BF16), vf[r, both, :].astype(BF16),
                          True, store))
    grouped(False, tiles)

    ms = [m_s[pi] for pi in range(len(DILATIONS))]
    m = functools.reduce(jnp.maximum, ms)
    ws = [jnp.exp(mi - m) for mi in ms]
    num = sum(w * acc_s[pi] for pi, w in enumerate(ws))
    den = sum(w * l_s[pi] for pi, w in enumerate(ws))
    o = num / den
    for r in range(R4):
        o_nat[pl.ds(r, A_PLANE, stride=R4), :] = o[r]
    o_ref[...] = o_nat[...].astype(o_ref.dtype)


def _attn(qk, v, *, group=8):
    S = v.shape[2] * R4
    npat = len(DILATIONS)
    blk = (None, R4, A_PLANE, A_HEAD_DIM)
    prev = lambda c: jnp.maximum(c - 1, 0)
    return pl.pallas_call(
        functools.partial(_attn_kernel, group=group),
        grid=(A_HEADS, S // A_CHUNK),
        in_specs=[pl.BlockSpec(blk, lambda h, c: (h, 0, c, 0)),
                  pl.BlockSpec(blk, lambda h, c: (A_HEADS + h, 0, prev(c), 0)),
                  pl.BlockSpec(blk, lambda h, c: (A_HEADS + h, 0, c, 0)),
                  pl.BlockSpec(blk, lambda h, c: (h, 0, prev(c), 0)),
                  pl.BlockSpec(blk, lambda h, c: (h, 0, c, 0))],
        out_specs=pl.BlockSpec((A_CHUNK, A_HEAD_DIM), lambda h, c: (c, h)),
        out_shape=jax.ShapeDtypeStruct((S, A_WIDTH), BF16),
        scratch_shapes=[pltpu.VMEM((R4, A_PLANE, A_HEAD_DIM), F32),
                        pltpu.VMEM((R4, 2 * A_PLANE, A_HEAD_DIM), F32),
                        pltpu.VMEM((R4, 2 * A_PLANE, A_HEAD_DIM), F32),
                        pltpu.VMEM((npat, R4, A_PLANE, A_HEAD_DIM), F32),
                        pltpu.VMEM((npat, R4, A_PLANE, A_HEAD_DIM), F32),
                        pltpu.VMEM((npat, R4, A_PLANE, A_HEAD_DIM), F32),
                        pltpu.VMEM((A_CHUNK, A_HEAD_DIM), F32)],
        compiler_params=_params(("parallel", "arbitrary")),
        name="attn",
    )(qk, qk, qk, v, v)


def _gla_kernel(q_ref, k_ref, v_ref, b_ref, r_ref, gn_ref, o_ref, st_ref, *, nchunk):
    C, Cs = GLA_CHUNK, GLA_SUB
    NS = C // Cs

    @pl.when(pl.program_id(0) == 0)
    def _():
        st_ref[...] = jnp.zeros_like(st_ref)

    row = lax.broadcasted_iota(jnp.int32, (C, C), 0)
    col = lax.broadcasted_iota(jnp.int32, (C, C), 1)
    sub_start = (row // Cs) * Cs
    sub_row = lax.broadcasted_iota(jnp.int32, (NS, Cs, B_KEY_DIM), 1)
    levels = []
    z = C // 2
    while z >= Cs:
        levels.append((z, jnp.logical_and((row // z) % 2 == 1, col // z == row // z - 1)))
        z //= 2
    diag_masks = [col == sub_start + j for j in range(Cs)]
    for c in range(nchunk):
        rows = slice(c * C, (c + 1) * C)
        for h in range(B_HEADS):
            kcols = slice(h * B_KEY_DIM, (h + 1) * B_KEY_DIM)
            vcols = slice(h * B_VAL_DIM, (h + 1) * B_VAL_DIM)
            q = q_ref[rows, kcols] * (B_KEY_DIM ** -0.5)
            k = k_ref[rows, kcols]
            b = b_ref[rows, kcols]
            v = v_ref[rows, vcols]
            b_last = b[C - 1:C, :]
            st = st_ref[h]
            o = _dot_nt((q * jnp.exp(b)).astype(BF16), st.astype(BF16))
            k_end = (k * jnp.exp(b_last - b)).astype(BF16)
            st_ref[h] = st * jnp.exp(b_last) + _dot_tn(v, k_end)

            a = jnp.zeros((C, C), F32)
            for z, mask in levels:
                bz = b.reshape(C // z, z, B_KEY_DIM)
                ends = bz[:, z - 1:z, :]
                starts = jnp.concatenate([jnp.zeros_like(ends[:1]), ends[:-1]], axis=0)
                q_z = (q * jnp.exp(bz - starts).reshape(C, B_KEY_DIM)).astype(BF16)
                k_z = (k * jnp.exp(ends - bz).reshape(C, B_KEY_DIM)).astype(BF16)
                a = jnp.where(mask, _dot_nt(q_z, k_z), a)

            q3 = q.reshape(NS, Cs, B_KEY_DIM)
            k3 = k.reshape(NS, Cs, B_KEY_DIM)
            b3 = b.reshape(NS, Cs, B_KEY_DIM)
            for j in range(Cs):
                kb = jnp.broadcast_to(k3[:, j:j + 1, :], q3.shape)
                bb = jnp.broadcast_to(b3[:, j:j + 1, :], q3.shape)
                w = jnp.exp(jnp.where(sub_row >= j, b3 - bb, -jnp.inf))
                dj = jnp.sum((q3 * kb * w).reshape(C, B_KEY_DIM), axis=-1, keepdims=True)
                a = jnp.where(diag_masks[j], dj, a)
            o = o + _dot(a.astype(BF16), v)
            r = r_ref[rows, vcols]
            o_ref[rows, vcols] = (_rms(o, gn_ref[...]) * (r * jax.nn.sigmoid(r))).astype(o_ref.dtype)


def _gla(pf, bv, b, gain, *, rb=512):
    S = pf.shape[0]
    return pl.pallas_call(
        functools.partial(_gla_kernel, nchunk=rb // GLA_CHUNK),
        grid=(S // rb,),
        in_specs=[pl.BlockSpec((rb, B_KEY_WIDTH), lambda i: (i, 0)),
                  pl.BlockSpec((rb, B_KEY_WIDTH), lambda i: (i, 1)),
                  pl.BlockSpec((rb, B_VAL_WIDTH), lambda i: (i, 0)),
                  pl.BlockSpec((rb, B_KEY_WIDTH), lambda i: (i, 0)),
                  pl.BlockSpec((rb, B_VAL_WIDTH), lambda i: (i, 1)),
                  pl.BlockSpec((1, B_VAL_DIM), lambda i: (0, 0))],
        out_specs=pl.BlockSpec((rb, B_VAL_WIDTH), lambda i: (i, 0)),
        out_shape=jax.ShapeDtypeStruct((S, B_VAL_WIDTH), BF16),
        scratch_shapes=[pltpu.VMEM((B_HEADS, B_VAL_DIM, B_KEY_DIM), F32)],
        compiler_params=_params(("arbitrary",)),
        name="gla",
    )(pf, pf, bv, b, pf, gain)


def _merge_out_kernel(x_ref, oa_ref, ob_ref, ga_ref, gb_ref, wa_ref, wb_ref, wo_ref, o_ref):
    j = pl.program_id(1)
    tn = ga_ref.shape[1]
    cols = pl.ds(pl.multiple_of(j * tn, tn), tn)

    @pl.when(j == 0)
    def _():
        o_ref[...] = x_ref[...]

    ya = _dot(oa_ref[...], wa_ref[:, cols])
    yb = _dot(ob_ref[...], wb_ref[:, cols])
    y = jax.nn.sigmoid(ga_ref[...]) * ya + jax.nn.sigmoid(gb_ref[...]) * yb
    o_ref[...] += _dot(y.astype(BF16), wo_ref[cols, :])


def _merge_out(x, oa, ob, pf, wa, wb, wo, *, tm=512, tn=1024):
    S, D = x.shape
    ga_blk = 2048 // tn
    gb_blk = 4096 // tn
    row = lambda i, j: (i, 0)
    resident = lambda w: pl.BlockSpec(w.shape, lambda i, j: (0, 0), pipeline_mode=pl.Buffered(1))
    return pl.pallas_call(
        _merge_out_kernel,
        grid=(S // tm, D // tn),
        in_specs=[pl.BlockSpec((tm, D), row),
                  pl.BlockSpec((tm, A_WIDTH), row),
                  pl.BlockSpec((tm, B_VAL_WIDTH), row),
                  pl.BlockSpec((tm, tn), lambda i, j: (i, ga_blk + j)),
                  pl.BlockSpec((tm, tn), lambda i, j: (i, gb_blk + j)),
                  resident(wa), resident(wb), resident(wo)],
        out_specs=pl.BlockSpec((tm, D), row),
        out_shape=jax.ShapeDtypeStruct((S, D), F32),
        compiler_params=_params(("parallel", "arbitrary")),
        name="merge_out",
    )(x, oa, ob, pf, pf, wa, wb, wo)


IN_COLS = np.cumsum([0, A_WIDTH, A_WIDTH, A_WIDTH, B_KEY_WIDTH, B_KEY_WIDTH, B_VAL_WIDTH, B_VAL_WIDTH,
                     GATE_RANK, D_MODEL, D_MODEL]).tolist()


def _w_in_kernel(w_ref, qk_ref, v_ref, f_ref, z_ref):
    c = IN_COLS

    def rows(lo, hi):
        return w_ref[lo:hi, :].astype(BF16)

    perm_runs = [(0, ROPE_HALF), (ROPE_DIM, ROPE_DIM + ROPE_GAP - ROPE_HALF), (ROPE_HALF, ROPE_DIM),
                 (ROPE_DIM + ROPE_GAP - ROPE_HALF, A_HEAD_DIM)]
    for h in range(2 * A_HEADS):
        dst = h * A_HEAD_DIM
        for lo, hi in perm_runs:
            qk_ref[dst:dst + hi - lo, :] = rows(h * A_HEAD_DIM + lo, h * A_HEAD_DIM + hi)
            dst += hi - lo
    v_ref[0:A_WIDTH, :] = rows(c[2], c[3])
    v_ref[A_WIDTH:, :] = rows(c[5], c[6])
    f_ref[0:2 * B_KEY_WIDTH, :] = rows(c[3], c[5])
    f_ref[2 * B_KEY_WIDTH:2 * B_KEY_WIDTH + B_VAL_WIDTH, :] = rows(c[6], c[7])
    f_ref[2 * B_KEY_WIDTH + B_VAL_WIDTH:, :] = rows(c[8], c[10])
    z_ref[0:GATE_RANK, :] = rows(c[7], c[8])
    z_ref[GATE_RANK:, :] = jnp.zeros((z_ref.shape[0] - GATE_RANK, z_ref.shape[1]), z_ref.dtype)


def _w_in_layout(w_t, l, *, tk=256):
    _, N, K = w_t.shape
    heights = (2 * A_WIDTH, A_WIDTH + B_VAL_WIDTH, 2 * B_KEY_WIDTH + B_VAL_WIDTH + 2 * D_MODEL, LANES)
    return pl.pallas_call(
        _w_in_kernel,
        grid=(K // tk,),
        in_specs=[pl.BlockSpec((None, N, tk), lambda i: (l, 0, i))],
        out_specs=[pl.BlockSpec((n, tk), lambda i: (0, i)) for n in heights],
        out_shape=[jax.ShapeDtypeStruct((n, K), BF16) for n in heights],
        compiler_params=_params(("parallel",)),
        name="w_in_layout",
    )(w_t)


def _layer(x, pos, p):
    x1, h = _ffn(x, p["ffn1_norm"], p["ffn1_wg"], p["ffn1_wu"], p["ffn1_wd"], p["mix_norm"], tf=256)
    qk = _proj_qk(h, p["w_qk"], p["qk_gain"], pos, p["rope_inv"])
    av, bv, b = _proj_v(h, p["w_v"], p["w_z"], p["w_2"], p["gate_bias"])
    pf = _proj(h, p["w_f"], F32)
    o_a = _attn(qk, av)
    o_b = _gla(pf, bv, b, p["b_out_norm"])
    x2 = _merge_out(x1, o_a, o_b, pf, p["w_a_up"], p["w_b_up"], p["w_out"])
    return _ffn(x2, p["ffn2_norm"], p["ffn2_wg"], p["ffn2_wu"], p["ffn2_wd"], tf=512)


def kernel(x, positions, ffn1_norm, ffn1_w_gate, ffn1_w_up, ffn1_w_down, mix_norm, w_in, a_q_norm, a_k_norm, b_gate_w2, b_gate_bias, b_out_norm, w_a_up, w_b_up, w_out, ffn2_norm, ffn2_w_gate, ffn2_w_up, ffn2_w_down):
    B, S, D = x.shape
    assert D == D_MODEL and S % A_CHUNK == 0
    depth = w_in.shape[0]
    w_in_t = jnp.swapaxes(w_in, 1, 2)
    inv = jnp.power(ROPE_THETA, -(jnp.arange(ROPE_HALF, dtype=F32) * 2.0 / ROPE_DIM))
    rope_inv = jnp.tile(inv, LANES // ROPE_HALF)[None, :]
    outs = []
    for bi in range(B):
        xb = x.reshape(S, D) if B == 1 else x[bi]
        pos = (positions if B == 1 else positions[bi]).reshape(S // R4, R4)
        for l in range(depth):
            w_qk, w_v, w_f, w_z = _w_in_layout(w_in_t, l)
            p = {
                "ffn1_norm": ffn1_norm[l][None, :], "mix_norm": mix_norm[l][None, :],
                "ffn2_norm": ffn2_norm[l][None, :],
                "ffn1_wg": ffn1_w_gate[l].astype(BF16), "ffn1_wu": ffn1_w_up[l].astype(BF16),
                "ffn1_wd": ffn1_w_down[l],
                "ffn2_wg": ffn2_w_gate[l].astype(BF16), "ffn2_wu": ffn2_w_up[l].astype(BF16),
                "ffn2_wd": ffn2_w_down[l].astype(BF16),
                "w_qk": w_qk, "w_v": w_v, "w_f": w_f, "w_z": w_z,
                "w_2": jnp.pad(b_gate_w2[l], ((0, LANES - GATE_RANK), (0, 0))).astype(BF16),
                "gate_bias": b_gate_bias[l][None, :],
                "qk_gain": jnp.concatenate([jnp.tile(a_q_norm[l][HEAD_PERM], A_HEADS),
                                            jnp.tile(a_k_norm[l][HEAD_PERM], A_HEADS)])[None, :],
                "rope_inv": rope_inv,
                "b_out_norm": b_out_norm[l][None, :],
                "w_a_up": w_a_up[l].astype(BF16), "w_b_up": w_b_up[l].astype(BF16), "w_out": w_out[l].astype(BF16),
            }
            xb = _layer(xb, pos, p)
        outs.append(xb)
    return outs[0].reshape(B, S, D) if B == 1 else jnp.stack(outs, axis=0)
```

```python
import functools

import jax
import jax.numpy as jnp
import numpy as np
from jax import lax
from jax.experimental import pallas as pl
from jax.experimental.pallas import tpu as pltpu

F32 = jnp.float32
BF16 = jnp.bfloat16

D_MODEL = 2048
D_FF = 5632
RMS_EPS = 1e-6
ROPE_THETA = 500000.0
A_HEAD_DIM = 128
A_HEADS = 8
A_WIDTH = A_HEADS * A_HEAD_DIM
ROPE_DIM = A_HEAD_DIM // 4
ROPE_HALF = ROPE_DIM // 2
DILATIONS = (1, 4, 16)
A_SPAN = 128
R4 = 4
B_HEADS = 4
B_VAL_DIM = 256
B_KEY_DIM = 128
B_KEY_WIDTH = B_HEADS * B_KEY_DIM
B_VAL_WIDTH = B_HEADS * B_VAL_DIM
GATE_RANK = 16
GATE_NORMALIZER = 16.0
GLA_CHUNK = 64
GLA_SUB = 8
LANES = 128
MXU_N = 256

VMEM_LIMIT = 56 * 1024 * 1024


def _params(sem):
    return pltpu.CompilerParams(dimension_semantics=sem, vmem_limit_bytes=VMEM_LIMIT)


def _rms(x, g):
    return x * lax.rsqrt(jnp.mean(x * x, axis=-1, keepdims=True) + RMS_EPS) * g


def _dot(a, b):
    return jnp.dot(a, b, preferred_element_type=F32)


def _dot_nt(a, b):
    return lax.dot_general(a, b, (((1,), (1,)), ((), ())), preferred_element_type=F32)


def _dot_tn(a, b):
    return lax.dot_general(a, b, (((0,), (0,)), ((), ())), preferred_element_type=F32)


def _ffn_kernel(x_ref, g_ref, wg_ref, wu_ref, wd_ref, *rest, emit_next):
    if emit_next:
        gn_ref, o_ref, hn_ref, h_scr = rest
    else:
        o_ref, h_scr = rest
    f = pl.program_id(1)

    @pl.when(f == 0)
    def _():
        x = x_ref[...]
        h_scr[...] = _rms(x, g_ref[...]).astype(BF16)
        o_ref[...] = x

    h = h_scr[...]
    gate = _dot(h, wg_ref[...])
    up = _dot(h, wu_ref[...])
    act = (0.5 * (gate * jax.nn.sigmoid(gate)) * up).astype(BF16)
    o_ref[...] += _dot(act, wd_ref[...].astype(BF16))

    if emit_next:
        @pl.when(f == pl.num_programs(1) - 1)
        def _():
            hn_ref[...] = _rms(o_ref[...], gn_ref[...]).astype(BF16)


def _ffn(x, gain, wg, wu, wd, next_gain=None, *, tm=1024, tf=256):
    S, D = x.shape
    F = wg.shape[1]
    emit_next = next_gain is not None
    row = lambda i, f: (i, 0)
    fixed = lambda i, f: (0, 0)
    in_specs = [
        pl.BlockSpec((tm, D), row),
        pl.BlockSpec((1, D), fixed),
        pl.BlockSpec((D, tf), lambda i, f: (0, f)),
        pl.BlockSpec((D, tf), lambda i, f: (0, f)),
        pl.BlockSpec((tf, D), lambda i, f: (f, 0)),
    ]
    args = [x, gain, wg, wu, wd]
    out_shape = [jax.ShapeDtypeStruct((S, D), F32)]
    out_specs = [pl.BlockSpec((tm, D), row)]
    if emit_next:
        in_specs.append(pl.BlockSpec((1, D), fixed))
        args.append(next_gain)
        out_shape.append(jax.ShapeDtypeStruct((S, D), BF16))
        out_specs.append(pl.BlockSpec((tm, D), row))
    res = pl.pallas_call(
        functools.partial(_ffn_kernel, emit_next=emit_next),
        grid=(S // tm, F // tf),
        in_specs=in_specs,
        out_specs=out_specs,
        out_shape=out_shape,
        scratch_shapes=[pltpu.VMEM((tm, D), BF16)],
        compiler_params=_params(("parallel", "arbitrary")),
        name="ffn_next" if emit_next else "ffn",
    )(*args)
    return res if emit_next else res[0]


def _proj_kernel(a_ref, w_ref, o_ref):
    o_ref[...] = _dot_nt(a_ref[...], w_ref[...]).astype(o_ref.dtype)


def _proj(a, w, out_dtype, *, tm=1024, tn=1536):
    S, K = a.shape
    N = w.shape[0]
    return pl.pallas_call(
        _proj_kernel,
        grid=(S // tm, N // tn),
        in_specs=[pl.BlockSpec((tm, K), lambda i, j: (i, 0)),
                  pl.BlockSpec((tn, K), lambda i, j: (j, 0))],
        out_specs=pl.BlockSpec((tm, tn), lambda i, j: (i, j)),
        out_shape=jax.ShapeDtypeStruct((S, N), out_dtype),
        compiler_params=_params(("parallel", "arbitrary")),
        name="proj_" + jnp.dtype(out_dtype).name,
    )(a, w)


ROPE_GAP = A_HEAD_DIM // 2
HEAD_PERM = np.concatenate([np.arange(0, ROPE_HALF), np.arange(ROPE_DIM, ROPE_DIM + ROPE_GAP - ROPE_HALF),
                            np.arange(ROPE_HALF, ROPE_DIM), np.arange(ROPE_DIM + ROPE_GAP - ROPE_HALF, A_HEAD_DIM)])


def _proj_qk_kernel(a_ref, w_ref, g_ref, pos_ref, inv_ref, o_ref, raw):
    i = pl.program_id(0)
    nh = o_ref.shape[0]

    @pl.when(i == 0)
    def _():
        raw[1] = jnp.zeros(raw.shape[1:], raw.dtype)

    cur = i % 2
    acc = _dot_nt(a_ref[...], w_ref[...])
    for h in range(nh):
        raw[cur, h] = acc[:, h * A_HEAD_DIM:(h + 1) * A_HEAD_DIM]

    group = LANES // R4
    p4 = pos_ref[...].astype(F32)
    tq = p4.shape[0]
    lane = lax.broadcasted_iota(jnp.int32, (tq, LANES), 1)
    pos_lanes = p4[:, R4 - 1:R4]
    for r in reversed(range(R4 - 1)):
        pos_lanes = jnp.where(lane < (r + 1) * group, p4[:, r:r + 1], pos_lanes)
    ang = pos_lanes * inv_ref[...]
    cos_all = jnp.cos(ang)
    sin_all = jnp.sin(ang)
    lo = lane < ROPE_HALF
    hi = jnp.logical_and(lane >= ROPE_GAP, lane < ROPE_GAP + ROPE_HALF)

    def to_head_lanes(t, r):
        shifts = ((LANES - r * group) % LANES, (ROPE_GAP - r * group) % LANES)
        return [pltpu.roll(t, s, 1) if s else t for s in shifts]

    prev = raw.at[1 - cur]
    gain = jnp.stack([g_ref[:, h * A_HEAD_DIM:(h + 1) * A_HEAD_DIM] for h in range(nh)], axis=0)
    for r in range(R4):
        rows = pl.ds(r, tq, stride=R4)
        c1, c2 = to_head_lanes(cos_all, r)
        s1, s2 = to_head_lanes(sin_all, r)
        cosf = jnp.where(lo, c1, jnp.where(hi, c2, 1.0))
        sinf = jnp.where(lo, -s1, jnp.where(hi, s2, 0.0))
        y = _rms(jnp.stack([prev[h, rows, :] for h in range(nh)], axis=0), gain)
        o_ref[:, r] = (y * cosf[None] + pltpu.roll(y, ROPE_GAP, 2) * sinf[None]).astype(o_ref.dtype)


def _proj_qk(a, w, gains, pos4, inv, *, tm=1024):
    S, K = a.shape
    N = w.shape[0]
    nt = S // tm
    lag = lambda i: jnp.maximum(i - 1, 0)
    return pl.pallas_call(
        _proj_qk_kernel,
        grid=(nt + 1,),
        in_specs=[pl.BlockSpec((tm, K), lambda i: (jnp.minimum(i, nt - 1), 0)),
                  pl.BlockSpec((N, K), lambda i: (0, 0), pipeline_mode=pl.Buffered(1)),
                  pl.BlockSpec((1, N), lambda i: (0, 0)),
                  pl.BlockSpec((tm // R4, R4), lambda i: (lag(i), 0)),
                  pl.BlockSpec((1, LANES), lambda i: (0, 0))],
        out_specs=pl.BlockSpec((N // A_HEAD_DIM, R4, tm // R4, A_HEAD_DIM), lambda i: (0, 0, lag(i), 0)),
        out_shape=jax.ShapeDtypeStruct((N // A_HEAD_DIM, R4, S // R4, A_HEAD_DIM), BF16),
        scratch_shapes=[pltpu.VMEM((2, N // A_HEAD_DIM, tm, A_HEAD_DIM), F32)],
        compiler_params=_params(("arbitrary",)),
        name="proj_qk",
    )(a, w, gains, pos4, inv)


def _chunk_log_decay(z, w2, bias):
    pre = _dot(z.astype(BF16), w2) + bias
    g = (jnp.minimum(pre, 0.0) - jnp.log(1.0 + jnp.exp(-jnp.abs(pre)))) / GATE_NORMALIZER
    r = lax.broadcasted_iota(jnp.int32, g.shape, 0) % GLA_CHUNK
    shift = 1
    while shift < GLA_CHUNK:
        g = g + jnp.where(r >= shift, pltpu.roll(g, shift, 0), 0.0)
        shift *= 2
    return g


def _proj_v_kernel(a_ref, w_ref, wz_ref, w2_ref, bias_ref, oa_ref, ob_ref, b_ref, raw):
    a = a_ref[...]
    b_ref[...] = _chunk_log_decay(_dot_nt(a, wz_ref[...]), w2_ref[...], bias_ref[...])
    heads_per_dot = MXU_N // A_HEAD_DIM
    for g in range(A_WIDTH // MXU_N):
        acc = _dot_nt(a, w_ref[g * MXU_N:(g + 1) * MXU_N, :])
        for hh in range(heads_per_dot):
            raw[g * heads_per_dot + hh] = acc[:, hh * A_HEAD_DIM:(hh + 1) * A_HEAD_DIM]
    for g in range(B_VAL_WIDTH // MXU_N):
        cols = slice(g * MXU_N, (g + 1) * MXU_N)
        ob_ref[:, cols] = _dot_nt(a, w_ref[A_WIDTH + g * MXU_N:A_WIDTH + (g + 1) * MXU_N, :]).astype(ob_ref.dtype)
    tq = raw.shape[1] // R4
    for h in range(A_HEADS):
        for r in range(R4):
            oa_ref[h, r] = raw[h, pl.ds(r, tq, stride=R4), :].astype(oa_ref.dtype)


def _proj_v(a, w, wz, w2, bias, *, tm=1024):
    S, K = a.shape
    fixed = lambda t: pl.BlockSpec(t.shape, lambda i: (0, 0))
    return pl.pallas_call(
        _proj_v_kernel,
        grid=(S // tm,),
        in_specs=[pl.BlockSpec((tm, K), lambda i: (i, 0)), fixed(w), fixed(wz), fixed(w2), fixed(bias)],
        out_specs=[pl.BlockSpec((A_HEADS, R4, tm // R4, A_HEAD_DIM), lambda i: (0, 0, i, 0)),
                   pl.BlockSpec((tm, B_VAL_WIDTH), lambda i: (i, 0)),
                   pl.BlockSpec((tm, B_KEY_WIDTH), lambda i: (i, 0))],
        out_shape=[jax.ShapeDtypeStruct((A_HEADS, R4, S // R4, A_HEAD_DIM), BF16),
                   jax.ShapeDtypeStruct((S, B_VAL_WIDTH), BF16),
                   jax.ShapeDtypeStruct((S, B_KEY_WIDTH), F32)],
        scratch_shapes=[pltpu.VMEM((A_HEADS, tm, A_HEAD_DIM), F32)],
        compiler_params=_params(("parallel",)),
        name="proj_v",
    )(a, w, wz, w2, bias)


A_BLK = 128
A_CHUNK = A_BLK * max(DILATIONS)
A_PLANE = A_CHUNK // R4


def _attn_kernel(q_ref, kp_ref, kc_ref, vp_ref, vc_ref, o_ref, qf, kf, vf, acc_s, m_s, l_s, o_nat, *, group):
    c = pl.program_id(1)
    first_chunk = (c == 0).astype(jnp.int32)
    scale = A_HEAD_DIM ** -0.5
    row = lax.broadcasted_iota(jnp.int32, (1, A_BLK, 2 * A_BLK), 1)
    col = lax.broadcasted_iota(jnp.int32, (1, A_BLK, 2 * A_BLK), 2)
    dist_strided = row + A_SPAN - col
    prev_strided = col < A_BLK
    qn, kn = A_BLK // R4, 2 * A_BLK // R4
    dist_mixed = (R4 * (row % qn) + row // qn) - (R4 * (col % kn) + col // kn) + A_SPAN
    prev_mixed = (col % kn) < kn // 2

    def rows_of(prev_ref, cur_ref, plane, lo, n_rows):
        if lo >= 0:
            return cur_ref[plane, lo:lo + n_rows, :]
        return jnp.concatenate([prev_ref[plane, A_PLANE + lo:A_PLANE, :], cur_ref[plane, 0:lo + n_rows, :]], axis=0)

    def run_group(mixed, tiles):
        dist = dist_mixed if mixed else dist_strided
        band = jnp.logical_and(dist >= 0, dist <= A_SPAN)
        s = jnp.concatenate([_dot_nt(t[0], t[1]) for t in tiles], axis=0).reshape(len(tiles), A_BLK, 2 * A_BLK)
        s = jnp.where(band, s * scale, -jnp.inf)
        firsts = [gi for gi, t in enumerate(tiles) if t[3]]
        if firsts:
            gidx = lax.broadcasted_iota(jnp.int32, (len(tiles), 1, 1), 0)
            is_first = functools.reduce(jnp.logical_or, [gidx == gi for gi in firsts]).astype(jnp.int32) * first_chunk
            s = jnp.where(jnp.logical_and(prev_mixed if mixed else prev_strided, is_first > 0), -jnp.inf, s)
        m = jnp.max(s, axis=-1, keepdims=True)
        p = jnp.exp(s - m)
        l = jnp.sum(p, axis=-1, keepdims=True)
        pb = p.astype(BF16)
        mb = jnp.broadcast_to(m, (len(tiles), A_BLK, A_HEAD_DIM))
        lb = jnp.broadcast_to(l, (len(tiles), A_BLK, A_HEAD_DIM))
        for gi, t in enumerate(tiles):
            t[4](_dot(pb[gi], t[2]), mb[gi], lb[gi])

    def grouped(mixed, tiles):
        for g0 in range(0, len(tiles), group):
            run_group(mixed, tiles[g0:g0 + group])

    tiles = []
    for n in range(A_CHUNK // A_BLK):
        q = jnp.concatenate([q_ref[r, qn * n:qn * (n + 1), :] for r in range(R4)], axis=0)
        k = jnp.concatenate([rows_of(kp_ref, kc_ref, r, qn * (n - 1), kn) for r in range(R4)], axis=0)
        v = jnp.concatenate([rows_of(vp_ref, vc_ref, r, qn * (n - 1), kn) for r in range(R4)], axis=0)

        def store(acc, mb, lb, n=n):
            for r in range(R4):
                dst = slice(qn * n, qn * (n + 1))
                src = slice(qn * r, qn * (r + 1))
                acc_s[0, r, dst, :] = acc[src]
                m_s[0, r, dst, :] = mb[src]
                l_s[0, r, dst, :] = lb[src]
        tiles.append((q, k, v, n == 0, store))
    grouped(True, tiles)

    tiles = []
    for r in range(R4):
        for n in range(A_PLANE // A_BLK):
            q = q_ref[r, A_BLK * n:A_BLK * (n + 1), :]
            k = rows_of(kp_ref, kc_ref, r, A_BLK * (n - 1), 2 * A_BLK)
            v = rows_of(vp_ref, vc_ref, r, A_BLK * (n - 1), 2 * A_BLK)

            def store(acc, mb, lb, r=r, n=n):
                dst = slice(A_BLK * n, A_BLK * (n + 1))
                acc_s[1, r, dst, :] = acc
                m_s[1, r, dst, :] = mb
                l_s[1, r, dst, :] = lb
            tiles.append((q, k, v, n == 0, store))
    grouped(False, tiles)

    qf[...] = q_ref[...].astype(F32)
    kf[:, 0:A_PLANE, :] = kp_ref[...].astype(F32)
    kf[:, A_PLANE:2 * A_PLANE, :] = kc_ref[...].astype(F32)
    vf[:, 0:A_PLANE, :] = vp_ref[...].astype(F32)
    vf[:, A_PLANE:2 * A_PLANE, :] = vc_ref[...].astype(F32)
    tiles = []
    for r in range(R4):
        for g in range(R4):
            own = pl.ds(g, A_BLK, stride=R4)
            both = pl.ds(g, 2 * A_BLK, stride=R4)

            def store(acc, mb, lb, r=r, own=own):
                acc_s[2, r, own, :] = acc
                m_s[2, r, own, :] = mb
                l_s[2, r, own, :] = lb
            tiles.append((qf[r, own, :].astype(BF16), kf[r, both, :].astype(BF16), vf[r, both, :].astype(BF16),
                          True, store))
    grouped(False, tiles)

    ms = [m_s[pi] for pi in range(len(DILATIONS))]
    m = functools.reduce(jnp.maximum, ms)
    ws = [jnp.exp(mi - m) for mi in ms]
    num = sum(w * acc_s[pi] for pi, w in enumerate(ws))
    den = sum(w * l_s[pi] for pi, w in enumerate(ws))
    o = num / den
    for r in range(R4):
        o_nat[pl.ds(r, A_PLANE, stride=R4), :] = o[r]
    o_ref[...] = o_nat[...].astype(o_ref.dtype)


def _attn(qk, v, *, group=8):
    S = v.shape[2] * R4
    npat = len(DILATIONS)
    blk = (None, R4, A_PLANE, A_HEAD_DIM)
    prev = lambda c: jnp.maximum(c - 1, 0)
    return pl.pallas_call(
        functools.partial(_attn_kernel, group=group),
        grid=(A_HEADS, S // A_CHUNK),
        in_specs=[pl.BlockSpec(blk, lambda h, c: (h, 0, c, 0)),
                  pl.BlockSpec(blk, lambda h, c: (A_HEADS + h, 0, prev(c), 0)),
                  pl.BlockSpec(blk, lambda h, c: (A_HEADS + h, 0, c, 0)),
                  pl.BlockSpec(blk, lambda h, c: (h, 0, prev(c), 0)),
                  pl.BlockSpec(blk, lambda h, c: (h, 0, c, 0))],
        out_specs=pl.BlockSpec((A_CHUNK, A_HEAD_DIM), lambda h, c: (c, h)),
        out_shape=jax.ShapeDtypeStruct((S, A_WIDTH), BF16),
        scratch_shapes=[pltpu.VMEM((R4, A_PLANE, A_HEAD_DIM), F32),
                        pltpu.VMEM((R4, 2 * A_PLANE, A_HEAD_DIM), F32),
                        pltpu.VMEM((R4, 2 * A_PLANE, A_HEAD_DIM), F32),
                        pltpu.VMEM((npat, R4, A_PLANE, A_HEAD_DIM), F32),
                        pltpu.VMEM((npat, R4, A_PLANE, A_HEAD_DIM), F32),
                        pltpu.VMEM((npat, R4, A_PLANE, A_HEAD_DIM), F32),
                        pltpu.VMEM((A_CHUNK, A_HEAD_DIM), F32)],
        compiler_params=_params(("parallel", "arbitrary")),
        name="attn",
    )(qk, qk, qk, v, v)


def _gla_kernel(q_ref, k_ref, v_ref, b_ref, r_ref, gn_ref, o_ref, st_ref, *, nchunk):
    C, Cs = GLA_CHUNK, GLA_SUB
    NS = C // Cs

    @pl.when(pl.program_id(0) == 0)
    def _():
        st_ref[...] = jnp.zeros_like(st_ref)

    row = lax.broadcasted_iota(jnp.int32, (C, C), 0)
    col = lax.broadcasted_iota(jnp.int32, (C, C), 1)
    sub_start = (row // Cs) * Cs
    sub_row = lax.broadcasted_iota(jnp.int32, (NS, Cs, B_KEY_DIM), 1)
    levels = []
    z = C // 2
    while z >= Cs:
        levels.append((z, jnp.logical_and((row // z) % 2 == 1, col // z == row // z - 1)))
        z //= 2
    diag_masks = [col == sub_start + j for j in range(Cs)]
    for c in range(nchunk):
        rows = slice(c * C, (c + 1) * C)
        for h in range(B_HEADS):
            kcols = slice(h * B_KEY_DIM, (h + 1) * B_KEY_DIM)
            vcols = slice(h * B_VAL_DIM, (h + 1) * B_VAL_DIM)
            q = q_ref[rows, kcols] * (B_KEY_DIM ** -0.5)
            k = k_ref[rows, kcols]
            b = b_ref[rows, kcols]
            v = v_ref[rows, vcols]
            b_last = b[C - 1:C, :]
            st = st_ref[h]
            o = _dot_nt((q * jnp.exp(b)).astype(BF16), st.astype(BF16))
            k_end = (k * jnp.exp(b_last - b)).astype(BF16)
            st_ref[h] = st * jnp.exp(b_last) + _dot_tn(v, k_end)

            a = jnp.zeros((C, C), F32)
            for z, mask in levels:
                bz = b.reshape(C // z, z, B_KEY_DIM)
                ends = bz[:, z - 1:z, :]
                starts = jnp.concatenate([jnp.zeros_like(ends[:1]), ends[:-1]], axis=0)
                q_z = (q * jnp.exp(bz - starts).reshape(C, B_KEY_DIM)).astype(BF16)
                k_z = (k * jnp.exp(ends - bz).reshape(C, B_KEY_DIM)).astype(BF16)
                a = jnp.where(mask, _dot_nt(q_z, k_z), a)

            q3 = q.reshape(NS, Cs, B_KEY_DIM)
            k3 = k.reshape(NS, Cs, B_KEY_DIM)
            b3 = b.reshape(NS, Cs, B_KEY_DIM)
            for j in range(Cs):
                kb = jnp.broadcast_to(k3[:, j:j + 1, :], q3.shape)
                bb = jnp.broadcast_to(b3[:, j:j + 1, :], q3.shape)
                w = jnp.exp(jnp.where(sub_row >= j, b3 - bb, -jnp.inf))
                dj = jnp.sum((q3 * kb * w).reshape(C, B_KEY_DIM), axis=-1, keepdims=True)
                a = jnp.where(diag_masks[j], dj, a)
            o = o + _dot(a.astype(BF16), v)
            r = r_ref[rows, vcols]
            o_ref[rows, vcols] = (_rms(o, gn_ref[...]) * (r * jax.nn.sigmoid(r))).astype(o_ref.dtype)


def _gla(pf, bv, b, gain, *, rb=512):
    S = pf.shape[0]
    return pl.pallas_call(
        functools.partial(_gla_kernel, nchunk=rb // GLA_CHUNK),
        grid=(S // rb,),
        in_specs=[pl.BlockSpec((rb, B_KEY_WIDTH), lambda i: (i, 0)),
                  pl.BlockSpec((rb, B_KEY_WIDTH), lambda i: (i, 1)),
                  pl.BlockSpec((rb, B_VAL_WIDTH), lambda i: (i, 0)),
                  pl.BlockSpec((rb, B_KEY_WIDTH), lambda i: (i, 0)),
                  pl.BlockSpec((rb, B_VAL_WIDTH), lambda i: (i, 1)),
                  pl.BlockSpec((1, B_VAL_DIM), lambda i: (0, 0))],
        out_specs=pl.BlockSpec((rb, B_VAL_WIDTH), lambda i: (i, 0)),
        out_shape=jax.ShapeDtypeStruct((S, B_VAL_WIDTH), BF16),
        scratch_shapes=[pltpu.VMEM((B_HEADS, B_VAL_DIM, B_KEY_DIM), F32)],
        compiler_params=_params(("arbitrary",)),
        name="gla",
    )(pf, pf, bv, b, pf, gain)


def _merge_out_kernel(x_ref, oa_ref, ob_ref, ga_ref, gb_ref, wa_ref, wb_ref, wo_ref, o_ref):
    j = pl.program_id(1)
    tn = ga_ref.shape[1]
    cols = pl.ds(pl.multiple_of(j * tn, tn), tn)

    @pl.when(j == 0)
    def _():
        o_ref[...] = x_ref[...]

    ya = _dot(oa_ref[...], wa_ref[:, cols])
    yb = _dot(ob_ref[...], wb_ref[:, cols])
    y = jax.nn.sigmoid(ga_ref[...]) * ya + jax.nn.sigmoid(gb_ref[...]) * yb
    o_ref[...] += _dot(y.astype(BF16), wo_ref[cols, :])


def _merge_out(x, oa, ob, pf, wa, wb, wo, *, tm=512, tn=1024):
    S, D = x.shape
    ga_blk = 2048 // tn
    gb_blk = 4096 // tn
    row = lambda i, j: (i, 0)
    resident = lambda w: pl.BlockSpec(w.shape, lambda i, j: (0, 0), pipeline_mode=pl.Buffered(1))
    return pl.pallas_call(
        _merge_out_kernel,
        grid=(S // tm, D // tn),
        in_specs=[pl.BlockSpec((tm, D), row),
                  pl.BlockSpec((tm, A_WIDTH), row),
                  pl.BlockSpec((tm, B_VAL_WIDTH), row),
                  pl.BlockSpec((tm, tn), lambda i, j: (i, ga_blk + j)),
                  pl.BlockSpec((tm, tn), lambda i, j: (i, gb_blk + j)),
                  resident(wa), resident(wb), resident(wo)],
        out_specs=pl.BlockSpec((tm, D), row),
        out_shape=jax.ShapeDtypeStruct((S, D), F32),
        compiler_params=_params(("parallel", "arbitrary")),
        name="merge_out",
    )(x, oa, ob, pf, pf, wa, wb, wo)


IN_COLS = np.cumsum([0, A_WIDTH, A_WIDTH, A_WIDTH, B_KEY_WIDTH, B_KEY_WIDTH, B_VAL_WIDTH, B_VAL_WIDTH,
                     GATE_RANK, D_MODEL, D_MODEL]).tolist()


def _w_in_kernel(w_ref, qk_ref, v_ref, f_ref, z_ref):
    c = IN_COLS

    def rows(lo, hi):
        return w_ref[lo:hi, :].astype(BF16)

    perm_runs = [(0, ROPE_HALF), (ROPE_DIM, ROPE_DIM + ROPE_GAP - ROPE_HALF), (ROPE_HALF, ROPE_DIM),
                 (ROPE_DIM + ROPE_GAP - ROPE_HALF, A_HEAD_DIM)]
    for h in range(2 * A_HEADS):
        dst = h * A_HEAD_DIM
        for lo, hi in perm_runs:
            qk_ref[dst:dst + hi - lo, :] = rows(h * A_HEAD_DIM + lo, h * A_HEAD_DIM + hi)
            dst += hi - lo
    v_ref[0:A_WIDTH, :] = rows(c[2], c[3])
    v_ref[A_WIDTH:, :] = rows(c[5], c[6])
    f_ref[0:2 * B_KEY_WIDTH, :] = rows(c[3], c[5])
    f_ref[2 * B_KEY_WIDTH:2 * B_KEY_WIDTH + B_VAL_WIDTH, :] = rows(c[6], c[7])
    f_ref[2 * B_KEY_WIDTH + B_VAL_WIDTH:, :] = rows(c[8], c[10])
    z_ref[0:GATE_RANK, :] = rows(c[7], c[8])
    z_ref[GATE_RANK:, :] = jnp.zeros((z_ref.shape[0] - GATE_RANK, z_ref.shape[1]), z_ref.dtype)


def _w_in_layout(w_t, l, *, tk=256):
    _, N, K = w_t.shape
    heights = (2 * A_WIDTH, A_WIDTH + B_VAL_WIDTH, 2 * B_KEY_WIDTH + B_VAL_WIDTH + 2 * D_MODEL, LANES)
    return pl.pallas_call(
        _w_in_kernel,
        grid=(K // tk,),
        in_specs=[pl.BlockSpec((None, N, tk), lambda i: (l, 0, i))],
        out_specs=[pl.BlockSpec((n, tk), lambda i: (0, i)) for n in heights],
        out_shape=[jax.ShapeDtypeStruct((n, K), BF16) for n in heights],
        compiler_params=_params(("parallel",)),
        name="w_in_layout",
    )(w_t)


def _layer(x, pos, p):
    x1, h = _ffn(x, p["ffn1_norm"], p["ffn1_wg"], p["ffn1_wu"], p["ffn1_wd"], p["mix_norm"], tf=256)
    qk = _proj_qk(h, p["w_qk"], p["qk_gain"], pos, p["rope_inv"])
    av, bv, b = _proj_v(h, p["w_v"], p["w_z"], p["w_2"], p["gate_bias"])
    pf = _proj(h, p["w_f"], F32)
    o_a = _attn(qk, av)
    o_b = _gla(pf, bv, b, p["b_out_norm"])
    x2 = _merge_out(x1, o_a, o_b, pf, p["w_a_up"], p["w_b_up"], p["w_out"])
    return _ffn(x2, p["ffn2_norm"], p["ffn2_wg"], p["ffn2_wu"], p["ffn2_wd"], tf=512)


def kernel(x, positions, ffn1_norm, ffn1_w_gate, ffn1_w_up, ffn1_w_down, mix_norm, w_in, a_q_norm, a_k_norm, b_gate_w2, b_gate_bias, b_out_norm, w_a_up, w_b_up, w_out, ffn2_norm, ffn2_w_gate, ffn2_w_up, ffn2_w_down):
    B, S, D = x.shape
    assert D == D_MODEL and S % A_CHUNK == 0
    depth = w_in.shape[0]
    w_in_t = jnp.swapaxes(w_in, 1, 2)
    inv = jnp.power(ROPE_THETA, -(jnp.arange(ROPE_HALF, dtype=F32) * 2.0 / ROPE_DIM))
    rope_inv = jnp.tile(inv, LANES // ROPE_HALF)[None, :]
    outs = []
    for bi in range(B):
        xb = x.reshape(S, D) if B == 1 else x[bi]
        pos = (positions if B == 1 else positions[bi]).reshape(S // R4, R4)
        for l in range(depth):
            w_qk, w_v, w_f, w_z = _w_in_layout(w_in_t, l)
            p = {
                "ffn1_norm": ffn1_norm[l][None, :], "mix_norm": mix_norm[l][None, :],
                "ffn2_norm": ffn2_norm[l][None, :],
                "ffn1_wg": ffn1_w_gate[l].astype(BF16), "ffn1_wu": ffn1_w_up[l].astype(BF16),
                "ffn1_wd": ffn1_w_down[l],
                "ffn2_wg": ffn2_w_gate[l].astype(BF16), "ffn2_wu": ffn2_w_up[l].astype(BF16),
                "ffn2_wd": ffn2_w_down[l].astype(BF16),
                "w_qk": w_qk, "w_v": w_v, "w_f": w_f, "w_z": w_z,
                "w_2": jnp.pad(b_gate_w2[l], ((0, LANES - GATE_RANK), (0, 0))).astype(BF16),
                "gate_bias": b_gate_bias[l][None, :],
                "qk_gain": jnp.concatenate([jnp.tile(a_q_norm[l][HEAD_PERM], A_HEADS),
                                            jnp.tile(a_k_norm[l][HEAD_PERM], A_HEADS)])[None, :],
                "rope_inv": rope_inv,
                "b_out_norm": b_out_norm[l][None, :],
                "w_a_up": w_a_up[l].astype(BF16), "w_b_up": w_b_up[l].astype(BF16), "w_out": w_out[l].astype(BF16),
            }
            xb = _layer(xb, pos, p)
        outs.append(xb)
    return outs[0].reshape(B, S, D) if B == 1 else jnp.stack(outs, axis=0)
```
